```python
import jax
import jax.numpy as jnp
from jax import lax
import numpy as np

D_MODEL = 2048
BATCH = 4
SEQ = 8192
DEPTH = 1
DEC_BATCH = 8
DEC_SEQ = 16
PAST_LEN = 4096

CHUNK = 64
D_MIX = D_MODEL
C_CONV = D_MIX // 2
C_RWKV = D_MIX - C_CONV
HEAD_RWKV = 64
H_RWKV = C_RWKV // HEAD_RWKV
CONV_WIDTH = 31
R_DECAY = 64
R_ICLR = 64
R_GATE = 160
N_SHIFT = 3 * C_RWKV + R_DECAY + R_ICLR + R_GATE
P_IN = 2 * C_CONV + N_SHIFT
N_EXPERTS = 32
TOP_K = 4
D_FF = D_MODEL
SWIGLU_LIMIT = 7.0
SWIGLU_ALPHA = 1.702
MOE_BLOCK = 128
LN_EPS = 1e-5
GN_EPS = 64e-5
ALPHA = (2.0 * DEPTH) ** 0.25
BETA = (8.0 * DEPTH) ** -0.25

kernel_name = 'hybrid_conv_rwkv7_moe_stream_step'


def _layer_norm(x, g, b, eps=LN_EPS):
    xf = x.astype(jnp.float32)
    mu = jnp.mean(xf, axis=-1, keepdims=True)
    var = jnp.mean(jnp.square(xf - mu), axis=-1, keepdims=True)
    return ((xf - mu) * lax.rsqrt(var + eps) * g + b).astype(x.dtype)


def _wkv7_scan(state, r, decay, k, v, kk, a):
    def step(S, inp):
        r_t, w_t, k_t, v_t, kk_t, a_t = inp
        sa = jnp.einsum('bhvk,bhk->bhv', S, -kk_t)
        S = (S * w_t[:, :, None, :] + sa[..., None] * (kk_t * a_t)[:, :, None, :]
             + v_t[..., None] * k_t[:, :, None, :])
        return S, jnp.einsum('bhvk,bhk->bhv', S, r_t)
    xs = tuple(jnp.moveaxis(t, 1, 0) for t in (r, decay, k, v, kk, a))
    state, y = lax.scan(step, state, xs)
    return jnp.moveaxis(y, 0, 1), state


def _rwkv7_group(zs, state, lp):
    B, T, _ = zs.shape
    zf = zs.astype(jnp.float32)
    r, k, v, xw, xa, xg = jnp.split(
        zf, [C_RWKV, 2 * C_RWKV, 3 * C_RWKV, 3 * C_RWKV + R_DECAY, 3 * C_RWKV + R_DECAY + R_ICLR], axis=-1)
    w_log = -jax.nn.softplus(-(lp['rwkv_w0'] + jnp.tanh(xw) @ lp['rwkv_w2'])) - 0.5
    decay = jnp.exp(-jnp.exp(w_log))
    a = jax.nn.sigmoid(lp['rwkv_a0'] + xa @ lp['rwkv_a2'])
    g = jax.nn.sigmoid(xg) @ lp['rwkv_g2']
    heads = lambda t: t.reshape(B, T, H_RWKV, HEAD_RWKV)
    kk = heads(k * lp['rwkv_k_k'])
    kk = kk / jnp.maximum(jnp.sqrt(jnp.sum(kk * kk, axis=-1, keepdims=True)), 1e-12)
    k = k * (1.0 + (a - 1.0) * lp['rwkv_k_a'])
    r_h, k_h, v_h = heads(r), heads(k), heads(v)
    y, state_new = _wkv7_scan(state.astype(jnp.float32), r_h, heads(decay), k_h, v_h, kk, heads(a))
    mu = jnp.mean(y, axis=-1, keepdims=True)
    var = jnp.mean(jnp.square(y - mu), axis=-1, keepdims=True)
    y = ((y - mu) * lax.rsqrt(var + GN_EPS)).reshape(B, T, C_RWKV) * lp['rwkv_ln_g'] + lp['rwkv_ln_b']
    bonus = jnp.sum(r_h * k_h * lp['rwkv_r_k'], axis=-1, keepdims=True) * v_h
    y = (y + bonus.reshape(B, T, C_RWKV)) * g
    return y.astype(zs.dtype), state_new.astype(state.dtype)


def _mixer(x, conv_buf, shift_buf, wkv_state, lp):
    proj = jnp.einsum('btd,dp->btp', x, lp['w_in']) + lp['b_in']
    u_val, u_gate, z = jnp.split(proj, [C_CONV, 2 * C_CONV], axis=-1)
    u = u_val * jax.nn.sigmoid(u_gate)
    u_ext = jnp.concatenate([conv_buf.astype(u.dtype), u], axis=1)
    c = lax.conv_general_dilated(
        u_ext, lp['conv_w'][:, None, :].astype(u_ext.dtype), (1,), 'VALID',
        dimension_numbers=('NWC', 'WIO', 'NWC'), feature_group_count=C_CONV) + lp['conv_b']
    c = jax.nn.silu(_layer_norm(c, lp['conv_ln_g'], lp['conv_ln_b']))
    new_conv = u_ext[:, -(CONV_WIDTH - 1):]
    z_prev = jnp.concatenate([shift_buf.astype(z.dtype), z[:, :-1]], axis=1)
    zs = z + lp['mu_shift'] * (z_prev - z)
    new_shift = z[:, -1:]
    y_b, new_wkv = _rwkv7_group(zs, wkv_state, lp)
    mix = jnp.einsum('btc,cd->btd', jnp.concatenate([c, y_b], axis=-1), lp['w_out'])
    return mix, new_conv, new_shift, new_wkv


def _moe(x, lp):
    B, T, D = x.shape
    xt = x.reshape(B * T, D)
    n_tok = B * T
    logits = (xt @ lp['router_w'] + lp['router_b']).astype(jnp.float32)
    top_logit, top_idx = lax.top_k(logits, TOP_K)
    gates = jax.nn.softmax(top_logit, axis=-1)
    n_assign = n_tok * TOP_K
    flat_e = top_idx.reshape(-1)
    order = jnp.argsort(flat_e)
    sorted_e = flat_e[order]
    counts = jnp.bincount(flat_e, length=N_EXPERTS)
    padded = (counts + MOE_BLOCK - 1) // MOE_BLOCK * MOE_BLOCK
    seg_end = jnp.cumsum(padded)
    seg_start = seg_end - padded
    start = jnp.cumsum(counts) - counts
    dest = seg_start[sorted_e] + jnp.arange(n_assign) - start[sorted_e]
    n_rows = (n_assign + N_EXPERTS * (MOE_BLOCK - 1) + MOE_BLOCK - 1) // MOE_BLOCK * MOE_BLOCK
    n_blocks = n_rows // MOE_BLOCK
    row_tok = jnp.full((n_rows,), n_tok, jnp.int32).at[dest].set((order // TOP_K).astype(jnp.int32))
    row_gate = jnp.zeros((n_rows,), jnp.float32).at[dest].set(gates.reshape(-1)[order])
    block_exp = jnp.minimum(
        jnp.searchsorted(seg_end, jnp.arange(n_blocks) * MOE_BLOCK, side='right'), N_EXPERTS - 1)
    x_pad = jnp.concatenate([xt, jnp.zeros((1, D), xt.dtype)], axis=0)
    xb = x_pad[row_tok].reshape(n_blocks, MOE_BLOCK, D)
    w_gate, b_gate, w_up, b_up = lp['w_gate'], lp['b_gate'], lp['w_up'], lp['b_up']
    w_down, b_down = lp['w_down'], lp['b_down']

    def expert_block(args):
        xblk, e = args
        gate = jnp.minimum(xblk @ w_gate[e] + b_gate[e], SWIGLU_LIMIT)
        up = jnp.clip(xblk @ w_up[e] + b_up[e], -SWIGLU_LIMIT, SWIGLU_LIMIT)
        h = (up + 1.0) * gate * jax.nn.sigmoid(SWIGLU_ALPHA * gate)
        return h @ w_down[e] + b_down[e]

    yb = lax.map(expert_block, (xb, block_exp)).reshape(n_rows, D)
    y = jax.ops.segment_sum(yb * row_gate[:, None], row_tok, num_segments=n_tok + 1)[:n_tok]
    return y.reshape(B, T, D).astype(x.dtype)


def _layer(x, conv_buf, shift_buf, wkv_state, lp):
    mix, new_conv, new_shift, new_wkv = _mixer(x, conv_buf, shift_buf, wkv_state, lp)
    x = _layer_norm(ALPHA * x + mix, lp['ln1_g'], lp['ln1_b'])
    x = _layer_norm(ALPHA * x + _moe(x, lp), lp['ln2_g'], lp['ln2_b'])
    return x, new_conv, new_shift, new_wkv


def setup_inputs(seed: int = 0) -> dict:
    key = jax.random.key(seed)
    ks = jax.random.split(key, 40)
    nrm = lambda k, shape, s: s * jax.random.normal(k, shape, jnp.float32)
    L = DEPTH
    return {
        'x_prompt': nrm(ks[0], (BATCH, SEQ, D_MODEL), 1.0),
        'x_sample': nrm(ks[1], (DEC_BATCH, DEC_SEQ, D_MODEL), 1.0),
        'state_conv': nrm(ks[2], (L, DEC_BATCH, CONV_WIDTH - 1, C_CONV), 0.5),
        'state_shift': nrm(ks[3], (L, DEC_BATCH, 1, N_SHIFT), 1.0),
        'state_wkv': nrm(ks[4], (L, DEC_BATCH, H_RWKV, HEAD_RWKV, HEAD_RWKV), 0.3),
        'w_in': nrm(ks[5], (L, D_MODEL, P_IN), D_MODEL ** -0.5),
        'b_in': nrm(ks[6], (L, P_IN), 0.02),
        'mu_shift': jax.random.uniform(ks[7], (L, N_SHIFT), jnp.float32),
        'conv_w': nrm(ks[8], (L, CONV_WIDTH, C_CONV), CONV_WIDTH ** -0.5),
        'conv_b': nrm(ks[9], (L, C_CONV), 0.02),
        'conv_ln_g': 1.0 + nrm(ks[10], (L, C_CONV), 0.05),
        'conv_ln_b': nrm(ks[11], (L, C_CONV), 0.02),
        'rwkv_w0': jax.random.uniform(ks[12], (L, C_RWKV), jnp.float32, -5.0, 1.0),
        'rwkv_w2': nrm(ks[13], (L, R_DECAY, C_RWKV), 0.1),
        'rwkv_a0': nrm(ks[14], (L, C_RWKV), 0.1),
        'rwkv_a2': nrm(ks[15], (L, R_ICLR, C_RWKV), 0.1),
        'rwkv_g2': nrm(ks[16], (L, R_GATE, C_RWKV), R_GATE ** -0.5),
        'rwkv_k_k': 0.85 + nrm(ks[17], (L, C_RWKV), 0.05),
        'rwkv_k_a': 1.0 + nrm(ks[18], (L, C_RWKV), 0.05),
        'rwkv_r_k': nrm(ks[19], (L, H_RWKV, HEAD_RWKV), 0.1),
        'rwkv_ln_g': 1.0 + nrm(ks[20], (L, C_RWKV), 0.05),
        'rwkv_ln_b': nrm(ks[21], (L, C_RWKV), 0.02),
        'w_out': nrm(ks[22], (L, D_MIX, D_MODEL), BETA * D_MIX ** -0.5),
        'ln1_g': 1.0 + nrm(ks[23], (L, D_MODEL), 0.05),
        'ln1_b': nrm(ks[24], (L, D_MODEL), 0.02),
        'router_w': nrm(ks[25], (L, D_MODEL, N_EXPERTS), D_MODEL ** -0.5),
        'router_b': nrm(ks[26], (L, N_EXPERTS), 0.01),
        'w_gate': nrm(ks[27], (L, N_EXPERTS, D_MODEL, D_FF), D_MODEL ** -0.5),
        'b_gate': nrm(ks[28], (L, N_EXPERTS, D_FF), 0.01),
        'w_up': nrm(ks[29], (L, N_EXPERTS, D_MODEL, D_FF), D_MODEL ** -0.5),
        'b_up': nrm(ks[30], (L, N_EXPERTS, D_FF), 0.01),
        'w_down': nrm(ks[31], (L, N_EXPERTS, D_FF, D_MODEL), BETA * D_FF ** -0.5),
        'b_down': nrm(ks[32], (L, N_EXPERTS, D_MODEL), 0.01),
        'ln2_g': 1.0 + nrm(ks[33], (L, D_MODEL), 0.05),
        'ln2_b': nrm(ks[34], (L, D_MODEL), 0.02),
    }


def reference(x_prompt, x_sample, state_conv, state_shift, state_wkv, w_in, b_in, mu_shift,
              conv_w, conv_b, conv_ln_g, conv_ln_b, rwkv_w0, rwkv_w2, rwkv_a0, rwkv_a2, rwkv_g2,
              rwkv_k_k, rwkv_k_a, rwkv_r_k, rwkv_ln_g, rwkv_ln_b, w_out, ln1_g, ln1_b,
              router_w, router_b, w_gate, b_gate, w_up, b_up, w_down, b_down, ln2_g, ln2_b):
    n_p = x_prompt.shape[0]
    y_p, y_s = x_prompt, x_sample
    conv_p, shift_p, wkv_p, conv_s, shift_s, wkv_s = [], [], [], [], [], []
    for d in range(DEPTH):
        lp = {
            'w_in': w_in[d], 'b_in': b_in[d], 'mu_shift': mu_shift[d],
            'conv_w': conv_w[d], 'conv_b': conv_b[d], 'conv_ln_g': conv_ln_g[d], 'conv_ln_b': conv_ln_b[d],
            'rwkv_w0': rwkv_w0[d], 'rwkv_w2': rwkv_w2[d], 'rwkv_a0': rwkv_a0[d], 'rwkv_a2': rwkv_a2[d],
            'rwkv_g2': rwkv_g2[d], 'rwkv_k_k': rwkv_k_k[d], 'rwkv_k_a': rwkv_k_a[d], 'rwkv_r_k': rwkv_r_k[d],
            'rwkv_ln_g': rwkv_ln_g[d], 'rwkv_ln_b': rwkv_ln_b[d], 'w_out': w_out[d],
            'ln1_g': ln1_g[d], 'ln1_b': ln1_b[d], 'router_w': router_w[d], 'router_b': router_b[d],
            'w_gate': w_gate[d], 'b_gate': b_gate[d], 'w_up': w_up[d], 'b_up': b_up[d],
            'w_down': w_down[d], 'b_down': b_down[d], 'ln2_g': ln2_g[d], 'ln2_b': ln2_b[d],
        }
        zero_conv = jnp.zeros((n_p, CONV_WIDTH - 1, C_CONV), x_prompt.dtype)
        zero_shift = jnp.zeros((n_p, 1, N_SHIFT), x_prompt.dtype)
        zero_wkv = jnp.zeros((n_p, H_RWKV, HEAD_RWKV, HEAD_RWKV), state_wkv.dtype)
        y_p, cp, sp, wp = _layer(y_p, zero_conv, zero_shift, zero_wkv, lp)
        y_s, cs, ss, ws = _layer(y_s, state_conv[d], state_shift[d], state_wkv[d], lp)
        conv_p.append(cp); shift_p.append(sp); wkv_p.append(wp)
        conv_s.append(cs); shift_s.append(ss); wkv_s.append(ws)
    return (y_p, y_s, jnp.stack(conv_p), jnp.stack(shift_p), jnp.stack(wkv_p),
            jnp.stack(conv_s), jnp.stack(shift_s), jnp.stack(wkv_s))
```

```python
import functools
import math

import jax
import jax.numpy as jnp
from jax import lax
from jax.experimental import pallas as pl
from jax.experimental.pallas import tpu as pltpu

F32 = jnp.float32
BF16 = jnp.bfloat16

D_MODEL = 2048
C_CONV = 1024
C_RWKV = 1024
HEAD = 64
N_HEADS = C_RWKV // HEAD
CONV_WIDTH = 31
R_DECAY = 64
R_ICLR = 64
R_GATE = 160
N_SHIFT = 3 * C_RWKV + R_DECAY + R_ICLR + R_GATE
N_EXPERTS = 32
TOP_K = 4
D_FF = 2048
SWIGLU_LIMIT = 7.0
SWIGLU_ALPHA = 1.702
LN_EPS = 1e-5
GN_EPS = 64e-5
ALPHA = 2.0 ** 0.25

LANES = 128
MXU_DIM = 256

HIST = 32
LORA_PAD = 512
P_PAD = 2 * C_CONV + 3 * C_RWKV + LORA_PAD
GROUP_HEADS = MXU_DIM // HEAD
N_GROUPS = N_HEADS // GROUP_HEADS
MOE_TM = 512
MOE_TF = 512
ROUTER_PAD = LANES


def _dot(a, b, prec=1, dims=(((1,), (0,)), ((), ()))):
    if prec == 6:
        return lax.dot_general(a.astype(F32), b.astype(F32), dims, precision=lax.Precision.HIGHEST,
                               preferred_element_type=F32)
    d = lambda x, y: lax.dot_general(x, y, dims, preferred_element_type=F32)
    if prec == 1:
        return d(a.astype(BF16), b.astype(BF16))
    a_hi = a.astype(BF16)
    a_lo = (a - a_hi.astype(F32)).astype(BF16)
    b_hi = b.astype(BF16)
    b_lo = (b - b_hi.astype(F32)).astype(BF16)
    return d(a_hi, b_hi) + d(a_hi, b_lo) + d(a_lo, b_hi)


_NT = (((1,), (1,)), ((), ()))


def _mm_bias_kernel(x_ref, w_ref, b_ref, o_ref, xb_ref):
    @pl.when(pl.program_id(1) == 0)
    def _():
        xb_ref[...] = x_ref[...].astype(BF16)

    o_ref[...] = jnp.dot(xb_ref[...], w_ref[...], preferred_element_type=F32) + b_ref[...]


def _in_proj(x, w_bf16, b, tm, tn):
    n, k = x.shape
    p = w_bf16.shape[1]
    return pl.pallas_call(
        _mm_bias_kernel,
        out_shape=jax.ShapeDtypeStruct((n, p), F32),
        grid=(n // tm, p // tn),
        in_specs=[pl.BlockSpec((tm, k), lambda i, j: (i, 0)),
                  pl.BlockSpec((k, tn), lambda i, j: (0, j)),
                  pl.BlockSpec((1, tn), lambda i, j: (0, j))],
        out_specs=pl.BlockSpec((tm, tn), lambda i, j: (i, j)),
        scratch_shapes=[pltpu.VMEM((tm, k), BF16)],
        compiler_params=pltpu.CompilerParams(dimension_semantics=("arbitrary", "arbitrary")),
        name="in_proj",
    )(x, w_bf16, b)


def _conv_kernel(val_ref, gate_ref, hist_ref, w_ref, cb_ref, g_ref, b_ref, c_ref, tail_ref, ext_ref):
    t = pl.program_id(1)
    tt = val_ref.shape[1]

    @pl.when(t == 0)
    def _():
        ext_ref[0:HIST, :] = hist_ref[0]

    u = val_ref[0] * jax.nn.sigmoid(gate_ref[0])
    ext_ref[HIST:HIST + tt, :] = u
    off = HIST - (CONV_WIDTH - 1)
    acc = jnp.broadcast_to(cb_ref[...], (tt, C_CONV))
    for j in range(CONV_WIDTH):
        acc = acc + w_ref[j:j + 1, :] * ext_ref[off + j:off + j + tt, :]
    mu = jnp.mean(acc, axis=-1, keepdims=True)
    xc = acc - mu
    var = jnp.mean(xc * xc, axis=-1, keepdims=True)
    y = xc * lax.rsqrt(var + LN_EPS) * g_ref[...] + b_ref[...]
    c_ref[0] = (y * jax.nn.sigmoid(y)).astype(c_ref.dtype)
    tail = ext_ref[tt:tt + HIST, :]
    ext_ref[0:HIST, :] = tail
    tail_ref[0] = tail


def _conv_module(proj3, hist, conv_w, conv_b, ln_g, ln_b, tt):
    bsz, t_len, _ = proj3.shape
    nblk = C_CONV // C_CONV
    return pl.pallas_call(
        _conv_kernel,
        out_shape=(jax.ShapeDtypeStruct((bsz, t_len, C_CONV), BF16),
                   jax.ShapeDtypeStruct((bsz, HIST, C_CONV), F32)),
        grid=(bsz, t_len // tt),
        in_specs=[pl.BlockSpec((1, tt, C_CONV), lambda b, t: (b, t, 0)),
                  pl.BlockSpec((1, tt, C_CONV), lambda b, t: (b, t, nblk)),
                  pl.BlockSpec((1, HIST, C_CONV), lambda b, t: (b, 0, 0)),
                  pl.BlockSpec((CONV_WIDTH, C_CONV), lambda b, t: (0, 0)),
                  pl.BlockSpec((1, C_CONV), lambda b, t: (0, 0)),
                  pl.BlockSpec((1, C_CONV), lambda b, t: (0, 0)),
                  pl.BlockSpec((1, C_CONV), lambda b, t: (0, 0))],
        out_specs=(pl.BlockSpec((1, tt, C_CONV), lambda b, t: (b, t, 0)),
                   pl.BlockSpec((1, HIST, C_CONV), lambda b, t: (b, 0, 0))),
        scratch_shapes=[pltpu.VMEM((HIST + tt, C_CONV), F32)],
        compiler_params=pltpu.CompilerParams(dimension_semantics=("arbitrary", "arbitrary")),
        name="conv_module",
    )(proj3, proj3, hist, conv_w, conv_b, ln_g, ln_b)


PREC_STAT = 6
PREC_LORA = 6
PREC_CHUNK = 1
PREC_STATE = 1


def _seg_sum(x, e_ref, et_ref):
    s = _dot(x, e_ref[...], PREC_STAT)
    return _dot(s, et_ref[...], PREC_STAT)


def _rwkv_kernel(r_ref, k_ref, v_ref, lo_ref, shr_ref, shk_ref, shv_ref, shlo_ref, st0_ref,
                 mur_ref, muk_ref, muv_ref, mulo_ref, w0_ref, w2_ref, a0_ref, a2_ref, g2_ref,
                 kkw_ref, kaw_ref, rkw_ref, lng_ref, lnb_ref, e_ref, et_ref,
                 y_ref, stout_ref, shout_ref,
                 st_sc, pr_sc, pk_sc, pv_sc, plo_sc):
    c = pl.program_id(1)
    n_chunks = pl.num_programs(1)
    L = r_ref.shape[1]
    GL = GROUP_HEADS * L
    log2l = int(math.log2(L))

    @pl.when(c == 0)
    def _():
        st_sc[...] = st0_ref[0]
        pr_sc[...] = shr_ref[0]
        pk_sc[...] = shk_ref[0]
        pv_sc[...] = shv_ref[0]
        plo_sc[...] = shlo_ref[0]

    def token_shift(x_ref, prev_sc, mu_ref):
        x = x_ref[0]
        row = lax.broadcasted_iota(jnp.int32, x.shape, 0)
        xprev = jnp.where(row == 0, jnp.broadcast_to(prev_sc[...], x.shape), pltpu.roll(x, 1, 0))
        prev_sc[...] = x[L - 1:L, :]
        return x + mu_ref[...] * (xprev - x)

    r = token_shift(r_ref, pr_sc, mur_ref)
    k = token_shift(k_ref, pk_sc, muk_ref)
    v = token_shift(v_ref, pv_sc, muv_ref)
    lo = token_shift(lo_ref, plo_sc, mulo_ref)
    xw = lo[:, 0:LANES]
    xa = lo[:, LANES:2 * LANES]
    xg = lo[:, 2 * LANES:LORA_PAD]

    u_dec = w0_ref[...] + _dot(jnp.tanh(xw), w2_ref[...], PREC_LORA)
    logw = (-math.exp(-0.5)) * jax.nn.sigmoid(u_dec)
    a = jax.nn.sigmoid(a0_ref[...] + _dot(xa, a2_ref[...], PREC_LORA))
    g = _dot(jax.nn.sigmoid(xg), g2_ref[...], PREC_LORA)

    kk = k * kkw_ref[...]
    nrm = jnp.sqrt(_seg_sum(kk * kk, e_ref, et_ref))
    kappa = kk / jnp.maximum(nrm, 1e-12)
    k2 = k * (1.0 + (a - 1.0) * kaw_ref[...])
    bvec = kappa * a
    bonus = _seg_sum(r * k2 * rkw_ref[...], e_ref, et_ref) * v

    ti = lax.broadcasted_iota(jnp.int32, (L, L), 0)
    tj = lax.broadcasted_iota(jnp.int32, (L, L), 1)
    tril = jnp.where(tj <= ti, 1.0, 0.0).astype(F32)
    cum = _dot(tril, logw, PREC_STAT)
    cum_l = cum[L - 1:L, :]
    gam = jnp.exp(cum)
    ginv = jnp.exp(-cum)
    gprev = jnp.exp(cum - logw)
    gtail = jnp.exp(cum_l - cum)
    gam_l = jnp.exp(cum_l)

    kt = kappa * gprev
    kinv = k2 * ginv
    binv = bvec * ginv
    rt = r * gam
    khat = k2 * gtail
    bhat = bvec * gtail

    rr = lax.broadcasted_iota(jnp.int32, (GL, GL), 0)
    cc = lax.broadcasted_iota(jnp.int32, (GL, GL), 1)
    same = (rr >> log2l) == (cc >> log2l)
    tpos = rr & (L - 1)
    jpos = cc & (L - 1)
    mask_s = same & (jpos < tpos)
    mask_i = same & (jpos <= tpos)
    eye = rr == cc
    srow = lax.broadcasted_iota(jnp.int32, (GL, MXU_DIM), 0)
    slane = lax.broadcasted_iota(jnp.int32, (GL, MXU_DIM), 1)
    bmask = (srow >> log2l) == (slane >> int(math.log2(HEAD)))
    drow = lax.broadcasted_iota(jnp.int32, (MXU_DIM, MXU_DIM), 0)
    dcol = lax.broadcasted_iota(jnp.int32, (MXU_DIM, MXU_DIM), 1)
    deye = drow == dcol

    def stack(x, gi):
        xg_ = x[:, gi * MXU_DIM:(gi + 1) * MXU_DIM]
        return jnp.where(bmask, jnp.concatenate([xg_] * GROUP_HEADS, axis=0), 0.0)

    y_groups = []
    for gi in range(N_GROUPS):
        kt_s = stack(kt, gi)
        rt_s = stack(rt, gi)
        binv_s = stack(binv, gi)
        kinv_s = stack(kinv, gi)
        v_s = stack(v, gi)
        khat_s = stack(khat, gi)
        bhat_s = stack(bhat, gi)

        n_mat = jnp.where(mask_s, _dot(kt_s, binv_s, PREC_CHUNK, _NT), 0.0)
        a_kk = jnp.where(mask_s, _dot(kt_s, kinv_s, PREC_CHUNK, _NT), 0.0)
        a_br = jnp.where(mask_i, _dot(rt_s, binv_s, PREC_CHUNK, _NT), 0.0)
        a_kr = jnp.where(mask_i, _dot(rt_s, kinv_s, PREC_CHUNK, _NT), 0.0)

        p_mat = -n_mat
        t_mat = jnp.where(eye, 1.0, 0.0) + p_mat
        for _ in range(log2l - 1):
            p_mat = _dot(p_mat, p_mat, PREC_CHUNK)
            t_mat = t_mat + _dot(t_mat, p_mat, PREC_CHUNK)

        w_s = _dot(t_mat, kt_s, PREC_CHUNK)
        uv_s = _dot(t_mat, _dot(a_kk, v_s, PREC_CHUNK), PREC_CHUNK)
        q_s = rt_s - _dot(a_br, w_s, PREC_CHUNK)
        y0_s = _dot(a_kr, v_s, PREC_CHUNK) - _dot(a_br, uv_s, PREC_CHUNK)
        bhat_t = bhat_s.T
        khat_t = khat_s.T
        gl_g = gam_l[:, gi * MXU_DIM:(gi + 1) * MXU_DIM]
        m_mat = jnp.where(deye, jnp.broadcast_to(gl_g, (MXU_DIM, MXU_DIM)), 0.0) - _dot(bhat_t, w_s, PREC_CHUNK)
        c_mat = _dot(khat_t, v_s, PREC_CHUNK) - _dot(bhat_t, uv_s, PREC_CHUNK)

        st = st_sc[gi]
        ys = _dot(q_s, st, PREC_STATE) + y0_s
        st_sc[gi] = _dot(m_mat, st, PREC_STATE) + c_mat
        yg = ys[0:L]
        for h in range(1, GROUP_HEADS):
            yg = yg + ys[h * L:(h + 1) * L]
        y_groups.append(yg)

    y = jnp.concatenate(y_groups, axis=1)
    inv_head = 1.0 / HEAD
    mu = _seg_sum(y, e_ref, et_ref) * inv_head
    yc = y - mu
    var = _seg_sum(yc * yc, e_ref, et_ref) * inv_head
    yn = yc * lax.rsqrt(var + GN_EPS) * lng_ref[...] + lnb_ref[...]
    y_ref[0] = ((yn + bonus) * g).astype(y_ref.dtype)

    @pl.when(c == n_chunks - 1)
    def _():
        stout_ref[0] = st_sc[...]
        shout_ref[0, :, 0:C_RWKV] = pr_sc[...]
        shout_ref[0, :, C_RWKV:2 * C_RWKV] = pk_sc[...]
        shout_ref[0, :, 2 * C_RWKV:3 * C_RWKV] = pv_sc[...]
        shout_ref[0, :, 3 * C_RWKV:3 * C_RWKV + LORA_PAD] = plo_sc[...]


def _rwkv_mix(proj3, shift_parts, st0, params, chunk):
    bsz, t_len, _ = proj3.shape
    L = chunk
    rkv_blk0 = 2 * C_CONV // C_RWKV
    lora_blk = (2 * C_CONV + 3 * C_RWKV) // LORA_PAD
    row = lambda n: pl.BlockSpec((1, n), lambda b, c: (0, 0))
    full = lambda s: pl.BlockSpec(s, lambda b, c: tuple(0 for _ in s))
    sh = lambda n: pl.BlockSpec((1, 1, n), lambda b, c: (b, 0, 0))
    in_specs = [
        pl.BlockSpec((1, L, C_RWKV), lambda b, c: (b, c, rkv_blk0)),
        pl.BlockSpec((1, L, C_RWKV), lambda b, c: (b, c, rkv_blk0 + 1)),
        pl.BlockSpec((1, L, C_RWKV), lambda b, c: (b, c, rkv_blk0 + 2)),
        pl.BlockSpec((1, L, LORA_PAD), lambda b, c: (b, c, lora_blk)),
        sh(C_RWKV), sh(C_RWKV), sh(C_RWKV), sh(LORA_PAD),
        pl.BlockSpec((1, N_GROUPS, MXU_DIM, MXU_DIM), lambda b, c: (b, 0, 0, 0)),
        row(C_RWKV), row(C_RWKV), row(C_RWKV), row(LORA_PAD),
        row(C_RWKV), full((LANES, C_RWKV)), row(C_RWKV), full((LANES, C_RWKV)), full((2 * LANES, C_RWKV)),
        row(C_RWKV), row(C_RWKV), row(C_RWKV), row(C_RWKV), row(C_RWKV),
        full((C_RWKV, LANES)), full((LANES, C_RWKV)),
    ]
    out_shape = (jax.ShapeDtypeStruct((bsz, t_len, C_RWKV), BF16),
                 jax.ShapeDtypeStruct((bsz, N_GROUPS, MXU_DIM, MXU_DIM), F32),
                 jax.ShapeDtypeStruct((bsz, 1, 3 * C_RWKV + LORA_PAD), F32))
    out_specs = (pl.BlockSpec((1, L, C_RWKV), lambda b, c: (b, c, 0)),
                 pl.BlockSpec((1, N_GROUPS, MXU_DIM, MXU_DIM), lambda b, c: (b, 0, 0, 0)),
                 pl.BlockSpec((1, 1, 3 * C_RWKV + LORA_PAD), lambda b, c: (b, 0, 0)))
    return pl.pallas_call(
        _rwkv_kernel,
        out_shape=out_shape,
        grid=(bsz, t_len // L),
        in_specs=in_specs,
        out_specs=out_specs,
        scratch_shapes=[pltpu.VMEM((N_GROUPS, MXU_DIM, MXU_DIM), F32),
                        pltpu.VMEM((1, C_RWKV), F32), pltpu.VMEM((1, C_RWKV), F32),
                        pltpu.VMEM((1, C_RWKV), F32), pltpu.VMEM((1, LORA_PAD), F32)],
        compiler_params=pltpu.CompilerParams(dimension_semantics=("arbitrary", "arbitrary")),
        name="rwkv7_mix",
    )(proj3, proj3, proj3, proj3, *shift_parts, st0, *params)


def _outproj_kernel(c_ref, y_ref, x_ref, wa_ref, wb_ref, g_ref, b_ref, rw_ref, rb_ref,
                    x1_ref, idx_ref, gate_ref):
    mix = (jnp.dot(c_ref[...], wa_ref[...], preferred_element_type=F32)
           + jnp.dot(y_ref[...], wb_ref[...], preferred_element_type=F32))
    h = ALPHA * x_ref[...] + mix
    mu = jnp.mean(h, axis=-1, keepdims=True)
    hc = h - mu
    var = jnp.mean(hc * hc, axis=-1, keepdims=True)
    x1 = hc * lax.rsqrt(var + LN_EPS) * g_ref[...] + b_ref[...]
    x1_ref[...] = x1
    logits = _dot(x1, rw_ref[...], 6) + rb_ref[...]
    lane = lax.broadcasted_iota(jnp.int32, logits.shape, 1)
    idx_out = jnp.zeros(logits.shape, jnp.int32)
    val_out = jnp.zeros(logits.shape, F32)
    vals = []
    for kk in range(TOP_K):
        m = jnp.max(logits, axis=-1, keepdims=True)
        sel = jnp.min(jnp.where(logits == m, lane, ROUTER_PAD), axis=-1, keepdims=True)
        vals.append(m)
        idx_out = jnp.where(lane == kk, sel, idx_out)
        logits = jnp.where(lane == sel, -jnp.inf, logits)
    exps = [jnp.exp(vv - vals[0]) for vv in vals]
    denom = exps[0]
    for ee in exps[1:]:
        denom = denom + ee
    for kk in range(TOP_K):
        val_out = jnp.where(lane == kk, exps[kk] / denom, val_out)
    idx_ref[...] = idx_out
    gate_ref[...] = val_out


def _out_proj(c2, y2, x2, wa, wb, ln_g, ln_b, rw, rb, tm):
    n = x2.shape[0]
    row = lambda w: pl.BlockSpec((1, w), lambda i: (0, 0))
    return pl.pallas_call(
        _outproj_kernel,
        out_shape=(jax.ShapeDtypeStruct((n, D_MODEL), F32),
                   jax.ShapeDtypeStruct((n, ROUTER_PAD), jnp.int32),
                   jax.ShapeDtypeStruct((n, ROUTER_PAD), F32)),
        grid=(n // tm,),
        in_specs=[pl.BlockSpec((tm, C_CONV), lambda i: (i, 0)),
                  pl.BlockSpec((tm, C_RWKV), lambda i: (i, 0)),
                  pl.BlockSpec((tm, D_MODEL), lambda i: (i, 0)),
                  pl.BlockSpec((C_CONV, D_MODEL), lambda i: (0, 0)),
                  pl.BlockSpec((C_RWKV, D_MODEL), lambda i: (0, 0)),
                  row(D_MODEL), row(D_MODEL),
                  pl.BlockSpec((D_MODEL, ROUTER_PAD), lambda i: (0, 0)),
                  row(ROUTER_PAD)],
        out_specs=(pl.BlockSpec((tm, D_MODEL), lambda i: (i, 0)),
                   pl.BlockSpec((tm, ROUTER_PAD), lambda i: (i, 0)),
                   pl.BlockSpec((tm, ROUTER_PAD), lambda i: (i, 0))),
        compiler_params=pltpu.CompilerParams(dimension_semantics=("arbitrary",)),
        name="out_proj_ln_router",
    )(c2, y2, x2, wa, wb, ln_g, ln_b, rw, rb)


def _gather_rows_kernel(tok_ref, x_hbm, o_hbm, sem):
    rows = tok_ref.shape[2]
    base = pl.program_id(0) * rows

    def issue(r, carry):
        t = tok_ref[0, 0, r]
        pltpu.make_async_copy(x_hbm.at[pl.ds(t, 1)], o_hbm.at[pl.ds(base + r, 1)], sem).start()
        return carry

    lax.fori_loop(0, rows, issue, 0)

    def drain(r, carry):
        pltpu.make_async_copy(x_hbm.at[pl.ds(0, 1)], o_hbm.at[pl.ds(base + r, 1)], sem).wait()
        return carry

    lax.fori_loop(0, rows, drain, 0)


def _gather_rows(x, row_tok3):
    nb, _, rows = row_tok3.shape
    return pl.pallas_call(
        _gather_rows_kernel,
        out_shape=jax.ShapeDtypeStruct((nb * rows, x.shape[1]), x.dtype),
        grid=(nb,),
        in_specs=[pl.BlockSpec((1, 1, rows), lambda i: (i, 0, 0), memory_space=pltpu.SMEM),
                  pl.BlockSpec(memory_space=pl.ANY)],
        out_specs=pl.BlockSpec(memory_space=pl.ANY),
        scratch_shapes=[pltpu.SemaphoreType.DMA],
        compiler_params=pltpu.CompilerParams(dimension_semantics=("arbitrary",)),
        name="moe_gather_rows",
    )(row_tok3, x)


def _expert_kernel(bexp_ref, nused_ref, x_ref, wg_ref, bg_ref, wu_ref, bu_ref, wd_ref, bd_ref,
                   o_ref, xb_ref, acc_ref):
    i = pl.program_id(0)
    j = pl.program_id(1)
    nj = pl.num_programs(1)

    @pl.when(i < nused_ref[0])
    def _():
        @pl.when(j == 0)
        def _():
            xb_ref[...] = x_ref[...].astype(BF16)
            acc_ref[...] = jnp.zeros_like(acc_ref)

        xb = xb_ref[...]
        gate = jnp.minimum(jnp.dot(xb, wg_ref[0], preferred_element_type=F32) + bg_ref[0], SWIGLU_LIMIT)
        up = jnp.clip(jnp.dot(xb, wu_ref[0], preferred_element_type=F32) + bu_ref[0],
                      -SWIGLU_LIMIT, SWIGLU_LIMIT)
        hmid = (up + 1.0) * gate * jax.nn.sigmoid(SWIGLU_ALPHA * gate)
        acc_ref[...] += jnp.dot(hmid.astype(BF16), wd_ref[0], preferred_element_type=F32)

        @pl.when(j == nj - 1)
        def _():
            o_ref[...] = acc_ref[...] + bd_ref[0]

    @pl.when((i >= nused_ref[0]) & (j == nj - 1))
    def _():
        o_ref[...] = jnp.zeros_like(o_ref)


def _experts(block_exp, n_used, xb, wg, bg, wu, bu, wd, bd):
    n_rows = xb.shape[0]
    nb = n_rows // MOE_TM
    nj = D_FF // MOE_TF

    def jj(i, j, nu):
        return jnp.where(i < nu[0], j, nj - 1)

    grid_spec = pltpu.PrefetchScalarGridSpec(
        num_scalar_prefetch=2,
        grid=(nb, nj),
        in_specs=[pl.BlockSpec((MOE_TM, D_MODEL), lambda i, j, be, nu: (i, 0)),
                  pl.BlockSpec((1, D_MODEL, MOE_TF), lambda i, j, be, nu: (be[i], 0, jj(i, j, nu))),
                  pl.BlockSpec((1, 1, MOE_TF), lambda i, j, be, nu: (be[i], 0, jj(i, j, nu))),
                  pl.BlockSpec((1, D_MODEL, MOE_TF), lambda i, j, be, nu: (be[i], 0, jj(i, j, nu))),
                  pl.BlockSpec((1, 1, MOE_TF), lambda i, j, be, nu: (be[i], 0, jj(i, j, nu))),
                  pl.BlockSpec((1, MOE_TF, D_MODEL), lambda i, j, be, nu: (be[i], jj(i, j, nu), 0)),
                  pl.BlockSpec((1, 1, D_MODEL), lambda i, j, be, nu: (be[i], 0, 0))],
        out_specs=pl.BlockSpec((MOE_TM, D_MODEL), lambda i, j, be, nu: (i, 0)),
        scratch_shapes=[pltpu.VMEM((MOE_TM, D_MODEL), BF16), pltpu.VMEM((MOE_TM, D_MODEL), F32)],
    )
    return pl.pallas_call(
        _expert_kernel,
        out_shape=jax.ShapeDtypeStruct((n_rows, D_MODEL), F32),
        grid_spec=grid_spec,
        compiler_params=pltpu.CompilerParams(dimension_semantics=("arbitrary", "arbitrary")),
        name="moe_experts",
    )(block_exp, n_used, xb, wg, bg, wu, bu, wd, bd)


def _combine_kernel(dest_ref, gate_ref, x1_ref, g_ref, b_ref, yb_hbm, o_ref, buf_ref, sem):
    tc = x1_ref.shape[0]

    def issue(r, carry):
        for kk in range(TOP_K):
            d = dest_ref[0, 0, r * TOP_K + kk]
            pltpu.make_async_copy(yb_hbm.at[pl.ds(d, 1)], buf_ref.at[kk, pl.ds(r, 1)], sem).start()
        return carry

    lax.fori_loop(0, tc, issue, 0)

    def drain(r, carry):
        for kk in range(TOP_K):
            pltpu.make_async_copy(yb_hbm.at[pl.ds(0, 1)], buf_ref.at[kk, pl.ds(r, 1)], sem).wait()
        return carry

    lax.fori_loop(0, tc, drain, 0)

    gates = gate_ref[...]
    moe = gates[:, 0:1] * buf_ref[0]
    for kk in range(1, TOP_K):
        moe = moe + gates[:, kk:kk + 1] * buf_ref[kk]
    h = ALPHA * x1_ref[...] + moe
    mu = jnp.mean(h, axis=-1, keepdims=True)
    hc = h - mu
    var = jnp.mean(hc * hc, axis=-1, keepdims=True)
    o_ref[...] = hc * lax.rsqrt(var + LN_EPS) * g_ref[...] + b_ref[...]


def _combine(dest3, gates, x1, ln_g, ln_b, yb, tc):
    n = x1.shape[0]
    return pl.pallas_call(
        _combine_kernel,
        out_shape=jax.ShapeDtypeStruct((n, D_MODEL), F32),
        grid=(n // tc,),
        in_specs=[pl.BlockSpec((1, 1, tc * TOP_K), lambda i: (i, 0, 0), memory_space=pltpu.SMEM),
                  pl.BlockSpec((tc, ROUTER_PAD), lambda i: (i, 0)),
                  pl.BlockSpec((tc, D_MODEL), lambda i: (i, 0)),
                  pl.BlockSpec((1, D_MODEL), lambda i: (0, 0)),
                  pl.BlockSpec((1, D_MODEL), lambda i: (0, 0)),
                  pl.BlockSpec(memory_space=pl.ANY)],
        out_specs=pl.BlockSpec((tc, D_MODEL), lambda i: (i, 0)),
        scratch_shapes=[pltpu.VMEM((TOP_K, tc, D_MODEL), F32), pltpu.SemaphoreType.DMA],
        compiler_params=pltpu.CompilerParams(dimension_semantics=("arbitrary",)),
        name="moe_combine_ln2",
    )(dest3, gates, x1, ln_g, ln_b, yb)


def _pad_cols(w, width):
    return jnp.pad(w, ((0, 0), (0, width - w.shape[1])))


def _pad_rows(w, height):
    return jnp.pad(w, ((0, height - w.shape[0]), (0, 0)))


def _split_lora_cols(w):
    xw = w[..., 0:R_DECAY]
    xa = w[..., R_DECAY:R_DECAY + R_ICLR]
    xg = w[..., R_DECAY + R_ICLR:]
    pad = lambda x, n: jnp.pad(x, [(0, 0)] * (x.ndim - 1) + [(0, n - x.shape[-1])])
    return jnp.concatenate([pad(xw, LANES), pad(xa, LANES), pad(xg, 2 * LANES)], axis=-1)


def _pick(n, prefs):
    for p in prefs:
        if n % p == 0:
            return p
    return n


def _mixer_group(x, conv_buf, shift_buf, wkv_state, wts):
    bsz, t_len, _ = x.shape
    n = bsz * t_len
    proj = _in_proj(x.reshape(n, D_MODEL), wts["w_in"], wts["b_in"], _pick(n, (512, 256, 128)), 512)
    proj3 = proj.reshape(bsz, t_len, P_PAD)

    hist = jnp.pad(conv_buf, ((0, 0), (HIST - (CONV_WIDTH - 1), 0), (0, 0)))
    c, tail = _conv_module(proj3, hist, wts["conv_w"], wts["conv_b"], wts["conv_ln_g"], wts["conv_ln_b"],
                           _pick(t_len, (128, 64, 32, 16, 8)))
    new_conv = tail[:, HIST - (CONV_WIDTH - 1):, :]

    sh_rkv = shift_buf[:, :, :3 * C_RWKV]
    sh_lo = _split_lora_cols(shift_buf[:, :, 3 * C_RWKV:])
    shift_parts = (sh_rkv[:, :, 0:C_RWKV], sh_rkv[:, :, C_RWKV:2 * C_RWKV], sh_rkv[:, :, 2 * C_RWKV:], sh_lo)
    st_t = jnp.swapaxes(wkv_state, -1, -2).reshape(bsz, N_GROUPS, GROUP_HEADS, HEAD, HEAD)
    eye_h = jnp.eye(GROUP_HEADS, dtype=F32)
    st0 = jnp.einsum("bghkv,hj->bghkjv", st_t, eye_h).reshape(bsz, N_GROUPS, MXU_DIM, MXU_DIM)
    yb, st_out, sh_out = _rwkv_mix(proj3, shift_parts, st0, wts["rwkv_params"], _pick(t_len, (64, 32, 16)))
    st5 = st_out.reshape(bsz, N_GROUPS, GROUP_HEADS, HEAD, GROUP_HEADS, HEAD)
    st_diag = jnp.einsum("bghkhv->bghkv", st5)
    new_wkv = jnp.swapaxes(st_diag, -1, -2).reshape(bsz, N_HEADS, HEAD, HEAD)
    lo = sh_out[:, :, 3 * C_RWKV:]
    new_shift = jnp.concatenate([sh_out[:, :, :3 * C_RWKV], lo[:, :, 0:R_DECAY], lo[:, :, LANES:LANES + R_ICLR],
                                 lo[:, :, 2 * LANES:2 * LANES + R_GATE]], axis=-1)
    return c.reshape(n, C_CONV), yb.reshape(n, C_RWKV), new_conv, new_shift, new_wkv


def _route(top_idx, n_tok):
    n_assign = n_tok * TOP_K
    flat_e = top_idx.reshape(-1)
    onehot = (flat_e[:, None] == jnp.arange(N_EXPERTS, dtype=jnp.int32)[None, :]).astype(jnp.int32)
    csum = jnp.cumsum(onehot, axis=0)
    rank = jnp.take_along_axis(csum, flat_e[:, None], axis=1)[:, 0] - 1
    counts = csum[-1]
    padded = (counts + MOE_TM - 1) // MOE_TM * MOE_TM
    seg_end = jnp.cumsum(padded)
    seg_start = seg_end - padded
    dest = (seg_start[flat_e] + rank).astype(jnp.int32)
    n_rows = (n_assign + N_EXPERTS * (MOE_TM - 1) + MOE_TM - 1) // MOE_TM * MOE_TM
    n_blocks = n_rows // MOE_TM
    row_tok = jnp.zeros((n_rows,), jnp.int32).at[dest].set(jnp.arange(n_assign, dtype=jnp.int32) // TOP_K)
    block_exp = jnp.minimum(
        jnp.searchsorted(seg_end, jnp.arange(n_blocks, dtype=jnp.int32) * MOE_TM, side="right"),
        N_EXPERTS - 1).astype(jnp.int32)
    n_used = (seg_end[-1] // MOE_TM).astype(jnp.int32).reshape(1)
    return dest, row_tok, block_exp, n_used, n_blocks


def kernel(x_prompt, x_sample, state_conv, state_shift, state_wkv, w_in, b_in, mu_shift, conv_w, conv_b,
           conv_ln_g, conv_ln_b, rwkv_w0, rwkv_w2, rwkv_a0, rwkv_a2, rwkv_g2, rwkv_k_k, rwkv_k_a, rwkv_r_k,
           rwkv_ln_g, rwkv_ln_b, w_out, ln1_g, ln1_b, router_w, router_b, w_gate, b_gate, w_up, b_up,
           w_down, b_down, ln2_g, ln2_b):
    assert w_in.shape[0] == 1, "single layer"
    d = 0
    row = lambda v: v.reshape(1, -1)
    n_p, t_p, _ = x_prompt.shape
    n_s, t_s, _ = x_sample.shape

    w_rkv = w_in[d][:, 2 * C_CONV:2 * C_CONV + 3 * C_RWKV]
    w_lo = _split_lora_cols(w_in[d][:, 2 * C_CONV + 3 * C_RWKV:])
    w_in_p = jnp.concatenate([w_in[d][:, :2 * C_CONV], w_rkv, w_lo], axis=1).astype(BF16)
    b_in_p = jnp.concatenate([b_in[d][None, :2 * C_CONV], b_in[d][None, 2 * C_CONV:2 * C_CONV + 3 * C_RWKV],
                              _split_lora_cols(b_in[d][None, 2 * C_CONV + 3 * C_RWKV:])], axis=1)
    mu = mu_shift[d][None, :]
    mu_lo = _split_lora_cols(mu[:, 3 * C_RWKV:])
    head_of_lane = jnp.arange(C_RWKV, dtype=jnp.int32) // HEAD
    e_mat = (head_of_lane[:, None] == jnp.arange(LANES, dtype=jnp.int32)[None, :]).astype(F32)
    rwkv_params = (
        mu[:, 0:C_RWKV], mu[:, C_RWKV:2 * C_RWKV], mu[:, 2 * C_RWKV:3 * C_RWKV], mu_lo,
        row(rwkv_w0[d]), _pad_rows(rwkv_w2[d], LANES), row(rwkv_a0[d]), _pad_rows(rwkv_a2[d], LANES),
        _pad_rows(rwkv_g2[d], 2 * LANES),
        row(rwkv_k_k[d]), row(rwkv_k_a[d]), row(rwkv_r_k[d]), row(rwkv_ln_g[d]), row(rwkv_ln_b[d]),
        e_mat, e_mat.T,
    )
    wts = dict(w_in=w_in_p, b_in=b_in_p, conv_w=conv_w[d], conv_b=row(conv_b[d]),
               conv_ln_g=row(conv_ln_g[d]), conv_ln_b=row(conv_ln_b[d]), rwkv_params=rwkv_params)

    zero_conv = jnp.zeros((n_p, CONV_WIDTH - 1, C_CONV), x_prompt.dtype)
    zero_shift = jnp.zeros((n_p, 1, N_SHIFT), x_prompt.dtype)
    zero_wkv = jnp.zeros((n_p, N_HEADS, HEAD, HEAD), state_wkv.dtype)
    c_p, y_p, conv_p, shift_p, wkv_p = _mixer_group(x_prompt, zero_conv, zero_shift, zero_wkv, wts)
    c_s, y_s, conv_s, shift_s, wkv_s = _mixer_group(x_sample, state_conv[d], state_shift[d], state_wkv[d], wts)

    c_all = jnp.concatenate([c_p, c_s], axis=0)
    y_all = jnp.concatenate([y_p, y_s], axis=0)
    x_all = jnp.concatenate([x_prompt.reshape(-1, D_MODEL), x_sample.reshape(-1, D_MODEL)], axis=0)
    n_tok = x_all.shape[0]
    w_out_b = w_out[d].astype(BF16)
    rw = _pad_cols(router_w[d], ROUTER_PAD)
    rb = jnp.concatenate([router_b[d], jnp.full((ROUTER_PAD - N_EXPERTS,), -jnp.inf, F32)])[None, :]
    tm_tok = _pick(n_tok, (256, 128))
    x1, idx_pad, gate_pad = _out_proj(c_all, y_all, x_all, w_out_b[:C_CONV], w_out_b[C_CONV:], row(ln1_g[d]),
                                      row(ln1_b[d]), rw, rb, tm_tok)

    dest, row_tok, block_exp, n_used, n_blocks = _route(idx_pad[:, :TOP_K], n_tok)
    xb = _gather_rows(x1, row_tok.reshape(n_blocks, 1, MOE_TM))
    yb = _experts(block_exp, n_used, xb, w_gate[d].astype(BF16), b_gate[d][:, None, :],
                  w_up[d].astype(BF16), b_up[d][:, None, :], w_down[d].astype(BF16), b_down[d][:, None, :])
    tc = _pick(n_tok, (128,))
    out = _combine(dest.reshape(n_tok // tc, 1, tc * TOP_K), gate_pad, x1, row(ln2_g[d]), row(ln2_b[d]), yb, tc)

    n_prompt_tok = n_p * t_p
    y_prompt = out[:n_prompt_tok].reshape(n_p, t_p, D_MODEL)
    y_sample = out[n_prompt_tok:].reshape(n_s, t_s, D_MODEL)
    return (y_prompt, y_sample, conv_p[None], shift_p[None], wkv_p[None], conv_s[None], shift_s[None], wkv_s[None])
```

```python
import functools
import math

import jax
import jax.numpy as jnp
from jax import lax
from jax.experimental import pallas as pl
from jax.experimental.pallas import tpu as pltpu

F32 = jnp.float32
BF16 = jnp.bfloat16

D_MODEL = 2048
C_CONV = 1024
C_RWKV = 1024
HEAD = 64
N_HEADS = C_RWKV // HEAD
CONV_WIDTH = 31
R_DECAY = 64
R_ICLR = 64
R_GATE = 160
N_SHIFT = 3 * C_RWKV + R_DECAY + R_ICLR + R_GATE
N_EXPERTS = 32
TOP_K = 4
D_FF = 2048
SWIGLU_LIMIT = 7.0
SWIGLU_ALPHA = 1.702
LN_EPS = 1e-5
GN_EPS = 64e-5
ALPHA = 2.0 ** 0.25

LANES = 128

HIST = 32
LORA_PAD = 512
P_PAD = 2 * C_CONV + 3 * C_RWKV + LORA_PAD
GROUP_HEADS = 2
PACK = GROUP_HEADS * HEAD
N_GROUPS = N_HEADS // GROUP_HEADS
MOE_TM = 512
MOE_TF = 512
OUT_TM = 512
ROUTER_PAD = LANES


def _dot(a, b, prec=1, dims=(((1,), (0,)), ((), ()))):
    if prec == 6:
        return lax.dot_general(a.astype(F32), b.astype(F32), dims, precision=lax.Precision.HIGHEST,
                               preferred_element_type=F32)
    d = lambda x, y: lax.dot_general(x, y, dims, preferred_element_type=F32)
    if prec == 1:
        return d(a.astype(BF16), b.astype(BF16))
    a_hi = a.astype(BF16)
    a_lo = (a - a_hi.astype(F32)).astype(BF16)
    b_hi = b.astype(BF16)
    b_lo = (b - b_hi.astype(F32)).astype(BF16)
    return d(a_hi, b_hi) + d(a_hi, b_lo) + d(a_lo, b_hi)


_NT = (((1,), (1,)), ((), ()))
_BNN = (((2,), (1,)), ((0,), (0,)))
_BNT = (((2,), (2,)), ((0,), (0,)))


def _split3(x):
    p1 = x.astype(BF16)
    r1 = x - p1.astype(F32)
    p2 = r1.astype(BF16)
    p3 = (r1 - p2.astype(F32)).astype(BF16)
    return p1, p2, p3


def _dot_exact_rhs(x, m_bf16):
    d = lambda a: jnp.dot(a, m_bf16, preferred_element_type=F32)
    p1, p2, p3 = _split3(x)
    return d(p1) + d(p2) + d(p3)


def _dot_exact_lhs(m_bf16, x):
    d = lambda a: jnp.dot(m_bf16, a, preferred_element_type=F32)
    p1, p2, p3 = _split3(x)
    return d(p1) + d(p2) + d(p3)


def _dot_split_w(x, w_hi, w_lo):
    x_hi = x.astype(BF16)
    x_lo = (x - x_hi.astype(F32)).astype(BF16)
    d = lambda a, b: jnp.dot(a, b, preferred_element_type=F32)
    return d(x_hi, w_hi) + d(x_hi, w_lo) + d(x_lo, w_hi)


def _mm_bias_kernel(x_ref, w_ref, b_ref, o_ref, xb_ref):
    @pl.when(pl.program_id(1) == 0)
    def _():
        xb_ref[...] = x_ref[...].astype(BF16)

    o_ref[...] = jnp.dot(xb_ref[...], w_ref[...], preferred_element_type=F32) + b_ref[...]


def _in_proj(x, w_bf16, b, tm, tn):
    n, k = x.shape
    p = w_bf16.shape[1]
    return pl.pallas_call(
        _mm_bias_kernel,
        out_shape=jax.ShapeDtypeStruct((n, p), F32),
        grid=(n // tm, p // tn),
        in_specs=[pl.BlockSpec((tm, k), lambda i, j: (i, 0)),
                  pl.BlockSpec((k, tn), lambda i, j: (0, j)),
                  pl.BlockSpec((1, tn), lambda i, j: (0, j))],
        out_specs=pl.BlockSpec((tm, tn), lambda i, j: (i, j)),
        scratch_shapes=[pltpu.VMEM((tm, k), BF16)],
        compiler_params=pltpu.CompilerParams(dimension_semantics=("arbitrary", "arbitrary")),
        name="in_proj",
    )(x, w_bf16, b)


def _conv_kernel(val_ref, gate_ref, hist_ref, w_ref, cb_ref, g_ref, b_ref, c_ref, tail_ref, ext_ref):
    t = pl.program_id(1)
    tt = val_ref.shape[1]

    @pl.when(t == 0)
    def _():
        ext_ref[0:HIST, :] = hist_ref[0]

    u = val_ref[0] * jax.nn.sigmoid(gate_ref[0])
    ext_ref[HIST:HIST + tt, :] = u
    off = HIST - (CONV_WIDTH - 1)
    acc = jnp.broadcast_to(cb_ref[...], (tt, C_CONV))
    for j in range(CONV_WIDTH):
        acc = acc + w_ref[j:j + 1, :] * ext_ref[off + j:off + j + tt, :]
    mu = jnp.mean(acc, axis=-1, keepdims=True)
    xc = acc - mu
    var = jnp.mean(xc * xc, axis=-1, keepdims=True)
    y = xc * lax.rsqrt(var + LN_EPS) * g_ref[...] + b_ref[...]
    c_ref[0] = (y * jax.nn.sigmoid(y)).astype(c_ref.dtype)
    tail = ext_ref[tt:tt + HIST, :]
    ext_ref[0:HIST, :] = tail
    tail_ref[0] = tail


def _conv_module(proj3, hist, conv_w, conv_b, ln_g, ln_b, tt):
    bsz, t_len, _ = proj3.shape
    nblk = C_CONV // C_CONV
    return pl.pallas_call(
        _conv_kernel,
        out_shape=(jax.ShapeDtypeStruct((bsz, t_len, C_CONV), BF16),
                   jax.ShapeDtypeStruct((bsz, HIST, C_CONV), F32)),
        grid=(bsz, t_len // tt),
        in_specs=[pl.BlockSpec((1, tt, C_CONV), lambda b, t: (b, t, 0)),
                  pl.BlockSpec((1, tt, C_CONV), lambda b, t: (b, t, nblk)),
                  pl.BlockSpec((1, HIST, C_CONV), lambda b, t: (b, 0, 0)),
                  pl.BlockSpec((CONV_WIDTH, C_CONV), lambda b, t: (0, 0)),
                  pl.BlockSpec((1, C_CONV), lambda b, t: (0, 0)),
                  pl.BlockSpec((1, C_CONV), lambda b, t: (0, 0)),
                  pl.BlockSpec((1, C_CONV), lambda b, t: (0, 0))],
        out_specs=(pl.BlockSpec((1, tt, C_CONV), lambda b, t: (b, t, 0)),
                   pl.BlockSpec((1, HIST, C_CONV), lambda b, t: (b, 0, 0))),
        scratch_shapes=[pltpu.VMEM((HIST + tt, C_CONV), F32)],
        compiler_params=pltpu.CompilerParams(dimension_semantics=("arbitrary", "arbitrary")),
        name="conv_module",
    )(proj3, proj3, hist, conv_w, conv_b, ln_g, ln_b)


PREC_CHUNK = 1
PREC_STATE = 1


def _seg_sum(x, e_ref, et_ref):
    return _dot_exact_rhs(_dot_exact_rhs(x, e_ref[...]), et_ref[...])


def _rwkv_kernel(r_ref, k_ref, v_ref, lo_ref, shr_ref, shk_ref, shv_ref, shlo_ref, st0_ref,
                 mur_ref, muk_ref, muv_ref, mulo_ref, w0_ref, w2h_ref, w2l_ref, a0_ref, a2h_ref, a2l_ref,
                 g2h_ref, g2l_ref,
                 kkw_ref, kaw_ref, rkw_ref, lng_ref, lnb_ref, e_ref, et_ref,
                 y_ref, stout_ref, shout_ref,
                 st_sc, pr_sc, pk_sc, pv_sc, plo_sc):
    c = pl.program_id(1)
    n_chunks = pl.num_programs(1)
    L = r_ref.shape[1]
    GL = GROUP_HEADS * L
    log2l = int(math.log2(L))

    @pl.when(c == 0)
    def _():
        st_sc[...] = st0_ref[0]
        pr_sc[...] = shr_ref[0]
        pk_sc[...] = shk_ref[0]
        pv_sc[...] = shv_ref[0]
        plo_sc[...] = shlo_ref[0]

    def token_shift(x_ref, prev_sc, mu_ref):
        x = x_ref[0]
        row = lax.broadcasted_iota(jnp.int32, x.shape, 0)
        xprev = jnp.where(row == 0, jnp.broadcast_to(prev_sc[...], x.shape), pltpu.roll(x, 1, 0))
        prev_sc[...] = x[L - 1:L, :]
        return x + mu_ref[...] * (xprev - x)

    r = token_shift(r_ref, pr_sc, mur_ref)
    k = token_shift(k_ref, pk_sc, muk_ref)
    v = token_shift(v_ref, pv_sc, muv_ref)
    lo = token_shift(lo_ref, plo_sc, mulo_ref)
    xw = lo[:, 0:LANES]
    xa = lo[:, LANES:2 * LANES]
    xg = lo[:, 2 * LANES:LORA_PAD]

    u_dec = w0_ref[...] + _dot_split_w(jnp.tanh(xw), w2h_ref[...], w2l_ref[...])
    logw = (-math.exp(-0.5)) * jax.nn.sigmoid(u_dec)
    a = jax.nn.sigmoid(a0_ref[...] + _dot_split_w(xa, a2h_ref[...], a2l_ref[...]))
    g = _dot_split_w(jax.nn.sigmoid(xg), g2h_ref[...], g2l_ref[...])

    kk = k * kkw_ref[...]
    nrm = jnp.sqrt(_seg_sum(kk * kk, e_ref, et_ref))
    kappa = kk / jnp.maximum(nrm, 1e-12)
    k2 = k * (1.0 + (a - 1.0) * kaw_ref[...])
    bvec = kappa * a
    bonus = _seg_sum(r * k2 * rkw_ref[...], e_ref, et_ref) * v

    ti = lax.broadcasted_iota(jnp.int32, (L, L), 0)
    tj = lax.broadcasted_iota(jnp.int32, (L, L), 1)
    tril = jnp.where(tj <= ti, 1.0, 0.0).astype(BF16)
    cum = _dot_exact_lhs(tril, logw)
    cum_l = cum[L - 1:L, :]
    gam = jnp.exp(cum)
    ginv = jnp.exp(-cum)
    gprev = jnp.exp(cum - logw)
    gtail = jnp.exp(cum_l - cum)
    gam_l = jnp.exp(cum_l)

    kt = kappa * gprev
    kinv = k2 * ginv
    binv = bvec * ginv
    rt = r * gam
    khat = k2 * gtail
    bhat = bvec * gtail

    rr = lax.broadcasted_iota(jnp.int32, (GL, GL), 0)
    cc = lax.broadcasted_iota(jnp.int32, (GL, GL), 1)
    same = (rr >> log2l) == (cc >> log2l)
    tpos = rr & (L - 1)
    jpos = cc & (L - 1)
    mask_s = same & (jpos < tpos)
    mask_i = same & (jpos <= tpos)
    eye = rr == cc
    srow = lax.broadcasted_iota(jnp.int32, (GL, PACK), 0)
    slane = lax.broadcasted_iota(jnp.int32, (GL, PACK), 1)
    bmask = (srow >> log2l) == (slane >> int(math.log2(HEAD)))
    drow = lax.broadcasted_iota(jnp.int32, (PACK, PACK), 0)
    dcol = lax.broadcasted_iota(jnp.int32, (PACK, PACK), 1)
    deye = drow == dcol

    def stack(x):
        x3 = jnp.stack([x[:, gi * PACK:(gi + 1) * PACK] for gi in range(N_GROUPS)], axis=0)
        return jnp.where(bmask[None], jnp.concatenate([x3] * GROUP_HEADS, axis=1), 0.0)

    def bdot(a, b, dims=_BNN):
        return lax.dot_general(a.astype(BF16), b.astype(BF16), dims, preferred_element_type=F32)

    def btrans(x):
        return jnp.stack([x[gi].T for gi in range(N_GROUPS)], axis=0)

    kt_s = stack(kt)
    rt_s = stack(rt)
    binv_s = stack(binv)
    kinv_s = stack(kinv)
    v_s = stack(v)
    khat_s = stack(khat)
    bhat_s = stack(bhat)

    n_mat = jnp.where(mask_s[None], bdot(kt_s, binv_s, _BNT), 0.0)
    a_kk = jnp.where(mask_s[None], bdot(kt_s, kinv_s, _BNT), 0.0)
    a_br = jnp.where(mask_i[None], bdot(rt_s, binv_s, _BNT), 0.0)
    a_kr = jnp.where(mask_i[None], bdot(rt_s, kinv_s, _BNT), 0.0)

    p_mat = -n_mat
    t_mat = jnp.where(eye[None], 1.0, 0.0) + p_mat
    for _ in range(log2l - 1):
        p_mat = bdot(p_mat, p_mat)
        t_mat = t_mat + bdot(t_mat, p_mat)

    w_s = bdot(t_mat, kt_s)
    uv_s = bdot(t_mat, bdot(a_kk, v_s))
    q_s = rt_s - bdot(a_br, w_s)
    y0_s = bdot(a_kr, v_s) - bdot(a_br, uv_s)
    bhat_t = btrans(bhat_s)
    khat_t = btrans(khat_s)
    gl3 = jnp.stack([gam_l[:, gi * PACK:(gi + 1) * PACK] for gi in range(N_GROUPS)], axis=0)
    m_mat = jnp.where(deye[None], jnp.broadcast_to(gl3, (N_GROUPS, PACK, PACK)), 0.0) - bdot(bhat_t, w_s)
    c_mat = bdot(khat_t, v_s) - bdot(bhat_t, uv_s)

    st = st_sc[...]
    ys = bdot(q_s, st) + y0_s
    st_sc[...] = bdot(m_mat, st) + c_mat
    yg = ys[:, 0:L]
    for h in range(1, GROUP_HEADS):
        yg = yg + ys[:, h * L:(h + 1) * L]
    y_groups = [yg[gi] for gi in range(N_GROUPS)]

    y = jnp.concatenate(y_groups, axis=1)
    inv_head = 1.0 / HEAD
    mu = _seg_sum(y, e_ref, et_ref) * inv_head
    yc = y - mu
    var = _seg_sum(yc * yc, e_ref, et_ref) * inv_head
    yn = yc * lax.rsqrt(var + GN_EPS) * lng_ref[...] + lnb_ref[...]
    y_ref[0] = ((yn + bonus) * g).astype(y_ref.dtype)

    @pl.when(c == n_chunks - 1)
    def _():
        stout_ref[0] = st_sc[...]
        shout_ref[0, :, 0:C_RWKV] = pr_sc[...]
        shout_ref[0, :, C_RWKV:2 * C_RWKV] = pk_sc[...]
        shout_ref[0, :, 2 * C_RWKV:3 * C_RWKV] = pv_sc[...]
        shout_ref[0, :, 3 * C_RWKV:3 * C_RWKV + LORA_PAD] = plo_sc[...]


def _rwkv_mix(proj3, shift_parts, st0, params, chunk):
    bsz, t_len, _ = proj3.shape
    L = chunk
    rkv_blk0 = 2 * C_CONV // C_RWKV
    lora_blk = (2 * C_CONV + 3 * C_RWKV) // LORA_PAD
    row = lambda n: pl.BlockSpec((1, n), lambda b, c: (0, 0))
    full = lambda s: pl.BlockSpec(s, lambda b, c: tuple(0 for _ in s))
    sh = lambda n: pl.BlockSpec((1, 1, n), lambda b, c: (b, 0, 0))
    in_specs = [
        pl.BlockSpec((1, L, C_RWKV), lambda b, c: (b, c, rkv_blk0)),
        pl.BlockSpec((1, L, C_RWKV), lambda b, c: (b, c, rkv_blk0 + 1)),
        pl.BlockSpec((1, L, C_RWKV), lambda b, c: (b, c, rkv_blk0 + 2)),
        pl.BlockSpec((1, L, LORA_PAD), lambda b, c: (b, c, lora_blk)),
        sh(C_RWKV), sh(C_RWKV), sh(C_RWKV), sh(LORA_PAD),
        pl.BlockSpec((1, N_GROUPS, PACK, PACK), lambda b, c: (b, 0, 0, 0)),
        row(C_RWKV), row(C_RWKV), row(C_RWKV), row(LORA_PAD),
        row(C_RWKV), full((LANES, C_RWKV)), full((LANES, C_RWKV)),
        row(C_RWKV), full((LANES, C_RWKV)), full((LANES, C_RWKV)),
        full((2 * LANES, C_RWKV)), full((2 * LANES, C_RWKV)),
        row(C_RWKV), row(C_RWKV), row(C_RWKV), row(C_RWKV), row(C_RWKV),
        full((C_RWKV, LANES)), full((LANES, C_RWKV)),
    ]
    out_shape = (jax.ShapeDtypeStruct((bsz, t_len, C_RWKV), BF16),
                 jax.ShapeDtypeStruct((bsz, N_GROUPS, PACK, PACK), F32),
                 jax.ShapeDtypeStruct((bsz, 1, 3 * C_RWKV + LORA_PAD), F32))
    out_specs = (pl.BlockSpec((1, L, C_RWKV), lambda b, c: (b, c, 0)),
                 pl.BlockSpec((1, N_GROUPS, PACK, PACK), lambda b, c: (b, 0, 0, 0)),
                 pl.BlockSpec((1, 1, 3 * C_RWKV + LORA_PAD), lambda b, c: (b, 0, 0)))
    return pl.pallas_call(
        _rwkv_kernel,
        out_shape=out_shape,
        grid=(bsz, t_len // L),
        in_specs=in_specs,
        out_specs=out_specs,
        scratch_shapes=[pltpu.VMEM((N_GROUPS, PACK, PACK), F32),
                        pltpu.VMEM((1, C_RWKV), F32), pltpu.VMEM((1, C_RWKV), F32),
                        pltpu.VMEM((1, C_RWKV), F32), pltpu.VMEM((1, LORA_PAD), F32)],
        compiler_params=pltpu.CompilerParams(dimension_semantics=("arbitrary", "arbitrary")),
        name="rwkv7_mix",
    )(proj3, proj3, proj3, proj3, *shift_parts, st0, *params)


def _outproj_kernel(c_ref, y_ref, x_ref, wa_ref, wb_ref, g_ref, b_ref, rwh_ref, rwl_ref, rb_ref,
                    x1_ref, idx_ref, gate_ref):
    mix = (jnp.dot(c_ref[...], wa_ref[...], preferred_element_type=F32)
           + jnp.dot(y_ref[...], wb_ref[...], preferred_element_type=F32))
    h = ALPHA * x_ref[...] + mix
    mu = jnp.mean(h, axis=-1, keepdims=True)
    hc = h - mu
    var = jnp.mean(hc * hc, axis=-1, keepdims=True)
    x1 = hc * lax.rsqrt(var + LN_EPS) * g_ref[...] + b_ref[...]
    x1_ref[...] = x1
    logits = _dot_split_w(x1, rwh_ref[...], rwl_ref[...]) + rb_ref[...]
    lane = lax.broadcasted_iota(jnp.int32, logits.shape, 1)
    idx_out = jnp.zeros(logits.shape, jnp.int32)
    val_out = jnp.zeros(logits.shape, F32)
    vals = []
    for kk in range(TOP_K):
        m = jnp.max(logits, axis=-1, keepdims=True)
        sel = jnp.min(jnp.where(logits == m, lane, ROUTER_PAD), axis=-1, keepdims=True)
        vals.append(m)
        idx_out = jnp.where(lane == kk, sel, idx_out)
        logits = jnp.where(lane == sel, -jnp.inf, logits)
    exps = [jnp.exp(vv - vals[0]) for vv in vals]
    denom = exps[0]
    for ee in exps[1:]:
        denom = denom + ee
    for kk in range(TOP_K):
        val_out = jnp.where(lane == kk, exps[kk] / denom, val_out)
    idx_ref[...] = idx_out
    gate_ref[...] = val_out


def _out_proj(c2, y2, x2, wa, wb, ln_g, ln_b, rw_hi, rw_lo, rb, tm):
    n = x2.shape[0]
    row = lambda w: pl.BlockSpec((1, w), lambda i: (0, 0))
    return pl.pallas_call(
        _outproj_kernel,
        out_shape=(jax.ShapeDtypeStruct((n, D_MODEL), F32),
                   jax.ShapeDtypeStruct((n, ROUTER_PAD), jnp.int32),
                   jax.ShapeDtypeStruct((n, ROUTER_PAD), F32)),
        grid=(n // tm,),
        in_specs=[pl.BlockSpec((tm, C_CONV), lambda i: (i, 0)),
                  pl.BlockSpec((tm, C_RWKV), lambda i: (i, 0)),
                  pl.BlockSpec((tm, D_MODEL), lambda i: (i, 0)),
                  pl.BlockSpec((C_CONV, D_MODEL), lambda i: (0, 0)),
                  pl.BlockSpec((C_RWKV, D_MODEL), lambda i: (0, 0)),
                  row(D_MODEL), row(D_MODEL),
                  pl.BlockSpec((D_MODEL, ROUTER_PAD), lambda i: (0, 0)),
                  pl.BlockSpec((D_MODEL, ROUTER_PAD), lambda i: (0, 0)),
                  row(ROUTER_PAD)],
        out_specs=(pl.BlockSpec((tm, D_MODEL), lambda i: (i, 0)),
                   pl.BlockSpec((tm, ROUTER_PAD), lambda i: (i, 0)),
                   pl.BlockSpec((tm, ROUTER_PAD), lambda i: (i, 0))),
        compiler_params=pltpu.CompilerParams(dimension_semantics=("arbitrary",)),
        name="out_proj_ln_router",
    )(c2, y2, x2, wa, wb, ln_g, ln_b, rw_hi, rw_lo, rb)


def _gather_rows_kernel(tok_ref, x_hbm, o_ref, sem):
    rows = o_ref.shape[0]

    def issue(r, carry):
        t = tok_ref[0, 0, r]
        pltpu.make_async_copy(x_hbm.at[pl.ds(t, 1)], o_ref.at[pl.ds(r, 1)], sem).start()
        return carry

    lax.fori_loop(0, rows, issue, 0, unroll=8)

    def drain(r, carry):
        pltpu.make_async_copy(x_hbm.at[pl.ds(0, 1)], o_ref.at[pl.ds(r, 1)], sem).wait()
        return carry

    lax.fori_loop(0, rows, drain, 0, unroll=8)


def _gather_rows(x, row_tok3):
    nb, _, rows = row_tok3.shape
    return pl.pallas_call(
        _gather_rows_kernel,
        out_shape=jax.ShapeDtypeStruct((nb * rows, x.shape[1]), x.dtype),
        grid=(nb,),
        in_specs=[pl.BlockSpec((1, 1, rows), lambda i: (i, 0, 0), memory_space=pltpu.SMEM),
                  pl.BlockSpec(memory_space=pl.ANY)],
        out_specs=pl.BlockSpec((rows, x.shape[1]), lambda i: (i, 0)),
        scratch_shapes=[pltpu.SemaphoreType.DMA],
        compiler_params=pltpu.CompilerParams(dimension_semantics=("arbitrary",)),
        name="moe_gather_rows",
    )(row_tok3, x)


def _expert_kernel(bexp_ref, nused_ref, x_ref, wg_ref, bg_ref, wu_ref, bu_ref, wd_ref, bd_ref,
                   o_ref, xb_ref, acc_ref):
    i = pl.program_id(0)
    j = pl.program_id(1)
    nj = pl.num_programs(1)

    @pl.when(i < nused_ref[0])
    def _():
        @pl.when(j == 0)
        def _():
            xb_ref[...] = x_ref[...].astype(BF16)
            acc_ref[...] = jnp.zeros_like(acc_ref)

        xb = xb_ref[...]
        gate = jnp.minimum(jnp.dot(xb, wg_ref[0], preferred_element_type=F32) + bg_ref[0], SWIGLU_LIMIT)
        up = jnp.clip(jnp.dot(xb, wu_ref[0], preferred_element_type=F32) + bu_ref[0],
                      -SWIGLU_LIMIT, SWIGLU_LIMIT)
        hmid = (up + 1.0) * gate * jax.nn.sigmoid(SWIGLU_ALPHA * gate)
        acc_ref[...] += jnp.dot(hmid.astype(BF16), wd_ref[0], preferred_element_type=F32)

        @pl.when(j == nj - 1)
        def _():
            o_ref[...] = acc_ref[...] + bd_ref[0]

    @pl.when((i >= nused_ref[0]) & (j == nj - 1))
    def _():
        o_ref[...] = jnp.zeros_like(o_ref)


def _experts(block_exp, n_used, xb, wg, bg, wu, bu, wd, bd):
    n_rows = xb.shape[0]
    nb = n_rows // MOE_TM
    nj = D_FF // MOE_TF

    def jj(i, j, nu):
        return jnp.where(i < nu[0], j, nj - 1)

    grid_spec = pltpu.PrefetchScalarGridSpec(
        num_scalar_prefetch=2,
        grid=(nb, nj),
        in_specs=[pl.BlockSpec((MOE_TM, D_MODEL), lambda i, j, be, nu: (i, 0)),
                  pl.BlockSpec((1, D_MODEL, MOE_TF), lambda i, j, be, nu: (be[i], 0, jj(i, j, nu))),
                  pl.BlockSpec((1, 1, MOE_TF), lambda i, j, be, nu: (be[i], 0, jj(i, j, nu))),
                  pl.BlockSpec((1, D_MODEL, MOE_TF), lambda i, j, be, nu: (be[i], 0, jj(i, j, nu))),
                  pl.BlockSpec((1, 1, MOE_TF), lambda i, j, be, nu: (be[i], 0, jj(i, j, nu))),
                  pl.BlockSpec((1, MOE_TF, D_MODEL), lambda i, j, be, nu: (be[i], jj(i, j, nu), 0)),
                  pl.BlockSpec((1, 1, D_MODEL), lambda i, j, be, nu: (be[i], 0, 0))],
        out_specs=pl.BlockSpec((MOE_TM, D_MODEL), lambda i, j, be, nu: (i, 0)),
        scratch_shapes=[pltpu.VMEM((MOE_TM, D_MODEL), BF16), pltpu.VMEM((MOE_TM, D_MODEL), F32)],
    )
    return pl.pallas_call(
        _expert_kernel,
        out_shape=jax.ShapeDtypeStruct((n_rows, D_MODEL), F32),
        grid_spec=grid_spec,
        compiler_params=pltpu.CompilerParams(dimension_semantics=("arbitrary", "arbitrary")),
        name="moe_experts",
    )(block_exp, n_used, xb, wg, bg, wu, bu, wd, bd)


def _combine_kernel(dest_ref, gate_ref, x1_ref, g_ref, b_ref, yb_hbm, o_ref, buf_ref, sem):
    tc = x1_ref.shape[0]

    def issue(r, carry):
        for kk in range(TOP_K):
            d = dest_ref[0, 0, r * TOP_K + kk]
            pltpu.make_async_copy(yb_hbm.at[pl.ds(d, 1)], buf_ref.at[kk, pl.ds(r, 1)], sem).start()
        return carry

    lax.fori_loop(0, tc, issue, 0, unroll=2)

    def drain(r, carry):
        for kk in range(TOP_K):
            pltpu.make_async_copy(yb_hbm.at[pl.ds(0, 1)], buf_ref.at[kk, pl.ds(r, 1)], sem).wait()
        return carry

    lax.fori_loop(0, tc, drain, 0, unroll=2)

    gates = gate_ref[...]
    moe = gates[:, 0:1] * buf_ref[0]
    for kk in range(1, TOP_K):
        moe = moe + gates[:, kk:kk + 1] * buf_ref[kk]
    h = ALPHA * x1_ref[...] + moe
    mu = jnp.mean(h, axis=-1, keepdims=True)
    hc = h - mu
    var = jnp.mean(hc * hc, axis=-1, keepdims=True)
    o_ref[...] = hc * lax.rsqrt(var + LN_EPS) * g_ref[...] + b_ref[...]


def _combine(dest3, gates, x1, ln_g, ln_b, yb, tc):
    n = dest3.shape[0] * tc
    return pl.pallas_call(
        _combine_kernel,
        out_shape=jax.ShapeDtypeStruct((n, D_MODEL), F32),
        grid=(n // tc,),
        in_specs=[pl.BlockSpec((1, 1, tc * TOP_K), lambda i: (i, 0, 0), memory_space=pltpu.SMEM),
                  pl.BlockSpec((tc, ROUTER_PAD), lambda i: (i, 0)),
                  pl.BlockSpec((tc, D_MODEL), lambda i: (i, 0)),
                  pl.BlockSpec((1, D_MODEL), lambda i: (0, 0)),
                  pl.BlockSpec((1, D_MODEL), lambda i: (0, 0)),
                  pl.BlockSpec(memory_space=pl.ANY)],
        out_specs=pl.BlockSpec((tc, D_MODEL), lambda i: (i, 0)),
        scratch_shapes=[pltpu.VMEM((TOP_K, tc, D_MODEL), F32), pltpu.SemaphoreType.DMA],
        compiler_params=pltpu.CompilerParams(dimension_semantics=("arbitrary",)),
        name="moe_combine_ln2",
    )(dest3, gates, x1, ln_g, ln_b, yb)


def _pad_cols(w, width):
    return jnp.pad(w, ((0, 0), (0, width - w.shape[1])))


def _pad_rows(w, height):
    return jnp.pad(w, ((0, height - w.shape[0]), (0, 0)))


def _split_lora_cols(w):
    xw = w[..., 0:R_DECAY]
    xa = w[..., R_DECAY:R_DECAY + R_ICLR]
    xg = w[..., R_DECAY + R_ICLR:]
    pad = lambda x, n: jnp.pad(x, [(0, 0)] * (x.ndim - 1) + [(0, n - x.shape[-1])])
    return jnp.concatenate([pad(xw, LANES), pad(xa, LANES), pad(xg, 2 * LANES)], axis=-1)


def _pick(n, prefs):
    for p in prefs:
        if n % p == 0:
            return p
    return n


def _mixer_group(x, conv_buf, shift_buf, wkv_state, wts):
    bsz, t_len, _ = x.shape
    n = bsz * t_len
    proj = _in_proj(x.reshape(n, D_MODEL), wts["w_in"], wts["b_in"], _pick(n, (1024, 512, 256, 128)), 512)
    proj3 = proj.reshape(bsz, t_len, P_PAD)

    hist = jnp.pad(conv_buf, ((0, 0), (HIST - (CONV_WIDTH - 1), 0), (0, 0)))
    c, tail = _conv_module(proj3, hist, wts["conv_w"], wts["conv_b"], wts["conv_ln_g"], wts["conv_ln_b"],
                           _pick(t_len, (128, 64, 32, 16, 8)))
    new_conv = tail[:, HIST - (CONV_WIDTH - 1):, :]

    sh_rkv = shift_buf[:, :, :3 * C_RWKV]
    sh_lo = _split_lora_cols(shift_buf[:, :, 3 * C_RWKV:])
    shift_parts = (sh_rkv[:, :, 0:C_RWKV], sh_rkv[:, :, C_RWKV:2 * C_RWKV], sh_rkv[:, :, 2 * C_RWKV:], sh_lo)
    st_t = jnp.swapaxes(wkv_state, -1, -2).reshape(bsz, N_GROUPS, GROUP_HEADS, HEAD, HEAD)
    eye_h = jnp.eye(GROUP_HEADS, dtype=F32)
    st0 = jnp.einsum("bghkv,hj->bghkjv", st_t, eye_h).reshape(bsz, N_GROUPS, PACK, PACK)
    yb, st_out, sh_out = _rwkv_mix(proj3, shift_parts, st0, wts["rwkv_params"], _pick(t_len, (64, 32, 16)))
    st5 = st_out.reshape(bsz, N_GROUPS, GROUP_HEADS, HEAD, GROUP_HEADS, HEAD)
    st_diag = jnp.einsum("bghkhv->bghkv", st5)
    new_wkv = jnp.swapaxes(st_diag, -1, -2).reshape(bsz, N_HEADS, HEAD, HEAD)
    lo = sh_out[:, :, 3 * C_RWKV:]
    new_shift = jnp.concatenate([sh_out[:, :, :3 * C_RWKV], lo[:, :, 0:R_DECAY], lo[:, :, LANES:LANES + R_ICLR],
                                 lo[:, :, 2 * LANES:2 * LANES + R_GATE]], axis=-1)
    return c.reshape(n, C_CONV), yb.reshape(n, C_RWKV), new_conv, new_shift, new_wkv


def _route(top_idx, n_tok):
    n_assign = n_tok * TOP_K
    flat_e = top_idx.reshape(-1)
    onehot = (flat_e[:, None] == jnp.arange(N_EXPERTS, dtype=jnp.int32)[None, :]).astype(jnp.int32)
    csum = jnp.cumsum(onehot, axis=0)
    rank = jnp.take_along_axis(csum, flat_e[:, None], axis=1)[:, 0] - 1
    counts = csum[-1]
    padded = (counts + MOE_TM - 1) // MOE_TM * MOE_TM
    seg_end = jnp.cumsum(padded)
    seg_start = seg_end - padded
    dest = (seg_start[flat_e] + rank).astype(jnp.int32)
    n_rows = (n_assign + N_EXPERTS * (MOE_TM - 1) + MOE_TM - 1) // MOE_TM * MOE_TM
    n_blocks = n_rows // MOE_TM
    row_tok = jnp.zeros((n_rows,), jnp.int32).at[dest].set(jnp.arange(n_assign, dtype=jnp.int32) // TOP_K)
    block_exp = jnp.minimum(
        jnp.searchsorted(seg_end, jnp.arange(n_blocks, dtype=jnp.int32) * MOE_TM, side="right"),
        N_EXPERTS - 1).astype(jnp.int32)
    n_used = (seg_end[-1] // MOE_TM).astype(jnp.int32).reshape(1)
    return dest, row_tok, block_exp, n_used, n_blocks


def kernel(x_prompt, x_sample, state_conv, state_shift, state_wkv, w_in, b_in, mu_shift, conv_w, conv_b,
           conv_ln_g, conv_ln_b, rwkv_w0, rwkv_w2, rwkv_a0, rwkv_a2, rwkv_g2, rwkv_k_k, rwkv_k_a, rwkv_r_k,
           rwkv_ln_g, rwkv_ln_b, w_out, ln1_g, ln1_b, router_w, router_b, w_gate, b_gate, w_up, b_up,
           w_down, b_down, ln2_g, ln2_b):
    assert w_in.shape[0] == 1, "single layer"
    d = 0
    row = lambda v: v.reshape(1, -1)
    n_p, t_p, _ = x_prompt.shape
    n_s, t_s, _ = x_sample.shape

    w_rkv = w_in[d][:, 2 * C_CONV:2 * C_CONV + 3 * C_RWKV]
    w_lo = _split_lora_cols(w_in[d][:, 2 * C_CONV + 3 * C_RWKV:])
    w_in_p = jnp.concatenate([w_in[d][:, :2 * C_CONV], w_rkv, w_lo], axis=1).astype(BF16)
    b_in_p = jnp.concatenate([b_in[d][None, :2 * C_CONV], b_in[d][None, 2 * C_CONV:2 * C_CONV + 3 * C_RWKV],
                              _split_lora_cols(b_in[d][None, 2 * C_CONV + 3 * C_RWKV:])], axis=1)
    mu = mu_shift[d][None, :]
    mu_lo = _split_lora_cols(mu[:, 3 * C_RWKV:])
    head_of_lane = jnp.arange(C_RWKV, dtype=jnp.int32) // HEAD
    e_mat = (head_of_lane[:, None] == jnp.arange(LANES, dtype=jnp.int32)[None, :]).astype(BF16)

    def hi_lo(w, height):
        w = _pad_rows(w, height)
        w_hi = w.astype(BF16)
        return w_hi, (w - w_hi.astype(F32)).astype(BF16)

    rwkv_params = (
        mu[:, 0:C_RWKV], mu[:, C_RWKV:2 * C_RWKV], mu[:, 2 * C_RWKV:3 * C_RWKV], mu_lo,
        row(rwkv_w0[d]), *hi_lo(rwkv_w2[d], LANES), row(rwkv_a0[d]), *hi_lo(rwkv_a2[d], LANES),
        *hi_lo(rwkv_g2[d], 2 * LANES),
        row(rwkv_k_k[d]), row(rwkv_k_a[d]), row(rwkv_r_k[d]), row(rwkv_ln_g[d]), row(rwkv_ln_b[d]),
        e_mat, e_mat.T,
    )
    wts = dict(w_in=w_in_p, b_in=b_in_p, conv_w=conv_w[d], conv_b=row(conv_b[d]),
               conv_ln_g=row(conv_ln_g[d]), conv_ln_b=row(conv_ln_b[d]), rwkv_params=rwkv_params)

    zero_conv = jnp.zeros((n_p, CONV_WIDTH - 1, C_CONV), x_prompt.dtype)
    zero_shift = jnp.zeros((n_p, 1, N_SHIFT), x_prompt.dtype)
    zero_wkv = jnp.zeros((n_p, N_HEADS, HEAD, HEAD), state_wkv.dtype)
    c_p, y_p, conv_p, shift_p, wkv_p = _mixer_group(x_prompt, zero_conv, zero_shift, zero_wkv, wts)
    c_s, y_s, conv_s, shift_s, wkv_s = _mixer_group(x_sample, state_conv[d], state_shift[d], state_wkv[d], wts)

    n_tok = n_p * t_p + n_s * t_s
    tm_tok = min(OUT_TM, n_tok)
    n_pad = -n_tok % tm_tok
    tail = lambda dt, w: [jnp.zeros((n_pad, w), dt)] if n_pad else []
    c_all = jnp.concatenate([c_p, c_s] + tail(BF16, C_CONV), axis=0)
    y_all = jnp.concatenate([y_p, y_s] + tail(BF16, C_RWKV), axis=0)
    x_all = jnp.concatenate([x_prompt.reshape(-1, D_MODEL), x_sample.reshape(-1, D_MODEL)] + tail(F32, D_MODEL),
                            axis=0)
    w_out_b = w_out[d].astype(BF16)
    rw = _pad_cols(router_w[d], ROUTER_PAD)
    rw_hi = rw.astype(BF16)
    rw_lo = (rw - rw_hi.astype(F32)).astype(BF16)
    rb = jnp.concatenate([router_b[d], jnp.full((ROUTER_PAD - N_EXPERTS,), -jnp.inf, F32)])[None, :]
    x1, idx_pad, gate_pad = _out_proj(c_all, y_all, x_all, w_out_b[:C_CONV], w_out_b[C_CONV:], row(ln1_g[d]),
                                      row(ln1_b[d]), rw_hi, rw_lo, rb, tm_tok)
    idx_pad = idx_pad[:n_tok]

    dest, row_tok, block_exp, n_used, n_blocks = _route(idx_pad[:, :TOP_K], n_tok)
    xb = _gather_rows(x1, row_tok.reshape(n_blocks, 1, MOE_TM))
    yb = _experts(block_exp, n_used, xb, w_gate[d].astype(BF16), b_gate[d][:, None, :],
                  w_up[d].astype(BF16), b_up[d][:, None, :], w_down[d].astype(BF16), b_down[d][:, None, :])
    tc = _pick(n_tok, (128,))
    out = _combine(dest.reshape(n_tok // tc, 1, tc * TOP_K), gate_pad, x1, row(ln2_g[d]), row(ln2_b[d]), yb, tc)

    n_prompt_tok = n_p * t_p
    y_prompt = out[:n_prompt_tok].reshape(n_p, t_p, D_MODEL)
    y_sample = out[n_prompt_tok:].reshape(n_s, t_s, D_MODEL)
    return (y_prompt, y_sample, conv_p[None], shift_p[None], wkv_p[None], conv_s[None], shift_s[None], wkv_s[None])
```

```python
import functools
import math

import jax
import jax.numpy as jnp
from jax import lax
from jax.experimental import pallas as pl
from jax.experimental.pallas import tpu as pltpu

F32 = jnp.float32
BF16 = jnp.bfloat16

D_MODEL = 2048
C_CONV = 1024
C_RWKV = 1024
HEAD = 64
N_HEADS = C_RWKV // HEAD
CONV_WIDTH = 31
R_DECAY = 64
R_ICLR = 64
R_GATE = 160
N_SHIFT = 3 * C_RWKV + R_DECAY + R_ICLR + R_GATE
N_EXPERTS = 32
TOP_K = 4
D_FF = 2048
SWIGLU_LIMIT = 7.0
SWIGLU_ALPHA = 1.702
LN_EPS = 1e-5
GN_EPS = 64e-5
ALPHA = 2.0 ** 0.25

LANES = 128
SUBLANES = 8

HIST = 32
LORA_PAD = 512
P_PAD = 2 * C_CONV + 3 * C_RWKV + LORA_PAD
GROUP_HEADS = 2
PACK = GROUP_HEADS * HEAD
N_GROUPS = N_HEADS // GROUP_HEADS
MOE_TM = 512
MOE_TF = 256
OUT_TM = 512
MOE_VMEM_LIMIT = 60 * 1024 * 1024
ROUTER_PAD = LANES


def _dot(a, b, prec=1, dims=(((1,), (0,)), ((), ()))):
    if prec == 6:
        return lax.dot_general(a.astype(F32), b.astype(F32), dims, precision=lax.Precision.HIGHEST,
                               preferred_element_type=F32)
    d = lambda x, y: lax.dot_general(x, y, dims, preferred_element_type=F32)
    if prec == 1:
        return d(a.astype(BF16), b.astype(BF16))
    a_hi = a.astype(BF16)
    a_lo = (a - a_hi.astype(F32)).astype(BF16)
    b_hi = b.astype(BF16)
    b_lo = (b - b_hi.astype(F32)).astype(BF16)
    return d(a_hi, b_hi) + d(a_hi, b_lo) + d(a_lo, b_hi)


_NT = (((1,), (1,)), ((), ()))
_BNN = (((2,), (1,)), ((0,), (0,)))
_BNT = (((2,), (2,)), ((0,), (0,)))


def _split3(x):
    p1 = x.astype(BF16)
    r1 = x - p1.astype(F32)
    p2 = r1.astype(BF16)
    p3 = (r1 - p2.astype(F32)).astype(BF16)
    return p1, p2, p3


def _dot_exact_rhs(x, m_bf16):
    d = lambda a: jnp.dot(a, m_bf16, preferred_element_type=F32)
    p1, p2, p3 = _split3(x)
    return d(p1) + d(p2) + d(p3)


def _dot_exact_lhs(m_bf16, x):
    d = lambda a: jnp.dot(m_bf16, a, preferred_element_type=F32)
    p1, p2, p3 = _split3(x)
    return d(p1) + d(p2) + d(p3)


def _dot_split_w(x, w_hi, w_lo):
    x_hi = x.astype(BF16)
    x_lo = (x - x_hi.astype(F32)).astype(BF16)
    d = lambda a, b: jnp.dot(a, b, preferred_element_type=F32)
    return d(x_hi, w_hi) + d(x_hi, w_lo) + d(x_lo, w_hi)


def _mm_bias_kernel(x_ref, w_ref, b_ref, o_ref, xb_ref):
    @pl.when(pl.program_id(1) == 0)
    def _():
        xb_ref[...] = x_ref[...].astype(BF16)

    o_ref[...] = jnp.dot(xb_ref[...], w_ref[...], preferred_element_type=F32) + b_ref[...]


def _in_proj(x, w_bf16, b, tm, tn):
    n, k = x.shape
    p = w_bf16.shape[1]
    return pl.pallas_call(
        _mm_bias_kernel,
        out_shape=jax.ShapeDtypeStruct((n, p), F32),
        grid=(n // tm, p // tn),
        in_specs=[pl.BlockSpec((tm, k), lambda i, j: (i, 0)),
                  pl.BlockSpec((k, tn), lambda i, j: (0, j)),
                  pl.BlockSpec((1, tn), lambda i, j: (0, j))],
        out_specs=pl.BlockSpec((tm, tn), lambda i, j: (i, j)),
        scratch_shapes=[pltpu.VMEM((tm, k), BF16)],
        compiler_params=pltpu.CompilerParams(dimension_semantics=("arbitrary", "arbitrary")),
        name="in_proj",
    )(x, w_bf16, b)


def _conv_kernel(val_ref, gate_ref, hist_ref, w_ref, cb_ref, g_ref, b_ref, c_ref, tail_ref, ext_ref, sh_ref):
    t = pl.program_id(1)
    tt = val_ref.shape[1]

    @pl.when(t == 0)
    def _():
        ext_ref[0:HIST, :] = hist_ref[0]

    u = val_ref[0] * jax.nn.sigmoid(gate_ref[0])
    ext_ref[HIST:HIST + tt, :] = u
    span = tt + HIST - SUBLANES
    for s in range(1, SUBLANES):
        sh_ref[s, 0:span, :] = ext_ref[s:s + span, :]
    off = HIST - (CONV_WIDTH - 1)
    acc = jnp.broadcast_to(cb_ref[...], (tt, C_CONV))
    for j in range(CONV_WIDTH):
        base = (off + j) // SUBLANES * SUBLANES
        s = (off + j) % SUBLANES
        src = ext_ref[base:base + tt, :] if s == 0 else sh_ref[s, base:base + tt, :]
        acc = acc + w_ref[j:j + 1, :] * src
    mu = jnp.mean(acc, axis=-1, keepdims=True)
    xc = acc - mu
    var = jnp.mean(xc * xc, axis=-1, keepdims=True)
    y = xc * lax.rsqrt(var + LN_EPS) * g_ref[...] + b_ref[...]
    c_ref[0] = (y * jax.nn.sigmoid(y)).astype(c_ref.dtype)
    tail = ext_ref[tt:tt + HIST, :]
    ext_ref[0:HIST, :] = tail
    tail_ref[0] = tail


def _conv_module(proj3, hist, conv_w, conv_b, ln_g, ln_b, tt):
    bsz, t_len, _ = proj3.shape
    nblk = C_CONV // C_CONV
    return pl.pallas_call(
        _conv_kernel,
        out_shape=(jax.ShapeDtypeStruct((bsz, t_len, C_CONV), BF16),
                   jax.ShapeDtypeStruct((bsz, HIST, C_CONV), F32)),
        grid=(bsz, t_len // tt),
        in_specs=[pl.BlockSpec((1, tt, C_CONV), lambda b, t: (b, t, 0)),
                  pl.BlockSpec((1, tt, C_CONV), lambda b, t: (b, t, nblk)),
                  pl.BlockSpec((1, HIST, C_CONV), lambda b, t: (b, 0, 0)),
                  pl.BlockSpec((CONV_WIDTH, C_CONV), lambda b, t: (0, 0)),
                  pl.BlockSpec((1, C_CONV), lambda b, t: (0, 0)),
                  pl.BlockSpec((1, C_CONV), lambda b, t: (0, 0)),
                  pl.BlockSpec((1, C_CONV), lambda b, t: (0, 0))],
        out_specs=(pl.BlockSpec((1, tt, C_CONV), lambda b, t: (b, t, 0)),
                   pl.BlockSpec((1, HIST, C_CONV), lambda b, t: (b, 0, 0))),
        scratch_shapes=[pltpu.VMEM((HIST + tt, C_CONV), F32),
                        pltpu.VMEM((SUBLANES, HIST + tt, C_CONV), F32)],
        compiler_params=pltpu.CompilerParams(dimension_semantics=("arbitrary", "arbitrary")),
        name="conv_module",
    )(proj3, proj3, hist, conv_w, conv_b, ln_g, ln_b)


PREC_CHUNK = 1
PREC_STATE = 1


def _seg_sum(x, e_ref, et_ref):
    return _dot_exact_rhs(_dot_exact_rhs(x, e_ref[...]), et_ref[...])


def _rwkv_kernel(r_ref, k_ref, v_ref, lo_ref, shr_ref, shk_ref, shv_ref, shlo_ref, st0_ref,
                 mur_ref, muk_ref, muv_ref, mulo_ref, w0_ref, w2h_ref, w2l_ref, a0_ref, a2h_ref, a2l_ref,
                 g2h_ref, g2l_ref,
                 kkw_ref, kaw_ref, rkw_ref, lng_ref, lnb_ref, e_ref, et_ref,
                 y_ref, stout_ref, shout_ref,
                 st_sc, pr_sc, pk_sc, pv_sc, plo_sc):
    c = pl.program_id(1)
    n_chunks = pl.num_programs(1)
    L = r_ref.shape[1]
    GL = GROUP_HEADS * L
    log2l = int(math.log2(L))

    @pl.when(c == 0)
    def _():
        st_sc[...] = st0_ref[0]
        pr_sc[...] = shr_ref[0]
        pk_sc[...] = shk_ref[0]
        pv_sc[...] = shv_ref[0]
        plo_sc[...] = shlo_ref[0]

    def token_shift(x_ref, prev_sc, mu_ref):
        x = x_ref[0]
        row = lax.broadcasted_iota(jnp.int32, x.shape, 0)
        xprev = jnp.where(row == 0, jnp.broadcast_to(prev_sc[...], x.shape), pltpu.roll(x, 1, 0))
        prev_sc[...] = x[L - 1:L, :]
        return x + mu_ref[...] * (xprev - x)

    r = token_shift(r_ref, pr_sc, mur_ref)
    k = token_shift(k_ref, pk_sc, muk_ref)
    v = token_shift(v_ref, pv_sc, muv_ref)
    lo = token_shift(lo_ref, plo_sc, mulo_ref)
    xw = lo[:, 0:LANES]
    xa = lo[:, LANES:2 * LANES]
    xg = lo[:, 2 * LANES:LORA_PAD]

    u_dec = w0_ref[...] + _dot_split_w(jnp.tanh(xw), w2h_ref[...], w2l_ref[...])
    logw = (-math.exp(-0.5)) * jax.nn.sigmoid(u_dec)
    a = jax.nn.sigmoid(a0_ref[...] + _dot_split_w(xa, a2h_ref[...], a2l_ref[...]))
    g = _dot_split_w(jax.nn.sigmoid(xg), g2h_ref[...], g2l_ref[...])

    kk = k * kkw_ref[...]
    nrm = jnp.sqrt(_seg_sum(kk * kk, e_ref, et_ref))
    kappa = kk / jnp.maximum(nrm, 1e-12)
    k2 = k * (1.0 + (a - 1.0) * kaw_ref[...])
    bvec = kappa * a
    bonus = _seg_sum(r * k2 * rkw_ref[...], e_ref, et_ref) * v

    ti = lax.broadcasted_iota(jnp.int32, (L, L), 0)
    tj = lax.broadcasted_iota(jnp.int32, (L, L), 1)
    tril = jnp.where(tj <= ti, 1.0, 0.0).astype(BF16)
    cum = _dot_exact_lhs(tril, logw)
    cum_l = cum[L - 1:L, :]
    gam = jnp.exp(cum)
    ginv = jnp.exp(-cum)
    gprev = jnp.exp(cum - logw)
    gtail = jnp.exp(cum_l - cum)
    gam_l = jnp.exp(cum_l)

    kt = kappa * gprev
    kinv = k2 * ginv
    binv = bvec * ginv
    rt = r * gam
    khat = k2 * gtail
    bhat = bvec * gtail

    rr = lax.broadcasted_iota(jnp.int32, (GL, GL), 0)
    cc = lax.broadcasted_iota(jnp.int32, (GL, GL), 1)
    same = (rr >> log2l) == (cc >> log2l)
    tpos = rr & (L - 1)
    jpos = cc & (L - 1)
    mask_s = same & (jpos < tpos)
    mask_i = same & (jpos <= tpos)
    eye = rr == cc
    srow = lax.broadcasted_iota(jnp.int32, (GL, PACK), 0)
    slane = lax.broadcasted_iota(jnp.int32, (GL, PACK), 1)
    bmask = (srow >> log2l) == (slane >> int(math.log2(HEAD)))
    drow = lax.broadcasted_iota(jnp.int32, (PACK, PACK), 0)
    dcol = lax.broadcasted_iota(jnp.int32, (PACK, PACK), 1)
    deye = drow == dcol

    def stack(x):
        x3 = jnp.stack([x[:, gi * PACK:(gi + 1) * PACK] for gi in range(N_GROUPS)], axis=0)
        return jnp.where(bmask[None], jnp.concatenate([x3] * GROUP_HEADS, axis=1), 0.0)

    def bdot(a, b, dims=_BNN):
        return lax.dot_general(a.astype(BF16), b.astype(BF16), dims, preferred_element_type=F32)

    def btrans(x):
        return jnp.stack([x[gi].T for gi in range(N_GROUPS)], axis=0)

    kt_s = stack(kt)
    rt_s = stack(rt)
    binv_s = stack(binv)
    kinv_s = stack(kinv)
    v_s = stack(v)
    khat_s = stack(khat)
    bhat_s = stack(bhat)

    n_mat = jnp.where(mask_s[None], bdot(kt_s, binv_s, _BNT), 0.0)
    a_kk = jnp.where(mask_s[None], bdot(kt_s, kinv_s, _BNT), 0.0)
    a_br = jnp.where(mask_i[None], bdot(rt_s, binv_s, _BNT), 0.0)
    a_kr = jnp.where(mask_i[None], bdot(rt_s, kinv_s, _BNT), 0.0)

    p_mat = -n_mat
    t_mat = jnp.where(eye[None], 1.0, 0.0) + p_mat
    for _ in range(log2l - 1):
        p_mat = bdot(p_mat, p_mat)
        t_mat = t_mat + bdot(t_mat, p_mat)

    w_s = bdot(t_mat, kt_s)
    uv_s = bdot(t_mat, bdot(a_kk, v_s))
    q_s = rt_s - bdot(a_br, w_s)
    y0_s = bdot(a_kr, v_s) - bdot(a_br, uv_s)
    bhat_t = btrans(bhat_s)
    khat_t = btrans(khat_s)
    gl3 = jnp.stack([gam_l[:, gi * PACK:(gi + 1) * PACK] for gi in range(N_GROUPS)], axis=0)
    m_mat = jnp.where(deye[None], jnp.broadcast_to(gl3, (N_GROUPS, PACK, PACK)), 0.0) - bdot(bhat_t, w_s)
    c_mat = bdot(khat_t, v_s) - bdot(bhat_t, uv_s)

    st = st_sc[...]
    ys = bdot(q_s, st) + y0_s
    st_sc[...] = bdot(m_mat, st) + c_mat
    yg = ys[:, 0:L]
    for h in range(1, GROUP_HEADS):
        yg = yg + ys[:, h * L:(h + 1) * L]
    y_groups = [yg[gi] for gi in range(N_GROUPS)]

    y = jnp.concatenate(y_groups, axis=1)
    inv_head = 1.0 / HEAD
    mu = _seg_sum(y, e_ref, et_ref) * inv_head
    yc = y - mu
    var = _seg_sum(yc * yc, e_ref, et_ref) * inv_head
    yn = yc * lax.rsqrt(var + GN_EPS) * lng_ref[...] + lnb_ref[...]
    y_ref[0] = ((yn + bonus) * g).astype(y_ref.dtype)

    @pl.when(c == n_chunks - 1)
    def _():
        stout_ref[0] = st_sc[...]
        shout_ref[0, :, 0:C_RWKV] = pr_sc[...]
        shout_ref[0, :, C_RWKV:2 * C_RWKV] = pk_sc[...]
        shout_ref[0, :, 2 * C_RWKV:3 * C_RWKV] = pv_sc[...]
        shout_ref[0, :, 3 * C_RWKV:3 * C_RWKV + LORA_PAD] = plo_sc[...]


def _rwkv_mix(proj3, shift_parts, st0, params, chunk):
    bsz, t_len, _ = proj3.shape
    L = chunk
    rkv_blk0 = 2 * C_CONV // C_RWKV
    lora_blk = (2 * C_CONV + 3 * C_RWKV) // LORA_PAD
    row = lambda n: pl.BlockSpec((1, n), lambda b, c: (0, 0))
    full = lambda s: pl.BlockSpec(s, lambda b, c: tuple(0 for _ in s))
    sh = lambda n: pl.BlockSpec((1, 1, n), lambda b, c: (b, 0, 0))
    in_specs = [
        pl.BlockSpec((1, L, C_RWKV), lambda b, c: (b, c, rkv_blk0)),
        pl.BlockSpec((1, L, C_RWKV), lambda b, c: (b, c, rkv_blk0 + 1)),
        pl.BlockSpec((1, L, C_RWKV), lambda b, c: (b, c, rkv_blk0 + 2)),
        pl.BlockSpec((1, L, LORA_PAD), lambda b, c: (b, c, lora_blk)),
        sh(C_RWKV), sh(C_RWKV), sh(C_RWKV), sh(LORA_PAD),
        pl.BlockSpec((1, N_GROUPS, PACK, PACK), lambda b, c: (b, 0, 0, 0)),
        row(C_RWKV), row(C_RWKV), row(C_RWKV), row(LORA_PAD),
        row(C_RWKV), full((LANES, C_RWKV)), full((LANES, C_RWKV)),
        row(C_RWKV), full((LANES, C_RWKV)), full((LANES, C_RWKV)),
        full((2 * LANES, C_RWKV)), full((2 * LANES, C_RWKV)),
        row(C_RWKV), row(C_RWKV), row(C_RWKV), row(C_RWKV), row(C_RWKV),
        full((C_RWKV, LANES)), full((LANES, C_RWKV)),
    ]
    out_shape = (jax.ShapeDtypeStruct((bsz, t_len, C_RWKV), BF16),
                 jax.ShapeDtypeStruct((bsz, N_GROUPS, PACK, PACK), F32),
                 jax.ShapeDtypeStruct((bsz, 1, 3 * C_RWKV + LORA_PAD), F32))
    out_specs = (pl.BlockSpec((1, L, C_RWKV), lambda b, c: (b, c, 0)),
                 pl.BlockSpec((1, N_GROUPS, PACK, PACK), lambda b, c: (b, 0, 0, 0)),
                 pl.BlockSpec((1, 1, 3 * C_RWKV + LORA_PAD), lambda b, c: (b, 0, 0)))
    return pl.pallas_call(
        _rwkv_kernel,
        out_shape=out_shape,
        grid=(bsz, t_len // L),
        in_specs=in_specs,
        out_specs=out_specs,
        scratch_shapes=[pltpu.VMEM((N_GROUPS, PACK, PACK), F32),
                        pltpu.VMEM((1, C_RWKV), F32), pltpu.VMEM((1, C_RWKV), F32),
                        pltpu.VMEM((1, C_RWKV), F32), pltpu.VMEM((1, LORA_PAD), F32)],
        compiler_params=pltpu.CompilerParams(dimension_semantics=("arbitrary", "arbitrary")),
        name="rwkv7_mix",
    )(proj3, proj3, proj3, proj3, *shift_parts, st0, *params)


def _outproj_kernel(c_ref, y_ref, x_ref, wa_ref, wb_ref, g_ref, b_ref, rwh_ref, rwl_ref, rb_ref,
                    x1_ref, idx_ref, gate_ref):
    mix = (jnp.dot(c_ref[...], wa_ref[...], preferred_element_type=F32)
           + jnp.dot(y_ref[...], wb_ref[...], preferred_element_type=F32))
    h = ALPHA * x_ref[...] + mix
    mu = jnp.mean(h, axis=-1, keepdims=True)
    hc = h - mu
    var = jnp.mean(hc * hc, axis=-1, keepdims=True)
    x1 = hc * lax.rsqrt(var + LN_EPS) * g_ref[...] + b_ref[...]
    x1_ref[...] = x1
    logits = _dot_split_w(x1, rwh_ref[...], rwl_ref[...]) + rb_ref[...]
    lane = lax.broadcasted_iota(jnp.int32, logits.shape, 1)
    idx_out = jnp.zeros(logits.shape, jnp.int32)
    val_out = jnp.zeros(logits.shape, F32)
    vals = []
    for kk in range(TOP_K):
        m = jnp.max(logits, axis=-1, keepdims=True)
        sel = jnp.min(jnp.where(logits == m, lane, ROUTER_PAD), axis=-1, keepdims=True)
        vals.append(m)
        idx_out = jnp.where(lane == kk, sel, idx_out)
        logits = jnp.where(lane == sel, -jnp.inf, logits)
    exps = [jnp.exp(vv - vals[0]) for vv in vals]
    denom = exps[0]
    for ee in exps[1:]:
        denom = denom + ee
    for kk in range(TOP_K):
        val_out = jnp.where(lane == kk, exps[kk] / denom, val_out)
    idx_ref[...] = idx_out
    gate_ref[...] = val_out


def _out_proj(c2, y2, x2, wa, wb, ln_g, ln_b, rw_hi, rw_lo, rb, tm):
    n = x2.shape[0]
    row = lambda w: pl.BlockSpec((1, w), lambda i: (0, 0))
    return pl.pallas_call(
        _outproj_kernel,
        out_shape=(jax.ShapeDtypeStruct((n, D_MODEL), F32),
                   jax.ShapeDtypeStruct((n, ROUTER_PAD), jnp.int32),
                   jax.ShapeDtypeStruct((n, ROUTER_PAD), F32)),
        grid=(n // tm,),
        in_specs=[pl.BlockSpec((tm, C_CONV), lambda i: (i, 0)),
                  pl.BlockSpec((tm, C_RWKV), lambda i: (i, 0)),
                  pl.BlockSpec((tm, D_MODEL), lambda i: (i, 0)),
                  pl.BlockSpec((C_CONV, D_MODEL), lambda i: (0, 0)),
                  pl.BlockSpec((C_RWKV, D_MODEL), lambda i: (0, 0)),
                  row(D_MODEL), row(D_MODEL),
                  pl.BlockSpec((D_MODEL, ROUTER_PAD), lambda i: (0, 0)),
                  pl.BlockSpec((D_MODEL, ROUTER_PAD), lambda i: (0, 0)),
                  row(ROUTER_PAD)],
        out_specs=(pl.BlockSpec((tm, D_MODEL), lambda i: (i, 0)),
                   pl.BlockSpec((tm, ROUTER_PAD), lambda i: (i, 0)),
                   pl.BlockSpec((tm, ROUTER_PAD), lambda i: (i, 0))),
        compiler_params=pltpu.CompilerParams(dimension_semantics=("arbitrary",)),
        name="out_proj_ln_router",
    )(c2, y2, x2, wa, wb, ln_g, ln_b, rw_hi, rw_lo, rb)


def _gather_rows_kernel(nused_ref, tok_ref, xa_hbm, xs_hbm, o_ref, buf_ref, sem):
    rows = o_ref.shape[0]
    n_a = xa_hbm.shape[0]
    used = pl.program_id(0) < nused_ref[0]

    def row_copy(src_hbm, t, r):
        return pltpu.make_async_copy(src_hbm.at[pl.ds(t, 1)], buf_ref.at[pl.ds(r, 1)], sem)

    @pl.when(used)
    def _():
        def issue(r, carry):
            t = tok_ref[0, 0, r]

            @pl.when(t < n_a)
            def _():
                row_copy(xa_hbm, t, r).start()

            @pl.when(t >= n_a)
            def _():
                row_copy(xs_hbm, t - n_a, r).start()

            return carry

        lax.fori_loop(0, rows, issue, 0, unroll=8)

        def drain(r, carry):
            row_copy(xa_hbm, 0, r).wait()
            return carry

        lax.fori_loop(0, rows, drain, 0, unroll=8)
        o_ref[...] = buf_ref[...].astype(o_ref.dtype)

    @pl.when(jnp.logical_not(used))
    def _():
        o_ref[...] = jnp.zeros_like(o_ref)


def _gather_rows(n_used, row_tok3, xa, xs):
    nb, _, rows = row_tok3.shape
    width = xa.shape[1]
    grid_spec = pltpu.PrefetchScalarGridSpec(
        num_scalar_prefetch=1,
        grid=(nb,),
        in_specs=[pl.BlockSpec((1, 1, rows), lambda i, nu: (i, 0, 0), memory_space=pltpu.SMEM),
                  pl.BlockSpec(memory_space=pl.ANY),
                  pl.BlockSpec(memory_space=pl.ANY)],
        out_specs=pl.BlockSpec((rows, width), lambda i, nu: (i, 0)),
        scratch_shapes=[pltpu.VMEM((rows, width), xa.dtype), pltpu.SemaphoreType.DMA],
    )
    return pl.pallas_call(
        _gather_rows_kernel,
        out_shape=jax.ShapeDtypeStruct((nb * rows, width), BF16),
        grid_spec=grid_spec,
        compiler_params=pltpu.CompilerParams(dimension_semantics=("arbitrary",)),
        name="moe_gather_rows",
    )(n_used, row_tok3, xa, xs)


def _expert_kernel(bexp_ref, first_ref, nused_ref, x_ref, wg_ref, bg_ref, wu_ref, bu_ref, wd_ref, bd_ref,
                   o_ref, wgc_ref, wuc_ref, wdc_ref):
    i = pl.program_id(0)
    j = pl.program_id(1)
    nj = pl.num_programs(1)

    @pl.when(i < nused_ref[0])
    def _():
        @pl.when(first_ref[i] == 1)
        def _():
            wgc_ref[j] = wg_ref[0].astype(BF16)
            wuc_ref[j] = wu_ref[0].astype(BF16)
            wdc_ref[j] = wd_ref[0].astype(BF16)

        @pl.when(j == 0)
        def _():
            o_ref[...] = jnp.broadcast_to(bd_ref[0], o_ref.shape)

        x = x_ref[...]
        gate = jnp.minimum(jnp.dot(x, wgc_ref[j], preferred_element_type=F32) + bg_ref[0, pl.ds(j, 1), :],
                           SWIGLU_LIMIT)
        up = jnp.clip(jnp.dot(x, wuc_ref[j], preferred_element_type=F32) + bu_ref[0, pl.ds(j, 1), :],
                      -SWIGLU_LIMIT, SWIGLU_LIMIT)
        hmid = (up + 1.0) * gate * jax.nn.sigmoid(SWIGLU_ALPHA * gate)
        o_ref[...] += jnp.dot(hmid.astype(BF16), wdc_ref[j], preferred_element_type=F32)

    @pl.when((i >= nused_ref[0]) & (j == nj - 1))
    def _():
        o_ref[...] = jnp.zeros_like(o_ref)


def _experts(block_exp, first, n_used, xb, wg, bg, wu, bu, wd, bd):
    n_rows = xb.shape[0]
    nb = n_rows // MOE_TM
    nj = D_FF // MOE_TF

    def wsel(i, j, fi):
        return jnp.where(fi[i] == 1, j, nj - 1)

    grid_spec = pltpu.PrefetchScalarGridSpec(
        num_scalar_prefetch=3,
        grid=(nb, nj),
        in_specs=[pl.BlockSpec((MOE_TM, D_MODEL), lambda i, j, be, fi, nu: (i, 0)),
                  pl.BlockSpec((1, D_MODEL, MOE_TF), lambda i, j, be, fi, nu: (be[i], 0, wsel(i, j, fi))),
                  pl.BlockSpec((1, nj, MOE_TF), lambda i, j, be, fi, nu: (be[i], 0, 0)),
                  pl.BlockSpec((1, D_MODEL, MOE_TF), lambda i, j, be, fi, nu: (be[i], 0, wsel(i, j, fi))),
                  pl.BlockSpec((1, nj, MOE_TF), lambda i, j, be, fi, nu: (be[i], 0, 0)),
                  pl.BlockSpec((1, MOE_TF, D_MODEL), lambda i, j, be, fi, nu: (be[i], wsel(i, j, fi), 0)),
                  pl.BlockSpec((1, 1, D_MODEL), lambda i, j, be, fi, nu: (be[i], 0, 0))],
        out_specs=pl.BlockSpec((MOE_TM, D_MODEL), lambda i, j, be, fi, nu: (i, 0)),
        scratch_shapes=[pltpu.VMEM((nj, D_MODEL, MOE_TF), BF16), pltpu.VMEM((nj, D_MODEL, MOE_TF), BF16),
                        pltpu.VMEM((nj, MOE_TF, D_MODEL), BF16)],
    )
    return pl.pallas_call(
        _expert_kernel,
        out_shape=jax.ShapeDtypeStruct((n_rows, D_MODEL), F32),
        grid_spec=grid_spec,
        compiler_params=pltpu.CompilerParams(dimension_semantics=("arbitrary", "arbitrary"),
                                             vmem_limit_bytes=MOE_VMEM_LIMIT),
        name="moe_experts",
    )(block_exp, first, n_used, xb, wg, bg.reshape(N_EXPERTS, nj, MOE_TF), wu, bu.reshape(N_EXPERTS, nj, MOE_TF),
      wd, bd.reshape(N_EXPERTS, 1, D_MODEL))


def _combine_kernel(dest_ref, gate_ref, x1_ref, g_ref, b_ref, yb_hbm, o_ref, buf_ref, sem):
    tc = x1_ref.shape[0]

    def issue(r, carry):
        for kk in range(TOP_K):
            d = dest_ref[0, 0, r * TOP_K + kk]
            pltpu.make_async_copy(yb_hbm.at[pl.ds(d, 1)], buf_ref.at[kk, pl.ds(r, 1)], sem).start()
        return carry

    lax.fori_loop(0, tc, issue, 0, unroll=2)

    def drain(r, carry):
        for kk in range(TOP_K):
            pltpu.make_async_copy(yb_hbm.at[pl.ds(0, 1)], buf_ref.at[kk, pl.ds(r, 1)], sem).wait()
        return carry

    lax.fori_loop(0, tc, drain, 0, unroll=2)

    gates = gate_ref[...]
    moe = gates[:, 0:1] * buf_ref[0]
    for kk in range(1, TOP_K):
        moe = moe + gates[:, kk:kk + 1] * buf_ref[kk]
    h = ALPHA * x1_ref[...] + moe
    mu = jnp.mean(h, axis=-1, keepdims=True)
    hc = h - mu
    var = jnp.mean(hc * hc, axis=-1, keepdims=True)
    o_ref[...] = hc * lax.rsqrt(var + LN_EPS) * g_ref[...] + b_ref[...]


def _combine(dest3, gates, x1, ln_g, ln_b, yb, tc):
    n = dest3.shape[0] * tc
    return pl.pallas_call(
        _combine_kernel,
        out_shape=jax.ShapeDtypeStruct((n, D_MODEL), F32),
        grid=(n // tc,),
        in_specs=[pl.BlockSpec((1, 1, tc * TOP_K), lambda i: (i, 0, 0), memory_space=pltpu.SMEM),
                  pl.BlockSpec((tc, ROUTER_PAD), lambda i: (i, 0)),
                  pl.BlockSpec((tc, D_MODEL), lambda i: (i, 0)),
                  pl.BlockSpec((1, D_MODEL), lambda i: (0, 0)),
                  pl.BlockSpec((1, D_MODEL), lambda i: (0, 0)),
                  pl.BlockSpec(memory_space=pl.ANY)],
        out_specs=pl.BlockSpec((tc, D_MODEL), lambda i: (i, 0)),
        scratch_shapes=[pltpu.VMEM((TOP_K, tc, D_MODEL), F32), pltpu.SemaphoreType.DMA],
        compiler_params=pltpu.CompilerParams(dimension_semantics=("arbitrary",)),
        name="moe_combine_ln2",
    )(dest3, gates, x1, ln_g, ln_b, yb)


def _pad_cols(w, width):
    return jnp.pad(w, ((0, 0), (0, width - w.shape[1])))


def _pad_rows(w, height):
    return jnp.pad(w, ((0, height - w.shape[0]), (0, 0)))


def _split_lora_cols(w):
    xw = w[..., 0:R_DECAY]
    xa = w[..., R_DECAY:R_DECAY + R_ICLR]
    xg = w[..., R_DECAY + R_ICLR:]
    pad = lambda x, n: jnp.pad(x, [(0, 0)] * (x.ndim - 1) + [(0, n - x.shape[-1])])
    return jnp.concatenate([pad(xw, LANES), pad(xa, LANES), pad(xg, 2 * LANES)], axis=-1)


def _pick(n, prefs):
    for p in prefs:
        if n % p == 0:
            return p
    return n


def _mixer_group(x, conv_buf, shift_buf, wkv_state, wts):
    bsz, t_len, _ = x.shape
    n = bsz * t_len
    proj = _in_proj(x.reshape(n, D_MODEL), wts["w_in"], wts["b_in"], _pick(n, (1024, 512, 256, 128)), 512)
    proj3 = proj.reshape(bsz, t_len, P_PAD)

    hist = jnp.pad(conv_buf, ((0, 0), (HIST - (CONV_WIDTH - 1), 0), (0, 0)))
    c, tail = _conv_module(proj3, hist, wts["conv_w"], wts["conv_b"], wts["conv_ln_g"], wts["conv_ln_b"],
                           _pick(t_len, (128, 64, 32, 16, 8)))
    new_conv = tail[:, HIST - (CONV_WIDTH - 1):, :]

    sh_rkv = shift_buf[:, :, :3 * C_RWKV]
    sh_lo = _split_lora_cols(shift_buf[:, :, 3 * C_RWKV:])
    shift_parts = (sh_rkv[:, :, 0:C_RWKV], sh_rkv[:, :, C_RWKV:2 * C_RWKV], sh_rkv[:, :, 2 * C_RWKV:], sh_lo)
    st_t = jnp.swapaxes(wkv_state, -1, -2).reshape(bsz, N_GROUPS, GROUP_HEADS, HEAD, HEAD)
    eye_h = jnp.eye(GROUP_HEADS, dtype=F32)
    st0 = jnp.einsum("bghkv,hj->bghkjv", st_t, eye_h).reshape(bsz, N_GROUPS, PACK, PACK)
    yb, st_out, sh_out = _rwkv_mix(proj3, shift_parts, st0, wts["rwkv_params"], _pick(t_len, (64, 32, 16)))
    st5 = st_out.reshape(bsz, N_GROUPS, GROUP_HEADS, HEAD, GROUP_HEADS, HEAD)
    st_diag = jnp.einsum("bghkhv->bghkv", st5)
    new_wkv = jnp.swapaxes(st_diag, -1, -2).reshape(bsz, N_HEADS, HEAD, HEAD)
    lo = sh_out[:, :, 3 * C_RWKV:]
    new_shift = jnp.concatenate([sh_out[:, :, :3 * C_RWKV], lo[:, :, 0:R_DECAY], lo[:, :, LANES:LANES + R_ICLR],
                                 lo[:, :, 2 * LANES:2 * LANES + R_GATE]], axis=-1)
    return c.reshape(n, C_CONV), yb.reshape(n, C_RWKV), new_conv, new_shift, new_wkv


def _route(top_idx, n_tok):
    n_assign = n_tok * TOP_K
    flat_e = top_idx.reshape(-1)
    onehot = (flat_e[:, None] == jnp.arange(N_EXPERTS, dtype=jnp.int32)[None, :]).astype(jnp.int32)
    csum = jnp.cumsum(onehot, axis=0)
    rank = jnp.take_along_axis(csum, flat_e[:, None], axis=1)[:, 0] - 1
    counts = csum[-1]
    padded = (counts + MOE_TM - 1) // MOE_TM * MOE_TM
    seg_end = jnp.cumsum(padded)
    seg_start = seg_end - padded
    dest = (seg_start[flat_e] + rank).astype(jnp.int32)
    n_rows = (n_assign + N_EXPERTS * (MOE_TM - 1) + MOE_TM - 1) // MOE_TM * MOE_TM
    n_blocks = n_rows // MOE_TM
    row_tok = jnp.zeros((n_rows,), jnp.int32).at[dest].set(jnp.arange(n_assign, dtype=jnp.int32) // TOP_K)
    block_start = jnp.arange(n_blocks, dtype=jnp.int32) * MOE_TM
    block_exp = jnp.minimum(jnp.sum((seg_end[None, :] <= block_start[:, None]).astype(jnp.int32), axis=1),
                            N_EXPERTS - 1).astype(jnp.int32)
    first = ((block_start == seg_start[block_exp]) & (block_start < seg_end[-1])).astype(jnp.int32)
    n_used = (seg_end[-1] // MOE_TM).astype(jnp.int32).reshape(1)
    return dest, row_tok, block_exp, first, n_used, n_blocks


def kernel(x_prompt, x_sample, state_conv, state_shift, state_wkv, w_in, b_in, mu_shift, conv_w, conv_b,
           conv_ln_g, conv_ln_b, rwkv_w0, rwkv_w2, rwkv_a0, rwkv_a2, rwkv_g2, rwkv_k_k, rwkv_k_a, rwkv_r_k,
           rwkv_ln_g, rwkv_ln_b, w_out, ln1_g, ln1_b, router_w, router_b, w_gate, b_gate, w_up, b_up,
           w_down, b_down, ln2_g, ln2_b):
    assert w_in.shape[0] == 1, "single layer"
    d = 0
    row = lambda v: v.reshape(1, -1)
    n_p, t_p, _ = x_prompt.shape
    n_s, t_s, _ = x_sample.shape

    w_rkv = w_in[d][:, 2 * C_CONV:2 * C_CONV + 3 * C_RWKV]
    w_lo = _split_lora_cols(w_in[d][:, 2 * C_CONV + 3 * C_RWKV:])
    w_in_p = jnp.concatenate([w_in[d][:, :2 * C_CONV], w_rkv, w_lo], axis=1).astype(BF16)
    b_in_p = jnp.concatenate([b_in[d][None, :2 * C_CONV], b_in[d][None, 2 * C_CONV:2 * C_CONV + 3 * C_RWKV],
                              _split_lora_cols(b_in[d][None, 2 * C_CONV + 3 * C_RWKV:])], axis=1)
    mu = mu_shift[d][None, :]
    mu_lo = _split_lora_cols(mu[:, 3 * C_RWKV:])
    head_of_lane = jnp.arange(C_RWKV, dtype=jnp.int32) // HEAD
    e_mat = (head_of_lane[:, None] == jnp.arange(LANES, dtype=jnp.int32)[None, :]).astype(BF16)

    def hi_lo(w, height):
        w = _pad_rows(w, height)
        w_hi = w.astype(BF16)
        return w_hi, (w - w_hi.astype(F32)).astype(BF16)

    rwkv_params = (
        mu[:, 0:C_RWKV], mu[:, C_RWKV:2 * C_RWKV], mu[:, 2 * C_RWKV:3 * C_RWKV], mu_lo,
        row(rwkv_w0[d]), *hi_lo(rwkv_w2[d], LANES), row(rwkv_a0[d]), *hi_lo(rwkv_a2[d], LANES),
        *hi_lo(rwkv_g2[d], 2 * LANES),
        row(rwkv_k_k[d]), row(rwkv_k_a[d]), row(rwkv_r_k[d]), row(rwkv_ln_g[d]), row(rwkv_ln_b[d]),
        e_mat, e_mat.T,
    )
    wts = dict(w_in=w_in_p, b_in=b_in_p, conv_w=conv_w[d], conv_b=row(conv_b[d]),
               conv_ln_g=row(conv_ln_g[d]), conv_ln_b=row(conv_ln_b[d]), rwkv_params=rwkv_params)

    zero_conv = jnp.zeros((n_p, CONV_WIDTH - 1, C_CONV), x_prompt.dtype)
    zero_shift = jnp.zeros((n_p, 1, N_SHIFT), x_prompt.dtype)
    zero_wkv = jnp.zeros((n_p, N_HEADS, HEAD, HEAD), state_wkv.dtype)
    c_p, y_p, conv_p, shift_p, wkv_p = _mixer_group(x_prompt, zero_conv, zero_shift, zero_wkv, wts)
    c_s, y_s, conv_s, shift_s, wkv_s = _mixer_group(x_sample, state_conv[d], state_shift[d], state_wkv[d], wts)

    w_out_b = w_out[d].astype(BF16)
    rw = _pad_cols(router_w[d], ROUTER_PAD)
    rw_hi = rw.astype(BF16)
    rw_lo = (rw - rw_hi.astype(F32)).astype(BF16)
    rb = jnp.concatenate([router_b[d], jnp.full((ROUTER_PAD - N_EXPERTS,), -jnp.inf, F32)])[None, :]

    def out_proj(c2, y2, x3):
        n = c2.shape[0]
        return _out_proj(c2, y2, x3.reshape(n, D_MODEL), w_out_b[:C_CONV], w_out_b[C_CONV:], row(ln1_g[d]),
                         row(ln1_b[d]), rw_hi, rw_lo, rb, _pick(n, (OUT_TM, 256, 128)))

    x1_p, idx_p, gate_p = out_proj(c_p, y_p, x_prompt)
    x1_s, idx_s, gate_s = out_proj(c_s, y_s, x_sample)
    n_tok_p = n_p * t_p
    n_tok = n_tok_p + n_s * t_s
    top_idx = jnp.concatenate([idx_p[:, :TOP_K], idx_s[:, :TOP_K]], axis=0)
    dest, row_tok, block_exp, first, n_used, n_blocks = _route(top_idx, n_tok)
    xb = _gather_rows(n_used, row_tok.reshape(n_blocks, 1, MOE_TM), x1_p, x1_s)
    yb = _experts(block_exp, first, n_used, xb, w_gate[d], b_gate[d], w_up[d], b_up[d], w_down[d], b_down[d])

    def combine(dest_g, gate_g, x1_g):
        n = x1_g.shape[0]
        tc = _pick(n, (128, 64, 32, 16, 8))
        return _combine(dest_g.reshape(n // tc, 1, tc * TOP_K), gate_g, x1_g, row(ln2_g[d]), row(ln2_b[d]), yb, tc)

    y_prompt = combine(dest[:n_tok_p * TOP_K], gate_p, x1_p).reshape(n_p, t_p, D_MODEL)
    y_sample = combine(dest[n_tok_p * TOP_K:], gate_s, x1_s).reshape(n_s, t_s, D_MODEL)
    return (y_prompt, y_sample, conv_p[None], shift_p[None], wkv_p[None], conv_s[None], shift_s[None], wkv_s[None])
```

```python
import functools
import math

import jax
import jax.numpy as jnp
from jax import lax
from jax.experimental import pallas as pl
from jax.experimental.pallas import tpu as pltpu

F32 = jnp.float32
BF16 = jnp.bfloat16

D_MODEL = 2048
C_CONV = 1024
C_RWKV = 1024
HEAD = 64
N_HEADS = C_RWKV // HEAD
CONV_WIDTH = 31
R_DECAY = 64
R_ICLR = 64
R_GATE = 160
N_SHIFT = 3 * C_RWKV + R_DECAY + R_ICLR + R_GATE
N_EXPERTS = 32
TOP_K = 4
D_FF = 2048
SWIGLU_LIMIT = 7.0
SWIGLU_ALPHA = 1.702
LN_EPS = 1e-5
GN_EPS = 64e-5
ALPHA = 2.0 ** 0.25

LANES = 128
SUBLANES = 8

HIST = 32
LORA_PAD = 512
P_PAD = 2 * C_CONV + 3 * C_RWKV + LORA_PAD
GROUP_HEADS = 2
PACK = GROUP_HEADS * HEAD
N_GROUPS = N_HEADS // GROUP_HEADS
MOE_TM = 512
MOE_TF = 512
OUT_TM = 512
MOE_VMEM_LIMIT = 60 * 1024 * 1024
ROUTER_PAD = LANES


def _dot(a, b, prec=1, dims=(((1,), (0,)), ((), ()))):
    if prec == 6:
        return lax.dot_general(a.astype(F32), b.astype(F32), dims, precision=lax.Precision.HIGHEST,
                               preferred_element_type=F32)
    d = lambda x, y: lax.dot_general(x, y, dims, preferred_element_type=F32)
    if prec == 1:
        return d(a.astype(BF16), b.astype(BF16))
    a_hi = a.astype(BF16)
    a_lo = (a - a_hi.astype(F32)).astype(BF16)
    b_hi = b.astype(BF16)
    b_lo = (b - b_hi.astype(F32)).astype(BF16)
    return d(a_hi, b_hi) + d(a_hi, b_lo) + d(a_lo, b_hi)


_NT = (((1,), (1,)), ((), ()))
_BNN = (((2,), (1,)), ((0,), (0,)))
_BNT = (((2,), (2,)), ((0,), (0,)))


def _split3(x):
    p1 = x.astype(BF16)
    r1 = x - p1.astype(F32)
    p2 = r1.astype(BF16)
    p3 = (r1 - p2.astype(F32)).astype(BF16)
    return p1, p2, p3


def _dot_exact_rhs(x, m_bf16):
    d = lambda a: jnp.dot(a, m_bf16, preferred_element_type=F32)
    p1, p2, p3 = _split3(x)
    return d(p1) + d(p2) + d(p3)


def _dot_exact_lhs(m_bf16, x):
    d = lambda a: jnp.dot(m_bf16, a, preferred_element_type=F32)
    p1, p2, p3 = _split3(x)
    return d(p1) + d(p2) + d(p3)


def _dot_split_w(x, w_hi, w_lo):
    x_hi = x.astype(BF16)
    x_lo = (x - x_hi.astype(F32)).astype(BF16)
    d = lambda a, b: jnp.dot(a, b, preferred_element_type=F32)
    return d(x_hi, w_hi) + d(x_hi, w_lo) + d(x_lo, w_hi)


def _mm_bias_kernel(x_ref, w_ref, b_ref, o_ref, xb_ref):
    @pl.when(pl.program_id(1) == 0)
    def _():
        xb_ref[...] = x_ref[...].astype(BF16)

    o_ref[...] = jnp.dot(xb_ref[...], w_ref[...], preferred_element_type=F32) + b_ref[...]


def _in_proj(x, w_bf16, b, tm, tn):
    n, k = x.shape
    p = w_bf16.shape[1]
    return pl.pallas_call(
        _mm_bias_kernel,
        out_shape=jax.ShapeDtypeStruct((n, p), F32),
        grid=(n // tm, p // tn),
        in_specs=[pl.BlockSpec((tm, k), lambda i, j: (i, 0)),
                  pl.BlockSpec((k, tn), lambda i, j: (0, j)),
                  pl.BlockSpec((1, tn), lambda i, j: (0, j))],
        out_specs=pl.BlockSpec((tm, tn), lambda i, j: (i, j)),
        scratch_shapes=[pltpu.VMEM((tm, k), BF16)],
        compiler_params=pltpu.CompilerParams(dimension_semantics=("arbitrary", "arbitrary")),
        name="in_proj",
    )(x, w_bf16, b)


def _conv_kernel(val_ref, gate_ref, hist_ref, w_ref, cb_ref, g_ref, b_ref, c_ref, tail_ref, ext_ref, sh_ref):
    t = pl.program_id(1)
    tt = val_ref.shape[1]

    @pl.when(t == 0)
    def _():
        ext_ref[0:HIST, :] = hist_ref[0]

    u = val_ref[0] * jax.nn.sigmoid(gate_ref[0])
    ext_ref[HIST:HIST + tt, :] = u
    span = tt + HIST - SUBLANES
    for s in range(1, SUBLANES):
        sh_ref[s, 0:span, :] = ext_ref[s:s + span, :]
    off = HIST - (CONV_WIDTH - 1)
    acc = jnp.broadcast_to(cb_ref[...], (tt, C_CONV))
    for j in range(CONV_WIDTH):
        base = (off + j) // SUBLANES * SUBLANES
        s = (off + j) % SUBLANES
        src = ext_ref[base:base + tt, :] if s == 0 else sh_ref[s, base:base + tt, :]
        acc = acc + w_ref[j:j + 1, :] * src
    mu = jnp.mean(acc, axis=-1, keepdims=True)
    xc = acc - mu
    var = jnp.mean(xc * xc, axis=-1, keepdims=True)
    y = xc * lax.rsqrt(var + LN_EPS) * g_ref[...] + b_ref[...]
    c_ref[0] = (y * jax.nn.sigmoid(y)).astype(c_ref.dtype)
    tail = ext_ref[tt:tt + HIST, :]
    ext_ref[0:HIST, :] = tail
    tail_ref[0] = tail


def _conv_module(proj3, hist, conv_w, conv_b, ln_g, ln_b, tt):
    bsz, t_len, _ = proj3.shape
    nblk = C_CONV // C_CONV
    return pl.pallas_call(
        _conv_kernel,
        out_shape=(jax.ShapeDtypeStruct((bsz, t_len, C_CONV), BF16),
                   jax.ShapeDtypeStruct((bsz, HIST, C_CONV), F32)),
        grid=(bsz, t_len // tt),
        in_specs=[pl.BlockSpec((1, tt, C_CONV), lambda b, t: (b, t, 0)),
                  pl.BlockSpec((1, tt, C_CONV), lambda b, t: (b, t, nblk)),
                  pl.BlockSpec((1, HIST, C_CONV), lambda b, t: (b, 0, 0)),
                  pl.BlockSpec((CONV_WIDTH, C_CONV), lambda b, t: (0, 0)),
                  pl.BlockSpec((1, C_CONV), lambda b, t: (0, 0)),
                  pl.BlockSpec((1, C_CONV), lambda b, t: (0, 0)),
                  pl.BlockSpec((1, C_CONV), lambda b, t: (0, 0))],
        out_specs=(pl.BlockSpec((1, tt, C_CONV), lambda b, t: (b, t, 0)),
                   pl.BlockSpec((1, HIST, C_CONV), lambda b, t: (b, 0, 0))),
        scratch_shapes=[pltpu.VMEM((HIST + tt, C_CONV), F32),
                        pltpu.VMEM((SUBLANES, HIST + tt, C_CONV), F32)],
        compiler_params=pltpu.CompilerParams(dimension_semantics=("arbitrary", "arbitrary")),
        name="conv_module",
    )(proj3, proj3, hist, conv_w, conv_b, ln_g, ln_b)


PREC_CHUNK = 1
PREC_STATE = 1


def _seg_sum(x, e_ref, et_ref):
    return _dot_exact_rhs(_dot_exact_rhs(x, e_ref[...]), et_ref[...])


def _rwkv_kernel(r_ref, k_ref, v_ref, lo_ref, shr_ref, shk_ref, shv_ref, shlo_ref, st0_ref,
                 mur_ref, muk_ref, muv_ref, mulo_ref, w0_ref, w2h_ref, w2l_ref, a0_ref, a2h_ref, a2l_ref,
                 g2h_ref, g2l_ref,
                 kkw_ref, kaw_ref, rkw_ref, lng_ref, lnb_ref, e_ref, et_ref,
                 y_ref, stout_ref, shout_ref,
                 st_sc, pr_sc, pk_sc, pv_sc, plo_sc):
    c = pl.program_id(1)
    n_chunks = pl.num_programs(1)
    L = r_ref.shape[1]
    GL = GROUP_HEADS * L
    log2l = int(math.log2(L))

    @pl.when(c == 0)
    def _():
        st_sc[...] = st0_ref[0]
        pr_sc[...] = shr_ref[0]
        pk_sc[...] = shk_ref[0]
        pv_sc[...] = shv_ref[0]
        plo_sc[...] = shlo_ref[0]

    def token_shift(x_ref, prev_sc, mu_ref):
        x = x_ref[0]
        row = lax.broadcasted_iota(jnp.int32, x.shape, 0)
        xprev = jnp.where(row == 0, jnp.broadcast_to(prev_sc[...], x.shape), pltpu.roll(x, 1, 0))
        prev_sc[...] = x[L - 1:L, :]
        return x + mu_ref[...] * (xprev - x)

    r = token_shift(r_ref, pr_sc, mur_ref)
    k = token_shift(k_ref, pk_sc, muk_ref)
    v = token_shift(v_ref, pv_sc, muv_ref)
    lo = token_shift(lo_ref, plo_sc, mulo_ref)
    xw = lo[:, 0:LANES]
    xa = lo[:, LANES:2 * LANES]
    xg = lo[:, 2 * LANES:LORA_PAD]

    u_dec = w0_ref[...] + _dot_split_w(jnp.tanh(xw), w2h_ref[...], w2l_ref[...])
    logw = (-math.exp(-0.5)) * jax.nn.sigmoid(u_dec)
    a = jax.nn.sigmoid(a0_ref[...] + _dot_split_w(xa, a2h_ref[...], a2l_ref[...]))
    g = _dot_split_w(jax.nn.sigmoid(xg), g2h_ref[...], g2l_ref[...])

    kk = k * kkw_ref[...]
    nrm = jnp.sqrt(_seg_sum(kk * kk, e_ref, et_ref))
    kappa = kk / jnp.maximum(nrm, 1e-12)
    k2 = k * (1.0 + (a - 1.0) * kaw_ref[...])
    bvec = kappa * a
    bonus = _seg_sum(r * k2 * rkw_ref[...], e_ref, et_ref) * v

    ti = lax.broadcasted_iota(jnp.int32, (L, L), 0)
    tj = lax.broadcasted_iota(jnp.int32, (L, L), 1)
    tril = jnp.where(tj <= ti, 1.0, 0.0).astype(BF16)
    cum = _dot_exact_lhs(tril, logw)
    cum_l = cum[L - 1:L, :]
    gam = jnp.exp(cum)
    ginv = jnp.exp(-cum)
    gprev = jnp.exp(cum - logw)
    gtail = jnp.exp(cum_l - cum)
    gam_l = jnp.exp(cum_l)

    kt = kappa * gprev
    kinv = k2 * ginv
    binv = bvec * ginv
    rt = r * gam
    khat = k2 * gtail
    bhat = bvec * gtail

    rr = lax.broadcasted_iota(jnp.int32, (GL, GL), 0)
    cc = lax.broadcasted_iota(jnp.int32, (GL, GL), 1)
    same = (rr >> log2l) == (cc >> log2l)
    tpos = rr & (L - 1)
    jpos = cc & (L - 1)
    mask_s = same & (jpos < tpos)
    mask_i = same & (jpos <= tpos)
    eye = rr == cc
    srow = lax.broadcasted_iota(jnp.int32, (GL, PACK), 0)
    slane = lax.broadcasted_iota(jnp.int32, (GL, PACK), 1)
    bmask = (srow >> log2l) == (slane >> int(math.log2(HEAD)))
    drow = lax.broadcasted_iota(jnp.int32, (PACK, PACK), 0)
    dcol = lax.broadcasted_iota(jnp.int32, (PACK, PACK), 1)
    deye = drow == dcol

    def stack(x):
        x3 = jnp.stack([x[:, gi * PACK:(gi + 1) * PACK] for gi in range(N_GROUPS)], axis=0)
        return jnp.where(bmask[None], jnp.concatenate([x3] * GROUP_HEADS, axis=1), 0.0)

    def bdot(a, b, dims=_BNN):
        return lax.dot_general(a.astype(BF16), b.astype(BF16), dims, preferred_element_type=F32)

    def btrans(x):
        return jnp.stack([x[gi].T for gi in range(N_GROUPS)], axis=0)

    kt_s = stack(kt)
    rt_s = stack(rt)
    binv_s = stack(binv)
    kinv_s = stack(kinv)
    v_s = stack(v)
    khat_s = stack(khat)
    bhat_s = stack(bhat)

    lane_split = GL % LANES == 0
    if lane_split:
        a_all = bdot(jnp.concatenate([kt_s, rt_s], axis=1), jnp.concatenate([binv_s, kinv_s], axis=1), _BNT)
        a_parts = (a_all[:, :GL, :GL], a_all[:, :GL, GL:], a_all[:, GL:, :GL], a_all[:, GL:, GL:])
    else:
        a_parts = (bdot(kt_s, binv_s, _BNT), bdot(kt_s, kinv_s, _BNT),
                   bdot(rt_s, binv_s, _BNT), bdot(rt_s, kinv_s, _BNT))
    n_mat = jnp.where(mask_s[None], a_parts[0], 0.0)
    a_kk = jnp.where(mask_s[None], a_parts[1], 0.0)
    a_br = jnp.where(mask_i[None], a_parts[2], 0.0)
    a_kr = jnp.where(mask_i[None], a_parts[3], 0.0)

    p_mat = -n_mat
    t_mat = jnp.where(eye[None], 1.0, 0.0) + p_mat
    if log2l > 1:
        p_mat = bdot(p_mat, p_mat)
    for lvl in range(1, log2l):
        if lvl == log2l - 1:
            t_mat = t_mat + bdot(p_mat, t_mat)
        elif lane_split:
            both = bdot(p_mat, jnp.concatenate([p_mat, t_mat], axis=2))
            t_mat = t_mat + both[:, :, GL:]
            p_mat = both[:, :, :GL]
        else:
            t_mat = t_mat + bdot(p_mat, t_mat)
            p_mat = bdot(p_mat, p_mat)

    av = bdot(jnp.concatenate([a_kk, a_kr], axis=1), v_s)
    wu = bdot(t_mat, jnp.concatenate([kt_s, av[:, :GL]], axis=2))
    w_s = wu[:, :, :PACK]
    uv_s = wu[:, :, PACK:]
    br = bdot(a_br, wu)
    q_s = rt_s - br[:, :, :PACK]
    y0_s = av[:, GL:] - br[:, :, PACK:]
    bhat_t = btrans(bhat_s)
    khat_t = btrans(khat_s)
    gl3 = jnp.stack([gam_l[:, gi * PACK:(gi + 1) * PACK] for gi in range(N_GROUPS)], axis=0)
    bw = bdot(bhat_t, wu)
    m_mat = jnp.where(deye[None], jnp.broadcast_to(gl3, (N_GROUPS, PACK, PACK)), 0.0) - bw[:, :, :PACK]
    c_mat = bdot(khat_t, v_s) - bw[:, :, PACK:]

    st = st_sc[...]
    qm = bdot(jnp.concatenate([q_s, m_mat], axis=1), st)
    ys = qm[:, :GL] + y0_s
    st_sc[...] = qm[:, GL:] + c_mat
    yg = ys[:, 0:L]
    for h in range(1, GROUP_HEADS):
        yg = yg + ys[:, h * L:(h + 1) * L]
    y_groups = [yg[gi] for gi in range(N_GROUPS)]

    y = jnp.concatenate(y_groups, axis=1)
    inv_head = 1.0 / HEAD
    mu = _seg_sum(y, e_ref, et_ref) * inv_head
    yc = y - mu
    var = _seg_sum(yc * yc, e_ref, et_ref) * inv_head
    yn = yc * lax.rsqrt(var + GN_EPS) * lng_ref[...] + lnb_ref[...]
    y_ref[0] = ((yn + bonus) * g).astype(y_ref.dtype)

    @pl.when(c == n_chunks - 1)
    def _():
        stout_ref[0] = st_sc[...]
        shout_ref[0, :, 0:C_RWKV] = pr_sc[...]
        shout_ref[0, :, C_RWKV:2 * C_RWKV] = pk_sc[...]
        shout_ref[0, :, 2 * C_RWKV:3 * C_RWKV] = pv_sc[...]
        shout_ref[0, :, 3 * C_RWKV:3 * C_RWKV + LORA_PAD] = plo_sc[...]


def _rwkv_mix(proj3, shift_parts, st0, params, chunk):
    bsz, t_len, _ = proj3.shape
    L = chunk
    rkv_blk0 = 2 * C_CONV // C_RWKV
    lora_blk = (2 * C_CONV + 3 * C_RWKV) // LORA_PAD
    row = lambda n: pl.BlockSpec((1, n), lambda b, c: (0, 0))
    full = lambda s: pl.BlockSpec(s, lambda b, c: tuple(0 for _ in s))
    sh = lambda n: pl.BlockSpec((1, 1, n), lambda b, c: (b, 0, 0))
    in_specs = [
        pl.BlockSpec((1, L, C_RWKV), lambda b, c: (b, c, rkv_blk0)),
        pl.BlockSpec((1, L, C_RWKV), lambda b, c: (b, c, rkv_blk0 + 1)),
        pl.BlockSpec((1, L, C_RWKV), lambda b, c: (b, c, rkv_blk0 + 2)),
        pl.BlockSpec((1, L, LORA_PAD), lambda b, c: (b, c, lora_blk)),
        sh(C_RWKV), sh(C_RWKV), sh(C_RWKV), sh(LORA_PAD),
        pl.BlockSpec((1, N_GROUPS, PACK, PACK), lambda b, c: (b, 0, 0, 0)),
        row(C_RWKV), row(C_RWKV), row(C_RWKV), row(LORA_PAD),
        row(C_RWKV), full((LANES, C_RWKV)), full((LANES, C_RWKV)),
        row(C_RWKV), full((LANES, C_RWKV)), full((LANES, C_RWKV)),
        full((2 * LANES, C_RWKV)), full((2 * LANES, C_RWKV)),
        row(C_RWKV), row(C_RWKV), row(C_RWKV), row(C_RWKV), row(C_RWKV),
        full((C_RWKV, LANES)), full((LANES, C_RWKV)),
    ]
    out_shape = (jax.ShapeDtypeStruct((bsz, t_len, C_RWKV), BF16),
                 jax.ShapeDtypeStruct((bsz, N_GROUPS, PACK, PACK), F32),
                 jax.ShapeDtypeStruct((bsz, 1, 3 * C_RWKV + LORA_PAD), F32))
    out_specs = (pl.BlockSpec((1, L, C_RWKV), lambda b, c: (b, c, 0)),
                 pl.BlockSpec((1, N_GROUPS, PACK, PACK), lambda b, c: (b, 0, 0, 0)),
                 pl.BlockSpec((1, 1, 3 * C_RWKV + LORA_PAD), lambda b, c: (b, 0, 0)))
    return pl.pallas_call(
        _rwkv_kernel,
        out_shape=out_shape,
        grid=(bsz, t_len // L),
        in_specs=in_specs,
        out_specs=out_specs,
        scratch_shapes=[pltpu.VMEM((N_GROUPS, PACK, PACK), F32),
                        pltpu.VMEM((1, C_RWKV), F32), pltpu.VMEM((1, C_RWKV), F32),
                        pltpu.VMEM((1, C_RWKV), F32), pltpu.VMEM((1, LORA_PAD), F32)],
        compiler_params=pltpu.CompilerParams(dimension_semantics=("arbitrary", "arbitrary")),
        name="rwkv7_mix",
    )(proj3, proj3, proj3, proj3, *shift_parts, st0, *params)


def _outproj_kernel(c_ref, y_ref, x_ref, wa_ref, wb_ref, g_ref, b_ref, rwh_ref, rwl_ref, rb_ref,
                    x1_ref, idx_ref, gate_ref):
    mix = (jnp.dot(c_ref[...], wa_ref[...], preferred_element_type=F32)
           + jnp.dot(y_ref[...], wb_ref[...], preferred_element_type=F32))
    h = ALPHA * x_ref[...] + mix
    mu = jnp.mean(h, axis=-1, keepdims=True)
    hc = h - mu
    var = jnp.mean(hc * hc, axis=-1, keepdims=True)
    x1 = hc * lax.rsqrt(var + LN_EPS) * g_ref[...] + b_ref[...]
    x1_ref[...] = x1
    logits = _dot_split_w(x1, rwh_ref[...], rwl_ref[...]) + rb_ref[...]
    lane = lax.broadcasted_iota(jnp.int32, logits.shape, 1)
    idx_out = jnp.zeros(logits.shape, jnp.int32)
    val_out = jnp.zeros(logits.shape, F32)
    vals = []
    for kk in range(TOP_K):
        m = jnp.max(logits, axis=-1, keepdims=True)
        sel = jnp.min(jnp.where(logits == m, lane, ROUTER_PAD), axis=-1, keepdims=True)
        vals.append(m)
        idx_out = jnp.where(lane == kk, sel, idx_out)
        logits = jnp.where(lane == sel, -jnp.inf, logits)
    exps = [jnp.exp(vv - vals[0]) for vv in vals]
    denom = exps[0]
    for ee in exps[1:]:
        denom = denom + ee
    for kk in range(TOP_K):
        val_out = jnp.where(lane == kk, exps[kk] / denom, val_out)
    idx_ref[...] = idx_out
    gate_ref[...] = val_out


def _outproj_into_kernel(base_ref, *refs):
    del base_ref
    _outproj_kernel(*refs)


def _out_proj(c2, y2, x2, wa, wb, ln_g, ln_b, rw_hi, rw_lo, rb, tm, n_total, x1_base=None, row0=0):
    n = x2.shape[0]
    blk0 = row0 // tm
    assert blk0 * tm == row0
    row = lambda w: pl.BlockSpec((1, w), lambda i: (0, 0))
    in_specs = [pl.BlockSpec((tm, C_CONV), lambda i: (i, 0)),
                pl.BlockSpec((tm, C_RWKV), lambda i: (i, 0)),
                pl.BlockSpec((tm, D_MODEL), lambda i: (i, 0)),
                pl.BlockSpec((C_CONV, D_MODEL), lambda i: (0, 0)),
                pl.BlockSpec((C_RWKV, D_MODEL), lambda i: (0, 0)),
                row(D_MODEL), row(D_MODEL),
                pl.BlockSpec((D_MODEL, ROUTER_PAD), lambda i: (0, 0)),
                pl.BlockSpec((D_MODEL, ROUTER_PAD), lambda i: (0, 0)),
                row(ROUTER_PAD)]
    args = (c2, y2, x2, wa, wb, ln_g, ln_b, rw_hi, rw_lo, rb)
    body, aliases = _outproj_kernel, {}
    if x1_base is not None:
        in_specs = [pl.BlockSpec(memory_space=pl.ANY)] + in_specs
        args = (x1_base,) + args
        body, aliases = _outproj_into_kernel, {0: 0}
    return pl.pallas_call(
        body,
        out_shape=(jax.ShapeDtypeStruct((n_total, D_MODEL), F32),
                   jax.ShapeDtypeStruct((n, ROUTER_PAD), jnp.int32),
                   jax.ShapeDtypeStruct((n, ROUTER_PAD), F32)),
        grid=(n // tm,),
        in_specs=in_specs,
        out_specs=(pl.BlockSpec((tm, D_MODEL), lambda i: (i + blk0, 0)),
                   pl.BlockSpec((tm, ROUTER_PAD), lambda i: (i, 0)),
                   pl.BlockSpec((tm, ROUTER_PAD), lambda i: (i, 0))),
        input_output_aliases=aliases,
        compiler_params=pltpu.CompilerParams(dimension_semantics=("arbitrary",)),
        name="out_proj_ln_router",
    )(*args)


DMA_PRIORITIES = 2


def _gather_rows_kernel(nused_ref, tok_ref, x_hbm, o_ref, buf_ref, sem):
    rows = o_ref.shape[0]
    used = pl.program_id(0) < nused_ref[0]

    def row_copy(t, r):
        return pltpu.make_async_copy(x_hbm.at[pl.ds(t, 1)], buf_ref.at[pl.ds(r, 1)], sem)

    @pl.when(used)
    def _():
        def issue(q, carry):
            for p in range(DMA_PRIORITIES):
                r = q * DMA_PRIORITIES + p
                row_copy(tok_ref[0, 0, r], r).start(priority=p)
            return carry

        lax.fori_loop(0, rows // DMA_PRIORITIES, issue, 0, unroll=4)

        def drain(r, carry):
            row_copy(0, r).wait()
            return carry

        lax.fori_loop(0, rows, drain, 0, unroll=8)
        o_ref[...] = buf_ref[...].astype(o_ref.dtype)

    @pl.when(jnp.logical_not(used))
    def _():
        o_ref[...] = jnp.zeros_like(o_ref)


def _gather_rows(n_used, row_tok3, x):
    nb, _, rows = row_tok3.shape
    width = x.shape[1]
    grid_spec = pltpu.PrefetchScalarGridSpec(
        num_scalar_prefetch=1,
        grid=(nb,),
        in_specs=[pl.BlockSpec((1, 1, rows), lambda i, nu: (i, 0, 0), memory_space=pltpu.SMEM),
                  pl.BlockSpec(memory_space=pl.ANY)],
        out_specs=pl.BlockSpec((rows, width), lambda i, nu: (i, 0)),
        scratch_shapes=[pltpu.VMEM((rows, width), x.dtype), pltpu.SemaphoreType.DMA],
    )
    return pl.pallas_call(
        _gather_rows_kernel,
        out_shape=jax.ShapeDtypeStruct((nb * rows, width), BF16),
        grid_spec=grid_spec,
        compiler_params=pltpu.CompilerParams(dimension_semantics=("arbitrary",)),
        name="moe_gather_rows",
    )(n_used, row_tok3, x)


def _expert_kernel(bexp_ref, first_ref, nused_ref, x_ref, wg_ref, bg_ref, wu_ref, bu_ref, wd_ref, bd_ref,
                   o_ref, wgc_ref, wuc_ref, wdc_ref):
    i = pl.program_id(0)
    j = pl.program_id(1)
    nj = pl.num_programs(1)

    @pl.when(i < nused_ref[0])
    def _():
        @pl.when(first_ref[i] == 1)
        def _():
            wgc_ref[j] = wg_ref[0].astype(BF16)
            wuc_ref[j] = wu_ref[0].astype(BF16)
            wdc_ref[j] = wd_ref[0].astype(BF16)

        @pl.when(j == 0)
        def _():
            o_ref[...] = jnp.broadcast_to(bd_ref[0], o_ref.shape)

        x = x_ref[...]
        gate = jnp.minimum(jnp.dot(x, wgc_ref[j], preferred_element_type=F32) + bg_ref[0, pl.ds(j, 1), :],
                           SWIGLU_LIMIT)
        up = jnp.clip(jnp.dot(x, wuc_ref[j], preferred_element_type=F32) + bu_ref[0, pl.ds(j, 1), :],
                      -SWIGLU_LIMIT, SWIGLU_LIMIT)
        hmid = (up + 1.0) * gate * jax.nn.sigmoid(SWIGLU_ALPHA * gate)
        o_ref[...] += jnp.dot(hmid.astype(BF16), wdc_ref[j], preferred_element_type=F32)

    @pl.when((i >= nused_ref[0]) & (j == nj - 1))
    def _():
        o_ref[...] = jnp.zeros_like(o_ref)


def _experts(block_exp, first, n_used, xb, wg, bg, wu, bu, wd, bd):
    n_rows = xb.shape[0]
    nb = n_rows // MOE_TM
    nj = D_FF // MOE_TF

    def wsel(i, j, fi):
        return jnp.where(fi[i] == 1, j, nj - 1)

    once = pl.Buffered(1)
    grid_spec = pltpu.PrefetchScalarGridSpec(
        num_scalar_prefetch=3,
        grid=(nb, nj),
        in_specs=[pl.BlockSpec((MOE_TM, D_MODEL), lambda i, j, be, fi, nu: (i, 0)),
                  pl.BlockSpec((1, D_MODEL, MOE_TF), lambda i, j, be, fi, nu: (be[i], 0, wsel(i, j, fi)),
                               pipeline_mode=once),
                  pl.BlockSpec((1, nj, MOE_TF), lambda i, j, be, fi, nu: (be[i], 0, 0)),
                  pl.BlockSpec((1, D_MODEL, MOE_TF), lambda i, j, be, fi, nu: (be[i], 0, wsel(i, j, fi)),
                               pipeline_mode=once),
                  pl.BlockSpec((1, nj, MOE_TF), lambda i, j, be, fi, nu: (be[i], 0, 0)),
                  pl.BlockSpec((1, MOE_TF, D_MODEL), lambda i, j, be, fi, nu: (be[i], wsel(i, j, fi), 0),
                               pipeline_mode=once),
                  pl.BlockSpec((1, 1, D_MODEL), lambda i, j, be, fi, nu: (be[i], 0, 0))],
        out_specs=pl.BlockSpec((MOE_TM, D_MODEL), lambda i, j, be, fi, nu: (i, 0)),
        scratch_shapes=[pltpu.VMEM((nj, D_MODEL, MOE_TF), BF16), pltpu.VMEM((nj, D_MODEL, MOE_TF), BF16),
                        pltpu.VMEM((nj, MOE_TF, D_MODEL), BF16)],
    )
    return pl.pallas_call(
        _expert_kernel,
        out_shape=jax.ShapeDtypeStruct((n_rows, D_MODEL), F32),
        grid_spec=grid_spec,
        compiler_params=pltpu.CompilerParams(dimension_semantics=("arbitrary", "arbitrary"),
                                             vmem_limit_bytes=MOE_VMEM_LIMIT),
        name="moe_experts",
    )(block_exp, first, n_used, xb, wg, bg.reshape(N_EXPERTS, nj, MOE_TF), wu, bu.reshape(N_EXPERTS, nj, MOE_TF),
      wd, bd.reshape(N_EXPERTS, 1, D_MODEL))


def _combine_kernel(dest_ref, gate_ref, x1_ref, g_ref, b_ref, yb_hbm, o_ref, buf_ref, sem):
    tc = x1_ref.shape[0]

    def issue(r, carry):
        for kk in range(TOP_K):
            d = dest_ref[0, 0, r * TOP_K + kk]
            pltpu.make_async_copy(yb_hbm.at[pl.ds(d, 1)], buf_ref.at[kk, pl.ds(r, 1)], sem).start(
                priority=kk % DMA_PRIORITIES)
        return carry

    lax.fori_loop(0, tc, issue, 0, unroll=2)

    def drain(r, carry):
        for kk in range(TOP_K):
            pltpu.make_async_copy(yb_hbm.at[pl.ds(0, 1)], buf_ref.at[kk, pl.ds(r, 1)], sem).wait()
        return carry

    lax.fori_loop(0, tc, drain, 0, unroll=2)

    gates = gate_ref[...]
    moe = gates[:, 0:1] * buf_ref[0]
    for kk in range(1, TOP_K):
        moe = moe + gates[:, kk:kk + 1] * buf_ref[kk]
    h = ALPHA * x1_ref[...] + moe
    mu = jnp.mean(h, axis=-1, keepdims=True)
    hc = h - mu
    var = jnp.mean(hc * hc, axis=-1, keepdims=True)
    o_ref[...] = hc * lax.rsqrt(var + LN_EPS) * g_ref[...] + b_ref[...]


def _combine(dest3, gates, x1, ln_g, ln_b, yb, tc, row0=0):
    n = dest3.shape[0] * tc
    blk0 = row0 // tc
    assert blk0 * tc == row0
    return pl.pallas_call(
        _combine_kernel,
        out_shape=jax.ShapeDtypeStruct((n, D_MODEL), F32),
        grid=(n // tc,),
        in_specs=[pl.BlockSpec((1, 1, tc * TOP_K), lambda i: (i, 0, 0), memory_space=pltpu.SMEM),
                  pl.BlockSpec((tc, ROUTER_PAD), lambda i: (i, 0)),
                  pl.BlockSpec((tc, D_MODEL), lambda i: (i + blk0, 0)),
                  pl.BlockSpec((1, D_MODEL), lambda i: (0, 0)),
                  pl.BlockSpec((1, D_MODEL), lambda i: (0, 0)),
                  pl.BlockSpec(memory_space=pl.ANY)],
        out_specs=pl.BlockSpec((tc, D_MODEL), lambda i: (i, 0)),
        scratch_shapes=[pltpu.VMEM((TOP_K, tc, D_MODEL), F32), pltpu.SemaphoreType.DMA],
        compiler_params=pltpu.CompilerParams(dimension_semantics=("arbitrary",)),
        name="moe_combine_ln2",
    )(dest3, gates, x1, ln_g, ln_b, yb)


def _pad_cols(w, width):
    return jnp.pad(w, ((0, 0), (0, width - w.shape[1])))


def _pad_rows(w, height):
    return jnp.pad(w, ((0, height - w.shape[0]), (0, 0)))


def _split_lora_cols(w):
    xw = w[..., 0:R_DECAY]
    xa = w[..., R_DECAY:R_DECAY + R_ICLR]
    xg = w[..., R_DECAY + R_ICLR:]
    pad = lambda x, n: jnp.pad(x, [(0, 0)] * (x.ndim - 1) + [(0, n - x.shape[-1])])
    return jnp.concatenate([pad(xw, LANES), pad(xa, LANES), pad(xg, 2 * LANES)], axis=-1)


def _pick(n, prefs):
    for p in prefs:
        if n % p == 0:
            return p
    return n


def _mixer_group(x, conv_buf, shift_buf, wkv_state, wts):
    bsz, t_len, _ = x.shape
    n = bsz * t_len
    proj = _in_proj(x.reshape(n, D_MODEL), wts["w_in"], wts["b_in"], _pick(n, (1024, 512, 256, 128)), 512)
    proj3 = proj.reshape(bsz, t_len, P_PAD)

    hist = jnp.pad(conv_buf, ((0, 0), (HIST - (CONV_WIDTH - 1), 0), (0, 0)))
    c, tail = _conv_module(proj3, hist, wts["conv_w"], wts["conv_b"], wts["conv_ln_g"], wts["conv_ln_b"],
                           _pick(t_len, (128, 64, 32, 16, 8)))
    new_conv = tail[:, HIST - (CONV_WIDTH - 1):, :]

    sh_rkv = shift_buf[:, :, :3 * C_RWKV]
    sh_lo = _split_lora_cols(shift_buf[:, :, 3 * C_RWKV:])
    shift_parts = (sh_rkv[:, :, 0:C_RWKV], sh_rkv[:, :, C_RWKV:2 * C_RWKV], sh_rkv[:, :, 2 * C_RWKV:], sh_lo)
    st_t = jnp.swapaxes(wkv_state, -1, -2).reshape(bsz, N_GROUPS, GROUP_HEADS, HEAD, HEAD)
    eye_h = jnp.eye(GROUP_HEADS, dtype=F32)
    st0 = jnp.einsum("bghkv,hj->bghkjv", st_t, eye_h).reshape(bsz, N_GROUPS, PACK, PACK)
    yb, st_out, sh_out = _rwkv_mix(proj3, shift_parts, st0, wts["rwkv_params"], _pick(t_len, (64, 32, 16)))
    st5 = st_out.reshape(bsz, N_GROUPS, GROUP_HEADS, HEAD, GROUP_HEADS, HEAD)
    st_diag = jnp.einsum("bghkhv->bghkv", st5)
    new_wkv = jnp.swapaxes(st_diag, -1, -2).reshape(bsz, N_HEADS, HEAD, HEAD)
    lo = sh_out[:, :, 3 * C_RWKV:]
    new_shift = jnp.concatenate([sh_out[:, :, :3 * C_RWKV], lo[:, :, 0:R_DECAY], lo[:, :, LANES:LANES + R_ICLR],
                                 lo[:, :, 2 * LANES:2 * LANES + R_GATE]], axis=-1)
    return c.reshape(n, C_CONV), yb.reshape(n, C_RWKV), new_conv, new_shift, new_wkv


def _route(top_idx, n_tok):
    n_assign = n_tok * TOP_K
    flat_e = top_idx.reshape(-1)
    onehot = (flat_e[:, None] == jnp.arange(N_EXPERTS, dtype=jnp.int32)[None, :]).astype(jnp.int32)
    csum = jnp.cumsum(onehot, axis=0)
    rank = jnp.take_along_axis(csum, flat_e[:, None], axis=1)[:, 0] - 1
    counts = csum[-1]
    padded = (counts + MOE_TM - 1) // MOE_TM * MOE_TM
    seg_end = jnp.cumsum(padded)
    seg_start = seg_end - padded
    dest = (seg_start[flat_e] + rank).astype(jnp.int32)
    n_rows = (n_assign + N_EXPERTS * (MOE_TM - 1) + MOE_TM - 1) // MOE_TM * MOE_TM
    n_blocks = n_rows // MOE_TM
    row_tok = jnp.zeros((n_rows,), jnp.int32).at[dest].set(jnp.arange(n_assign, dtype=jnp.int32) // TOP_K)
    block_start = jnp.arange(n_blocks, dtype=jnp.int32) * MOE_TM
    block_exp = jnp.minimum(jnp.sum((seg_end[None, :] <= block_start[:, None]).astype(jnp.int32), axis=1),
                            N_EXPERTS - 1).astype(jnp.int32)
    first = ((block_start == seg_start[block_exp]) & (block_start < seg_end[-1])).astype(jnp.int32)
    n_used = (seg_end[-1] // MOE_TM).astype(jnp.int32).reshape(1)
    return dest, row_tok, block_exp, first, n_used, n_blocks


def kernel(x_prompt, x_sample, state_conv, state_shift, state_wkv, w_in, b_in, mu_shift, conv_w, conv_b,
           conv_ln_g, conv_ln_b, rwkv_w0, rwkv_w2, rwkv_a0, rwkv_a2, rwkv_g2, rwkv_k_k, rwkv_k_a, rwkv_r_k,
           rwkv_ln_g, rwkv_ln_b, w_out, ln1_g, ln1_b, router_w, router_b, w_gate, b_gate, w_up, b_up,
           w_down, b_down, ln2_g, ln2_b):
    assert w_in.shape[0] == 1, "single layer"
    d = 0
    row = lambda v: v.reshape(1, -1)
    n_p, t_p, _ = x_prompt.shape
    n_s, t_s, _ = x_sample.shape

    w_rkv = w_in[d][:, 2 * C_CONV:2 * C_CONV + 3 * C_RWKV]
    w_lo = _split_lora_cols(w_in[d][:, 2 * C_CONV + 3 * C_RWKV:])
    w_in_p = jnp.concatenate([w_in[d][:, :2 * C_CONV], w_rkv, w_lo], axis=1).astype(BF16)
    b_in_p = jnp.concatenate([b_in[d][None, :2 * C_CONV], b_in[d][None, 2 * C_CONV:2 * C_CONV + 3 * C_RWKV],
                              _split_lora_cols(b_in[d][None, 2 * C_CONV + 3 * C_RWKV:])], axis=1)
    mu = mu_shift[d][None, :]
    mu_lo = _split_lora_cols(mu[:, 3 * C_RWKV:])
    head_of_lane = jnp.arange(C_RWKV, dtype=jnp.int32) // HEAD
    e_mat = (head_of_lane[:, None] == jnp.arange(LANES, dtype=jnp.int32)[None, :]).astype(BF16)

    def hi_lo(w, height):
        w = _pad_rows(w, height)
        w_hi = w.astype(BF16)
        return w_hi, (w - w_hi.astype(F32)).astype(BF16)

    rwkv_params = (
        mu[:, 0:C_RWKV], mu[:, C_RWKV:2 * C_RWKV], mu[:, 2 * C_RWKV:3 * C_RWKV], mu_lo,
        row(rwkv_w0[d]), *hi_lo(rwkv_w2[d], LANES), row(rwkv_a0[d]), *hi_lo(rwkv_a2[d], LANES),
        *hi_lo(rwkv_g2[d], 2 * LANES),
        row(rwkv_k_k[d]), row(rwkv_k_a[d]), row(rwkv_r_k[d]), row(rwkv_ln_g[d]), row(rwkv_ln_b[d]),
        e_mat, e_mat.T,
    )
    wts = dict(w_in=w_in_p, b_in=b_in_p, conv_w=conv_w[d], conv_b=row(conv_b[d]),
               conv_ln_g=row(conv_ln_g[d]), conv_ln_b=row(conv_ln_b[d]), rwkv_params=rwkv_params)

    zero_conv = jnp.zeros((n_p, CONV_WIDTH - 1, C_CONV), x_prompt.dtype)
    zero_shift = jnp.zeros((n_p, 1, N_SHIFT), x_prompt.dtype)
    zero_wkv = jnp.zeros((n_p, N_HEADS, HEAD, HEAD), state_wkv.dtype)
    c_p, y_p, conv_p, shift_p, wkv_p = _mixer_group(x_prompt, zero_conv, zero_shift, zero_wkv, wts)
    c_s, y_s, conv_s, shift_s, wkv_s = _mixer_group(x_sample, state_conv[d], state_shift[d], state_wkv[d], wts)

    w_out_b = w_out[d].astype(BF16)
    rw = _pad_cols(router_w[d], ROUTER_PAD)
    rw_hi = rw.astype(BF16)
    rw_lo = (rw - rw_hi.astype(F32)).astype(BF16)
    rb = jnp.concatenate([router_b[d], jnp.full((ROUTER_PAD - N_EXPERTS,), -jnp.inf, F32)])[None, :]

    n_tok_p = n_p * t_p
    n_tok_s = n_s * t_s
    n_tok = n_tok_p + n_tok_s

    def out_proj(c2, y2, x3, x1_base, row0):
        n = c2.shape[0]
        tm = math.gcd(_pick(n, (OUT_TM, 256, 128)), row0) if row0 else _pick(n, (OUT_TM, 256, 128))
        return _out_proj(c2, y2, x3.reshape(n, D_MODEL), w_out_b[:C_CONV], w_out_b[C_CONV:], row(ln1_g[d]),
                         row(ln1_b[d]), rw_hi, rw_lo, rb, tm, n_tok, x1_base, row0)

    x1, idx_p, gate_p = out_proj(c_p, y_p, x_prompt, None, 0)
    x1, idx_s, gate_s = out_proj(c_s, y_s, x_sample, x1, n_tok_p)
    top_idx = jnp.concatenate([idx_p[:, :TOP_K], idx_s[:, :TOP_K]], axis=0)
    dest, row_tok, block_exp, first, n_used, n_blocks = _route(top_idx, n_tok)
    xb = _gather_rows(n_used, row_tok.reshape(n_blocks, 1, MOE_TM), x1)
    yb = _experts(block_exp, first, n_used, xb, w_gate[d], b_gate[d], w_up[d], b_up[d], w_down[d], b_down[d])

    def combine(dest_g, gate_g, n, row0):
        tc = _pick(n, (128, 64, 32, 16, 8))
        tc = math.gcd(tc, row0) if row0 else tc
        return _combine(dest_g.reshape(n // tc, 1, tc * TOP_K), gate_g, x1, row(ln2_g[d]), row(ln2_b[d]), yb, tc,
                        row0)

    y_prompt = combine(dest[:n_tok_p * TOP_K], gate_p, n_tok_p, 0).reshape(n_p, t_p, D_MODEL)
    y_sample = combine(dest[n_tok_p * TOP_K:], gate_s, n_tok_s, n_tok_p).reshape(n_s, t_s, D_MODEL)
    return (y_prompt, y_sample, conv_p[None], shift_p[None], wkv_p[None], conv_s[None], shift_s[None], wkv_s[None])
```

```python
import functools
import math

import jax
import jax.numpy as jnp
from jax import lax
from jax.experimental import pallas as pl
from jax.experimental.pallas import tpu as pltpu

F32 = jnp.float32
BF16 = jnp.bfloat16

D_MODEL = 2048
C_CONV = 1024
C_RWKV = 1024
HEAD = 64
N_HEADS = C_RWKV // HEAD
CONV_WIDTH = 31
R_DECAY = 64
R_ICLR = 64
R_GATE = 160
N_SHIFT = 3 * C_RWKV + R_DECAY + R_ICLR + R_GATE
N_EXPERTS = 32
TOP_K = 4
D_FF = 2048
SWIGLU_LIMIT = 7.0
SWIGLU_ALPHA = 1.702
LN_EPS = 1e-5
GN_EPS = 64e-5
ALPHA = 2.0 ** 0.25

LANES = 128
SUBLANES = 8

HIST = 32
LORA_PAD = 512
P_PAD = 2 * C_CONV + 3 * C_RWKV + LORA_PAD
GROUP_HEADS = 2
PACK = GROUP_HEADS * HEAD
N_GROUPS = N_HEADS // GROUP_HEADS
MOE_TM = 512
MOE_TF = 512
OUT_TM = 512
RWKV_CHUNK = 64
RWKV_BLOCK = 128
ROUTER_PAD = LANES


def _dot(a, b, prec=1, dims=(((1,), (0,)), ((), ()))):
    if prec == 6:
        return lax.dot_general(a.astype(F32), b.astype(F32), dims, precision=lax.Precision.HIGHEST,
                               preferred_element_type=F32)
    d = lambda x, y: lax.dot_general(x, y, dims, preferred_element_type=F32)
    if prec == 1:
        return d(a.astype(BF16), b.astype(BF16))
    a_hi = a.astype(BF16)
    a_lo = (a - a_hi.astype(F32)).astype(BF16)
    b_hi = b.astype(BF16)
    b_lo = (b - b_hi.astype(F32)).astype(BF16)
    return d(a_hi, b_hi) + d(a_hi, b_lo) + d(a_lo, b_hi)


_NT = (((1,), (1,)), ((), ()))
_BNN = (((2,), (1,)), ((0,), (0,)))
_BNT = (((2,), (2,)), ((0,), (0,)))


def _split3(x):
    p1 = x.astype(BF16)
    r1 = x - p1.astype(F32)
    p2 = r1.astype(BF16)
    p3 = (r1 - p2.astype(F32)).astype(BF16)
    return p1, p2, p3


def _dot_exact_rhs(x, m_bf16):
    d = lambda a: jnp.dot(a, m_bf16, preferred_element_type=F32)
    p1, p2, p3 = _split3(x)
    return d(p1) + d(p2) + d(p3)


def _dot_exact_lhs(m_bf16, x):
    d = lambda a: jnp.dot(m_bf16, a, preferred_element_type=F32)
    p1, p2, p3 = _split3(x)
    return d(p1) + d(p2) + d(p3)


def _dot_split_w(x, w_hi, w_lo):
    x_hi = x.astype(BF16)
    x_lo = (x - x_hi.astype(F32)).astype(BF16)
    d = lambda a, b: jnp.dot(a, b, preferred_element_type=F32)
    return d(x_hi, w_hi) + d(x_hi, w_lo) + d(x_lo, w_hi)


def _mm_bias_kernel(x_ref, w_ref, b_ref, o_ref, xb_ref):
    @pl.when(pl.program_id(1) == 0)
    def _():
        xb_ref[...] = x_ref[...].astype(BF16)

    o_ref[...] = jnp.dot(xb_ref[...], w_ref[...], preferred_element_type=F32) + b_ref[...]


def _in_proj(x, w_bf16, b, tm, tn):
    n, k = x.shape
    p = w_bf16.shape[1]
    return pl.pallas_call(
        _mm_bias_kernel,
        out_shape=jax.ShapeDtypeStruct((n, p), F32),
        grid=(n // tm, p // tn),
        in_specs=[pl.BlockSpec((tm, k), lambda i, j: (i, 0)),
                  pl.BlockSpec((k, tn), lambda i, j: (0, j)),
                  pl.BlockSpec((1, tn), lambda i, j: (0, j))],
        out_specs=pl.BlockSpec((tm, tn), lambda i, j: (i, j)),
        scratch_shapes=[pltpu.VMEM((tm, k), BF16)],
        compiler_params=pltpu.CompilerParams(dimension_semantics=("arbitrary", "arbitrary")),
        name="in_proj",
    )(x, w_bf16, b)


def _conv_kernel(val_ref, gate_ref, hist_ref, w_ref, cb_ref, g_ref, b_ref, c_ref, tail_ref, ext_ref, sh_ref):
    t = pl.program_id(1)
    tt = val_ref.shape[1]

    @pl.when(t == 0)
    def _():
        ext_ref[0:HIST, :] = hist_ref[0]

    u = val_ref[0] * jax.nn.sigmoid(gate_ref[0])
    ext_ref[HIST:HIST + tt, :] = u
    span = tt + HIST - SUBLANES
    for s in range(1, SUBLANES):
        sh_ref[s, 0:span, :] = ext_ref[s:s + span, :]
    off = HIST - (CONV_WIDTH - 1)
    acc = jnp.broadcast_to(cb_ref[...], (tt, C_CONV))
    for j in range(CONV_WIDTH):
        base = (off + j) // SUBLANES * SUBLANES
        s = (off + j) % SUBLANES
        src = ext_ref[base:base + tt, :] if s == 0 else sh_ref[s, base:base + tt, :]
        acc = acc + w_ref[j:j + 1, :] * src
    mu = jnp.mean(acc, axis=-1, keepdims=True)
    xc = acc - mu
    var = jnp.mean(xc * xc, axis=-1, keepdims=True)
    y = xc * lax.rsqrt(var + LN_EPS) * g_ref[...] + b_ref[...]
    c_ref[0] = (y * jax.nn.sigmoid(y)).astype(c_ref.dtype)
    tail = ext_ref[tt:tt + HIST, :]
    ext_ref[0:HIST, :] = tail
    tail_ref[0] = tail


def _conv_module(proj3, hist, conv_w, conv_b, ln_g, ln_b, tt):
    bsz, t_len, _ = proj3.shape
    nblk = C_CONV // C_CONV
    return pl.pallas_call(
        _conv_kernel,
        out_shape=(jax.ShapeDtypeStruct((bsz, t_len, C_CONV), BF16),
                   jax.ShapeDtypeStruct((bsz, HIST, C_CONV), F32)),
        grid=(bsz, t_len // tt),
        in_specs=[pl.BlockSpec((1, tt, C_CONV), lambda b, t: (b, t, 0)),
                  pl.BlockSpec((1, tt, C_CONV), lambda b, t: (b, t, nblk)),
                  pl.BlockSpec((1, HIST, C_CONV), lambda b, t: (b, 0, 0)),
                  pl.BlockSpec((CONV_WIDTH, C_CONV), lambda b, t: (0, 0)),
                  pl.BlockSpec((1, C_CONV), lambda b, t: (0, 0)),
                  pl.BlockSpec((1, C_CONV), lambda b, t: (0, 0)),
                  pl.BlockSpec((1, C_CONV), lambda b, t: (0, 0))],
        out_specs=(pl.BlockSpec((1, tt, C_CONV), lambda b, t: (b, t, 0)),
                   pl.BlockSpec((1, HIST, C_CONV), lambda b, t: (b, 0, 0))),
        scratch_shapes=[pltpu.VMEM((HIST + tt, C_CONV), F32),
                        pltpu.VMEM((SUBLANES, HIST + tt, C_CONV), F32)],
        compiler_params=pltpu.CompilerParams(dimension_semantics=("arbitrary", "arbitrary")),
        name="conv_module",
    )(proj3, proj3, hist, conv_w, conv_b, ln_g, ln_b)


PREC_CHUNK = 1
PREC_STATE = 1


def _seg_sum(x, e_ref, et_ref):
    return _dot_exact_rhs(_dot_exact_rhs(x, e_ref[...]), et_ref[...])


def _rwkv_kernel(chunk, n_cast, *refs):
    n_in = 28
    (r_ref, k_ref, v_ref, lo_ref, shr_ref, shk_ref, shv_ref, shlo_ref, st0_ref,
     mur_ref, muk_ref, muv_ref, mulo_ref, w0_ref, w2h_ref, w2l_ref, a0_ref, a2h_ref, a2l_ref,
     g2h_ref, g2l_ref, kkw_ref, kaw_ref, rkw_ref, lng_ref, lnb_ref, e_ref, et_ref) = refs[:n_in]
    cast_in = refs[n_in:n_in + n_cast]
    y_ref, stout_ref, shout_ref = refs[n_in + n_cast:n_in + n_cast + 3]
    cast_out = refs[n_in + n_cast + 3:n_in + 2 * n_cast + 3]
    st_sc, pr_sc, pk_sc, pv_sc, plo_sc = refs[n_in + 2 * n_cast + 3:]

    for src, dst in zip(cast_in, cast_out):
        dst[...] = src[...].astype(dst.dtype)

    c = pl.program_id(1)
    n_chunks = pl.num_programs(1)
    Tb = r_ref.shape[1]
    L = chunk
    n_sub = Tb // L
    GL = GROUP_HEADS * L
    log2l = int(math.log2(L))

    @pl.when(c == 0)
    def _():
        st_sc[...] = st0_ref[0]
        pr_sc[...] = shr_ref[0]
        pk_sc[...] = shk_ref[0]
        pv_sc[...] = shv_ref[0]
        plo_sc[...] = shlo_ref[0]

    def token_shift(x_ref, prev_sc, mu_ref):
        x = x_ref[0]
        row = lax.broadcasted_iota(jnp.int32, x.shape, 0)
        xprev = jnp.where(row == 0, jnp.broadcast_to(prev_sc[...], x.shape), pltpu.roll(x, 1, 0))
        prev_sc[...] = x[Tb - 1:Tb, :]
        return x + mu_ref[...] * (xprev - x)

    r = token_shift(r_ref, pr_sc, mur_ref)
    k = token_shift(k_ref, pk_sc, muk_ref)
    v = token_shift(v_ref, pv_sc, muv_ref)
    lo = token_shift(lo_ref, plo_sc, mulo_ref)
    xw = lo[:, 0:LANES]
    xa = lo[:, LANES:2 * LANES]
    xg = lo[:, 2 * LANES:LORA_PAD]

    u_dec = w0_ref[...] + _dot_split_w(jnp.tanh(xw), w2h_ref[...], w2l_ref[...])
    logw = (-math.exp(-0.5)) * jax.nn.sigmoid(u_dec)
    a = jax.nn.sigmoid(a0_ref[...] + _dot_split_w(xa, a2h_ref[...], a2l_ref[...]))
    g = _dot_split_w(jax.nn.sigmoid(xg), g2h_ref[...], g2l_ref[...])

    kk = k * kkw_ref[...]
    nrm = jnp.sqrt(_seg_sum(kk * kk, e_ref, et_ref))
    kappa = kk / jnp.maximum(nrm, 1e-12)
    k2 = k * (1.0 + (a - 1.0) * kaw_ref[...])
    bvec = kappa * a
    bonus = _seg_sum(r * k2 * rkw_ref[...], e_ref, et_ref) * v

    ti = lax.broadcasted_iota(jnp.int32, (Tb, Tb), 0)
    tj = lax.broadcasted_iota(jnp.int32, (Tb, Tb), 1)
    tril = jnp.where((tj <= ti) & ((ti >> log2l) == (tj >> log2l)), 1.0, 0.0).astype(BF16)
    cum = _dot_exact_lhs(tril, logw)
    trow = lax.broadcasted_iota(jnp.int32, (Tb, C_RWKV), 0)
    cum_end = jnp.broadcast_to(cum[L - 1:L, :], (Tb, C_RWKV))
    for s in range(1, n_sub):
        cum_end = jnp.where(trow >= s * L, jnp.broadcast_to(cum[(s + 1) * L - 1:(s + 1) * L, :], (Tb, C_RWKV)),
                            cum_end)
    gam = jnp.exp(cum)
    ginv = jnp.exp(-cum)
    gprev = jnp.exp(cum - logw)
    gtail = jnp.exp(cum_end - cum)

    kt = kappa * gprev
    kinv = k2 * ginv
    binv = bvec * ginv
    rt = r * gam
    khat = k2 * gtail
    bhat = bvec * gtail

    rr = lax.broadcasted_iota(jnp.int32, (GL, GL), 0)
    cc = lax.broadcasted_iota(jnp.int32, (GL, GL), 1)
    same = (rr >> log2l) == (cc >> log2l)
    tpos = rr & (L - 1)
    jpos = cc & (L - 1)
    mask_s = same & (jpos < tpos)
    mask_i = same & (jpos <= tpos)
    eye = rr == cc
    srow = lax.broadcasted_iota(jnp.int32, (GL, PACK), 0)
    slane = lax.broadcasted_iota(jnp.int32, (GL, PACK), 1)
    bmask = (srow >> log2l) == (slane >> int(math.log2(HEAD)))
    drow = lax.broadcasted_iota(jnp.int32, (PACK, PACK), 0)
    dcol = lax.broadcasted_iota(jnp.int32, (PACK, PACK), 1)
    deye = drow == dcol

    lane_split = GL % LANES == 0

    def bdot(a_, b_, dims=_BNN):
        return lax.dot_general(a_.astype(BF16), b_.astype(BF16), dims, preferred_element_type=F32)

    def btrans(x):
        return jnp.stack([x[gi].T for gi in range(N_GROUPS)], axis=0)

    def chunk_step(row0, st):
        def stack(x):
            xc = x[row0:row0 + L]
            x3 = jnp.stack([xc[:, gi * PACK:(gi + 1) * PACK] for gi in range(N_GROUPS)], axis=0)
            return jnp.where(bmask[None], jnp.concatenate([x3] * GROUP_HEADS, axis=1), 0.0)

        kt_s = stack(kt)
        rt_s = stack(rt)
        binv_s = stack(binv)
        kinv_s = stack(kinv)
        v_s = stack(v)
        khat_s = stack(khat)
        bhat_s = stack(bhat)

        if lane_split:
            a_all = bdot(jnp.concatenate([kt_s, rt_s], axis=1), jnp.concatenate([binv_s, kinv_s], axis=1), _BNT)
            a_parts = (a_all[:, :GL, :GL], a_all[:, :GL, GL:], a_all[:, GL:, :GL], a_all[:, GL:, GL:])
        else:
            a_parts = (bdot(kt_s, binv_s, _BNT), bdot(kt_s, kinv_s, _BNT),
                       bdot(rt_s, binv_s, _BNT), bdot(rt_s, kinv_s, _BNT))
        n_mat = jnp.where(mask_s[None], a_parts[0], 0.0)
        a_kk = jnp.where(mask_s[None], a_parts[1], 0.0)
        a_br = jnp.where(mask_i[None], a_parts[2], 0.0)
        a_kr = jnp.where(mask_i[None], a_parts[3], 0.0)

        p_mat = -n_mat
        t_mat = jnp.where(eye[None], 1.0, 0.0) + p_mat
        if log2l > 1:
            p_mat = bdot(p_mat, p_mat)
        for lvl in range(1, log2l):
            if lvl == log2l - 1:
                t_mat = t_mat + bdot(p_mat, t_mat)
            elif lane_split:
                both = bdot(p_mat, jnp.concatenate([p_mat, t_mat], axis=2))
                t_mat = t_mat + both[:, :, GL:]
                p_mat = both[:, :, :GL]
            else:
                t_mat = t_mat + bdot(p_mat, t_mat)
                p_mat = bdot(p_mat, p_mat)

        av = bdot(jnp.concatenate([a_kk, a_kr], axis=1), v_s)
        wu = bdot(t_mat, jnp.concatenate([kt_s, av[:, :GL]], axis=2))
        br = bdot(a_br, wu)
        q_s = rt_s - br[:, :, :PACK]
        y0_s = av[:, GL:] - br[:, :, PACK:]
        bhat_t = btrans(bhat_s)
        khat_t = btrans(khat_s)
        gam_end = jnp.exp(cum[row0 + L - 1:row0 + L, :])
        gl3 = jnp.stack([gam_end[:, gi * PACK:(gi + 1) * PACK] for gi in range(N_GROUPS)], axis=0)
        bw = bdot(bhat_t, wu)
        m_mat = jnp.where(deye[None], jnp.broadcast_to(gl3, (N_GROUPS, PACK, PACK)), 0.0) - bw[:, :, :PACK]
        c_mat = bdot(khat_t, v_s) - bw[:, :, PACK:]

        qm = bdot(jnp.concatenate([q_s, m_mat], axis=1), st)
        ys = qm[:, :GL] + y0_s
        yg = ys[:, 0:L]
        for h in range(1, GROUP_HEADS):
            yg = yg + ys[:, h * L:(h + 1) * L]
        return jnp.concatenate([yg[gi] for gi in range(N_GROUPS)], axis=1), qm[:, GL:] + c_mat

    st = st_sc[...]
    y_chunks = []
    for s in range(n_sub):
        y_c, st = chunk_step(s * L, st)
        y_chunks.append(y_c)
    st_sc[...] = st
    y = y_chunks[0] if n_sub == 1 else jnp.concatenate(y_chunks, axis=0)

    inv_head = 1.0 / HEAD
    mu = _seg_sum(y, e_ref, et_ref) * inv_head
    yc = y - mu
    var = _seg_sum(yc * yc, e_ref, et_ref) * inv_head
    yn = yc * lax.rsqrt(var + GN_EPS) * lng_ref[...] + lnb_ref[...]
    y_ref[0] = ((yn + bonus) * g).astype(y_ref.dtype)

    @pl.when(c == n_chunks - 1)
    def _():
        stout_ref[0] = st_sc[...]
        shout_ref[0, :, 0:C_RWKV] = pr_sc[...]
        shout_ref[0, :, C_RWKV:2 * C_RWKV] = pk_sc[...]
        shout_ref[0, :, 2 * C_RWKV:3 * C_RWKV] = pv_sc[...]
        shout_ref[0, :, 3 * C_RWKV:3 * C_RWKV + LORA_PAD] = plo_sc[...]


def _rwkv_mix(proj3, shift_parts, st0, params, chunk, block, cast_arrays=()):
    bsz, t_len, _ = proj3.shape
    L = block
    n_steps = bsz * (t_len // L)
    n_cast = len(cast_arrays)
    rkv_blk0 = 2 * C_CONV // C_RWKV
    lora_blk = (2 * C_CONV + 3 * C_RWKV) // LORA_PAD
    row = lambda n: pl.BlockSpec((1, n), lambda b, c: (0, 0))
    full = lambda s: pl.BlockSpec(s, lambda b, c: tuple(0 for _ in s))
    sh = lambda n: pl.BlockSpec((1, 1, n), lambda b, c: (b, 0, 0))
    in_specs = [
        pl.BlockSpec((1, L, C_RWKV), lambda b, c: (b, c, rkv_blk0)),
        pl.BlockSpec((1, L, C_RWKV), lambda b, c: (b, c, rkv_blk0 + 1)),
        pl.BlockSpec((1, L, C_RWKV), lambda b, c: (b, c, rkv_blk0 + 2)),
        pl.BlockSpec((1, L, LORA_PAD), lambda b, c: (b, c, lora_blk)),
        sh(C_RWKV), sh(C_RWKV), sh(C_RWKV), sh(LORA_PAD),
        pl.BlockSpec((1, N_GROUPS, PACK, PACK), lambda b, c: (b, 0, 0, 0)),
        row(C_RWKV), row(C_RWKV), row(C_RWKV), row(LORA_PAD),
        row(C_RWKV), full((LANES, C_RWKV)), full((LANES, C_RWKV)),
        row(C_RWKV), full((LANES, C_RWKV)), full((LANES, C_RWKV)),
        full((2 * LANES, C_RWKV)), full((2 * LANES, C_RWKV)),
        row(C_RWKV), row(C_RWKV), row(C_RWKV), row(C_RWKV), row(C_RWKV),
        full((C_RWKV, LANES)), full((LANES, C_RWKV)),
    ]
    out_shape = (jax.ShapeDtypeStruct((bsz, t_len, C_RWKV), BF16),
                 jax.ShapeDtypeStruct((bsz, N_GROUPS, PACK, PACK), F32),
                 jax.ShapeDtypeStruct((bsz, 1, 3 * C_RWKV + LORA_PAD), F32))
    out_specs = (pl.BlockSpec((1, L, C_RWKV), lambda b, c: (b, c, 0)),
                 pl.BlockSpec((1, N_GROUPS, PACK, PACK), lambda b, c: (b, 0, 0, 0)),
                 pl.BlockSpec((1, 1, 3 * C_RWKV + LORA_PAD), lambda b, c: (b, 0, 0)))
    steps_per_b = t_len // L
    for arr in cast_arrays:
        rows, width = arr.shape
        win = rows // n_steps
        assert win * n_steps == rows
        spec = pl.BlockSpec((win, width), lambda b, c: (b * steps_per_b + c, 0))
        in_specs = in_specs + [spec]
        out_specs = out_specs + (spec,)
        out_shape = out_shape + (jax.ShapeDtypeStruct((rows, width), BF16),)
    return pl.pallas_call(
        functools.partial(_rwkv_kernel, chunk, n_cast),
        out_shape=out_shape,
        grid=(bsz, t_len // L),
        in_specs=in_specs,
        out_specs=out_specs,
        scratch_shapes=[pltpu.VMEM((N_GROUPS, PACK, PACK), F32),
                        pltpu.VMEM((1, C_RWKV), F32), pltpu.VMEM((1, C_RWKV), F32),
                        pltpu.VMEM((1, C_RWKV), F32), pltpu.VMEM((1, LORA_PAD), F32)],
        compiler_params=pltpu.CompilerParams(dimension_semantics=("arbitrary", "arbitrary")),
        name="rwkv7_mix",
    )(proj3, proj3, proj3, proj3, *shift_parts, st0, *params, *cast_arrays)


def _outproj_kernel(c_ref, y_ref, x_ref, wa_ref, wb_ref, g_ref, b_ref, rwh_ref, rwl_ref, rb_ref,
                    x1_ref, idx_ref, gate_ref):
    mix = (jnp.dot(c_ref[...], wa_ref[...], preferred_element_type=F32)
           + jnp.dot(y_ref[...], wb_ref[...], preferred_element_type=F32))
    h = ALPHA * x_ref[...] + mix
    mu = jnp.mean(h, axis=-1, keepdims=True)
    hc = h - mu
    var = jnp.mean(hc * hc, axis=-1, keepdims=True)
    x1 = hc * lax.rsqrt(var + LN_EPS) * g_ref[...] + b_ref[...]
    x1_ref[...] = x1
    logits = _dot_split_w(x1, rwh_ref[...], rwl_ref[...]) + rb_ref[...]
    lane = lax.broadcasted_iota(jnp.int32, logits.shape, 1)
    idx_out = jnp.zeros(logits.shape, jnp.int32)
    val_out = jnp.zeros(logits.shape, F32)
    vals = []
    for kk in range(TOP_K):
        m = jnp.max(logits, axis=-1, keepdims=True)
        sel = jnp.min(jnp.where(logits == m, lane, ROUTER_PAD), axis=-1, keepdims=True)
        vals.append(m)
        idx_out = jnp.where(lane == kk, sel, idx_out)
        logits = jnp.where(lane == sel, -jnp.inf, logits)
    exps = [jnp.exp(vv - vals[0]) for vv in vals]
    denom = exps[0]
    for ee in exps[1:]:
        denom = denom + ee
    for kk in range(TOP_K):
        val_out = jnp.where(lane == kk, exps[kk] / denom, val_out)
    idx_ref[...] = idx_out
    gate_ref[...] = val_out


def _outproj_into_kernel(base_ref, *refs):
    del base_ref
    _outproj_kernel(*refs)


def _out_proj(c2, y2, x2, wa, wb, ln_g, ln_b, rw_hi, rw_lo, rb, tm, n_total, x1_base=None, row0=0):
    n = x2.shape[0]
    blk0 = row0 // tm
    assert blk0 * tm == row0
    row = lambda w: pl.BlockSpec((1, w), lambda i: (0, 0))
    in_specs = [pl.BlockSpec((tm, C_CONV), lambda i: (i, 0)),
                pl.BlockSpec((tm, C_RWKV), lambda i: (i, 0)),
                pl.BlockSpec((tm, D_MODEL), lambda i: (i, 0)),
                pl.BlockSpec((C_CONV, D_MODEL), lambda i: (0, 0)),
                pl.BlockSpec((C_RWKV, D_MODEL), lambda i: (0, 0)),
                row(D_MODEL), row(D_MODEL),
                pl.BlockSpec((D_MODEL, ROUTER_PAD), lambda i: (0, 0)),
                pl.BlockSpec((D_MODEL, ROUTER_PAD), lambda i: (0, 0)),
                row(ROUTER_PAD)]
    args = (c2, y2, x2, wa, wb, ln_g, ln_b, rw_hi, rw_lo, rb)
    body, aliases = _outproj_kernel, {}
    if x1_base is not None:
        in_specs = [pl.BlockSpec(memory_space=pl.ANY)] + in_specs
        args = (x1_base,) + args
        body, aliases = _outproj_into_kernel, {0: 0}
    return pl.pallas_call(
        body,
        out_shape=(jax.ShapeDtypeStruct((n_total, D_MODEL), F32),
                   jax.ShapeDtypeStruct((n, ROUTER_PAD), jnp.int32),
                   jax.ShapeDtypeStruct((n, ROUTER_PAD), F32)),
        grid=(n // tm,),
        in_specs=in_specs,
        out_specs=(pl.BlockSpec((tm, D_MODEL), lambda i: (i + blk0, 0)),
                   pl.BlockSpec((tm, ROUTER_PAD), lambda i: (i, 0)),
                   pl.BlockSpec((tm, ROUTER_PAD), lambda i: (i, 0))),
        input_output_aliases=aliases,
        compiler_params=pltpu.CompilerParams(dimension_semantics=("arbitrary",)),
        name="out_proj_ln_router",
    )(*args)


DMA_PRIORITIES = 2


def _gather_rows_kernel(nused_ref, tok_ref, x_hbm, o_ref, buf_ref, sem):
    rows = o_ref.shape[0]
    used = pl.program_id(0) < nused_ref[0]

    def row_copy(t, r):
        return pltpu.make_async_copy(x_hbm.at[pl.ds(t, 1)], buf_ref.at[pl.ds(r, 1)], sem)

    @pl.when(used)
    def _():
        def issue(q, carry):
            for p in range(DMA_PRIORITIES):
                r = q * DMA_PRIORITIES + p
                row_copy(tok_ref[0, 0, r], r).start(priority=p)
            return carry

        lax.fori_loop(0, rows // DMA_PRIORITIES, issue, 0, unroll=4)

        def drain(r, carry):
            row_copy(0, r).wait()
            return carry

        lax.fori_loop(0, rows, drain, 0, unroll=8)
        o_ref[...] = buf_ref[...].astype(o_ref.dtype)

    @pl.when(jnp.logical_not(used))
    def _():
        o_ref[...] = jnp.zeros_like(o_ref)


def _gather_rows(n_used, row_tok3, x):
    nb, _, rows = row_tok3.shape
    width = x.shape[1]
    grid_spec = pltpu.PrefetchScalarGridSpec(
        num_scalar_prefetch=1,
        grid=(nb,),
        in_specs=[pl.BlockSpec((1, 1, rows), lambda i, nu: (i, 0, 0), memory_space=pltpu.SMEM),
                  pl.BlockSpec(memory_space=pl.ANY)],
        out_specs=pl.BlockSpec((rows, width), lambda i, nu: (i, 0)),
        scratch_shapes=[pltpu.VMEM((rows, width), x.dtype), pltpu.SemaphoreType.DMA],
    )
    return pl.pallas_call(
        _gather_rows_kernel,
        out_shape=jax.ShapeDtypeStruct((nb * rows, width), BF16),
        grid_spec=grid_spec,
        compiler_params=pltpu.CompilerParams(dimension_semantics=("arbitrary",)),
        name="moe_gather_rows",
    )(n_used, row_tok3, x)


def _expert_kernel(bexp_ref, nused_ref, x_ref, wg_ref, bg_ref, wu_ref, bu_ref, wd_ref, bd_ref, o_ref):
    del bexp_ref
    i = pl.program_id(0)
    j = pl.program_id(1)
    nj = pl.num_programs(1)

    @pl.when(i < nused_ref[0])
    def _():
        @pl.when(j == 0)
        def _():
            o_ref[...] = jnp.broadcast_to(bd_ref[0], o_ref.shape)

        x = x_ref[...]
        gate = jnp.minimum(jnp.dot(x, wg_ref[0], preferred_element_type=F32) + bg_ref[0, pl.ds(j, 1), :],
                           SWIGLU_LIMIT)
        up = jnp.clip(jnp.dot(x, wu_ref[0], preferred_element_type=F32) + bu_ref[0, pl.ds(j, 1), :],
                      -SWIGLU_LIMIT, SWIGLU_LIMIT)
        hmid = (up + 1.0) * gate * jax.nn.sigmoid(SWIGLU_ALPHA * gate)
        o_ref[...] += jnp.dot(hmid.astype(BF16), wd_ref[0], preferred_element_type=F32)

    @pl.when((i >= nused_ref[0]) & (j == nj - 1))
    def _():
        o_ref[...] = jnp.zeros_like(o_ref)


def _experts(block_exp, n_used, xb, wg, bg, wu, bu, wd, bd):
    n_rows = xb.shape[0]
    nb = n_rows // MOE_TM
    nj = D_FF // MOE_TF

    def jsel(i, j, nu):
        return jnp.where(i < nu[0], j, nj - 1)

    grid_spec = pltpu.PrefetchScalarGridSpec(
        num_scalar_prefetch=2,
        grid=(nb, nj),
        in_specs=[pl.BlockSpec((MOE_TM, D_MODEL), lambda i, j, be, nu: (i, 0)),
                  pl.BlockSpec((1, D_MODEL, MOE_TF), lambda i, j, be, nu: (be[i], 0, jsel(i, j, nu))),
                  pl.BlockSpec((1, nj, MOE_TF), lambda i, j, be, nu: (be[i], 0, 0)),
                  pl.BlockSpec((1, D_MODEL, MOE_TF), lambda i, j, be, nu: (be[i], 0, jsel(i, j, nu))),
                  pl.BlockSpec((1, nj, MOE_TF), lambda i, j, be, nu: (be[i], 0, 0)),
                  pl.BlockSpec((1, MOE_TF, D_MODEL), lambda i, j, be, nu: (be[i], jsel(i, j, nu), 0)),
                  pl.BlockSpec((1, 1, D_MODEL), lambda i, j, be, nu: (be[i], 0, 0))],
        out_specs=pl.BlockSpec((MOE_TM, D_MODEL), lambda i, j, be, nu: (i, 0)),
    )
    return pl.pallas_call(
        _expert_kernel,
        out_shape=jax.ShapeDtypeStruct((n_rows, D_MODEL), F32),
        grid_spec=grid_spec,
        compiler_params=pltpu.CompilerParams(dimension_semantics=("arbitrary", "arbitrary")),
        name="moe_experts",
    )(block_exp, n_used, xb, wg, bg.reshape(N_EXPERTS, nj, MOE_TF), wu, bu.reshape(N_EXPERTS, nj, MOE_TF),
      wd, bd.reshape(N_EXPERTS, 1, D_MODEL))


def _combine_kernel(dest_ref, gate_ref, x1_ref, g_ref, b_ref, yb_hbm, o_ref, buf_ref, sem):
    tc = x1_ref.shape[0]

    def issue(r, carry):
        for kk in range(TOP_K):
            d = dest_ref[0, 0, r * TOP_K + kk]
            pltpu.make_async_copy(yb_hbm.at[pl.ds(d, 1)], buf_ref.at[kk, pl.ds(r, 1)], sem).start(
                priority=kk % DMA_PRIORITIES)
        return carry

    lax.fori_loop(0, tc, issue, 0, unroll=2)

    def drain(r, carry):
        for kk in range(TOP_K):
            pltpu.make_async_copy(yb_hbm.at[pl.ds(0, 1)], buf_ref.at[kk, pl.ds(r, 1)], sem).wait()
        return carry

    lax.fori_loop(0, tc, drain, 0, unroll=2)

    gates = gate_ref[...]
    moe = gates[:, 0:1] * buf_ref[0]
    for kk in range(1, TOP_K):
        moe = moe + gates[:, kk:kk + 1] * buf_ref[kk]
    h = ALPHA * x1_ref[...] + moe
    mu = jnp.mean(h, axis=-1, keepdims=True)
    hc = h - mu
    var = jnp.mean(hc * hc, axis=-1, keepdims=True)
    o_ref[...] = hc * lax.rsqrt(var + LN_EPS) * g_ref[...] + b_ref[...]


def _combine(dest3, gates, x1, ln_g, ln_b, yb, tc, row0=0):
    n = dest3.shape[0] * tc
    blk0 = row0 // tc
    assert blk0 * tc == row0
    return pl.pallas_call(
        _combine_kernel,
        out_shape=jax.ShapeDtypeStruct((n, D_MODEL), F32),
        grid=(n // tc,),
        in_specs=[pl.BlockSpec((1, 1, tc * TOP_K), lambda i: (i, 0, 0), memory_space=pltpu.SMEM),
                  pl.BlockSpec((tc, ROUTER_PAD), lambda i: (i, 0)),
                  pl.BlockSpec((tc, D_MODEL), lambda i: (i + blk0, 0)),
                  pl.BlockSpec((1, D_MODEL), lambda i: (0, 0)),
                  pl.BlockSpec((1, D_MODEL), lambda i: (0, 0)),
                  pl.BlockSpec(memory_space=pl.ANY)],
        out_specs=pl.BlockSpec((tc, D_MODEL), lambda i: (i, 0)),
        scratch_shapes=[pltpu.VMEM((TOP_K, tc, D_MODEL), F32), pltpu.SemaphoreType.DMA],
        compiler_params=pltpu.CompilerParams(dimension_semantics=("arbitrary",)),
        name="moe_combine_ln2",
    )(dest3, gates, x1, ln_g, ln_b, yb)


def _pad_cols(w, width):
    return jnp.pad(w, ((0, 0), (0, width - w.shape[1])))


def _pad_rows(w, height):
    return jnp.pad(w, ((0, height - w.shape[0]), (0, 0)))


def _split_lora_cols(w):
    xw = w[..., 0:R_DECAY]
    xa = w[..., R_DECAY:R_DECAY + R_ICLR]
    xg = w[..., R_DECAY + R_ICLR:]
    pad = lambda x, n: jnp.pad(x, [(0, 0)] * (x.ndim - 1) + [(0, n - x.shape[-1])])
    return jnp.concatenate([pad(xw, LANES), pad(xa, LANES), pad(xg, 2 * LANES)], axis=-1)


def _pick(n, prefs):
    for p in prefs:
        if n % p == 0:
            return p
    return n


def _mixer_group(x, conv_buf, shift_buf, wkv_state, wts, cast_along=()):
    bsz, t_len, _ = x.shape
    n = bsz * t_len
    proj = _in_proj(x.reshape(n, D_MODEL), wts["w_in"], wts["b_in"], _pick(n, (1024, 512, 256, 128)), 512)
    proj3 = proj.reshape(bsz, t_len, P_PAD)

    hist = jnp.pad(conv_buf, ((0, 0), (HIST - (CONV_WIDTH - 1), 0), (0, 0)))
    c, tail = _conv_module(proj3, hist, wts["conv_w"], wts["conv_b"], wts["conv_ln_g"], wts["conv_ln_b"],
                           _pick(t_len, (128, 64, 32, 16, 8)))
    new_conv = tail[:, HIST - (CONV_WIDTH - 1):, :]

    sh_rkv = shift_buf[:, :, :3 * C_RWKV]
    sh_lo = _split_lora_cols(shift_buf[:, :, 3 * C_RWKV:])
    shift_parts = (sh_rkv[:, :, 0:C_RWKV], sh_rkv[:, :, C_RWKV:2 * C_RWKV], sh_rkv[:, :, 2 * C_RWKV:], sh_lo)
    st_t = jnp.swapaxes(wkv_state, -1, -2).reshape(bsz, N_GROUPS, GROUP_HEADS, HEAD, HEAD)
    eye_h = jnp.eye(GROUP_HEADS, dtype=F32)
    st0 = jnp.einsum("bghkv,hj->bghkjv", st_t, eye_h).reshape(bsz, N_GROUPS, PACK, PACK)
    chunk = _pick(t_len, (RWKV_CHUNK, 32, 16))
    block = RWKV_BLOCK if t_len % RWKV_BLOCK == 0 else chunk
    n_steps = bsz * (t_len // block)
    ride = tuple(w for w in cast_along if w.shape[0] % n_steps == 0 and (w.shape[0] // n_steps) % 16 == 0)
    yb, st_out, sh_out, *cast_done = _rwkv_mix(proj3, shift_parts, st0, wts["rwkv_params"], chunk, block, ride)
    if len(ride) != len(cast_along):
        cast_done = [w.astype(BF16) for w in cast_along]
    st5 = st_out.reshape(bsz, N_GROUPS, GROUP_HEADS, HEAD, GROUP_HEADS, HEAD)
    st_diag = jnp.einsum("bghkhv->bghkv", st5)
    new_wkv = jnp.swapaxes(st_diag, -1, -2).reshape(bsz, N_HEADS, HEAD, HEAD)
    lo = sh_out[:, :, 3 * C_RWKV:]
    new_shift = jnp.concatenate([sh_out[:, :, :3 * C_RWKV], lo[:, :, 0:R_DECAY], lo[:, :, LANES:LANES + R_ICLR],
                                 lo[:, :, 2 * LANES:2 * LANES + R_GATE]], axis=-1)
    return c.reshape(n, C_CONV), yb.reshape(n, C_RWKV), new_conv, new_shift, new_wkv, cast_done


def _route(top_idx, n_tok):
    n_assign = n_tok * TOP_K
    flat_e = top_idx.reshape(-1)
    onehot = (flat_e[:, None] == jnp.arange(N_EXPERTS, dtype=jnp.int32)[None, :]).astype(jnp.int32)
    csum = jnp.cumsum(onehot, axis=0)
    rank = jnp.take_along_axis(csum, flat_e[:, None], axis=1)[:, 0] - 1
    counts = csum[-1]
    padded = (counts + MOE_TM - 1) // MOE_TM * MOE_TM
    seg_end = jnp.cumsum(padded)
    seg_start = seg_end - padded
    dest = (seg_start[flat_e] + rank).astype(jnp.int32)
    n_rows = (n_assign + N_EXPERTS * (MOE_TM - 1) + MOE_TM - 1) // MOE_TM * MOE_TM
    n_blocks = n_rows // MOE_TM
    row_tok = jnp.zeros((n_rows,), jnp.int32).at[dest].set(jnp.arange(n_assign, dtype=jnp.int32) // TOP_K)
    block_start = jnp.arange(n_blocks, dtype=jnp.int32) * MOE_TM
    block_exp = jnp.minimum(jnp.sum((seg_end[None, :] <= block_start[:, None]).astype(jnp.int32), axis=1),
                            N_EXPERTS - 1).astype(jnp.int32)
    n_used = (seg_end[-1] // MOE_TM).astype(jnp.int32).reshape(1)
    return dest, row_tok, block_exp, n_used, n_blocks


def kernel(x_prompt, x_sample, state_conv, state_shift, state_wkv, w_in, b_in, mu_shift, conv_w, conv_b,
           conv_ln_g, conv_ln_b, rwkv_w0, rwkv_w2, rwkv_a0, rwkv_a2, rwkv_g2, rwkv_k_k, rwkv_k_a, rwkv_r_k,
           rwkv_ln_g, rwkv_ln_b, w_out, ln1_g, ln1_b, router_w, router_b, w_gate, b_gate, w_up, b_up,
           w_down, b_down, ln2_g, ln2_b):
    assert w_in.shape[0] == 1, "single layer"
    d = 0
    row = lambda v: v.reshape(1, -1)
    n_p, t_p, _ = x_prompt.shape
    n_s, t_s, _ = x_sample.shape

    w_rkv = w_in[d][:, 2 * C_CONV:2 * C_CONV + 3 * C_RWKV]
    w_lo = _split_lora_cols(w_in[d][:, 2 * C_CONV + 3 * C_RWKV:])
    w_in_p = jnp.concatenate([w_in[d][:, :2 * C_CONV], w_rkv, w_lo], axis=1).astype(BF16)
    b_in_p = jnp.concatenate([b_in[d][None, :2 * C_CONV], b_in[d][None, 2 * C_CONV:2 * C_CONV + 3 * C_RWKV],
                              _split_lora_cols(b_in[d][None, 2 * C_CONV + 3 * C_RWKV:])], axis=1)
    mu = mu_shift[d][None, :]
    mu_lo = _split_lora_cols(mu[:, 3 * C_RWKV:])
    head_of_lane = jnp.arange(C_RWKV, dtype=jnp.int32) // HEAD
    e_mat = (head_of_lane[:, None] == jnp.arange(LANES, dtype=jnp.int32)[None, :]).astype(BF16)

    def hi_lo(w, height):
        w = _pad_rows(w, height)
        w_hi = w.astype(BF16)
        return w_hi, (w - w_hi.astype(F32)).astype(BF16)

    rwkv_params = (
        mu[:, 0:C_RWKV], mu[:, C_RWKV:2 * C_RWKV], mu[:, 2 * C_RWKV:3 * C_RWKV], mu_lo,
        row(rwkv_w0[d]), *hi_lo(rwkv_w2[d], LANES), row(rwkv_a0[d]), *hi_lo(rwkv_a2[d], LANES),
        *hi_lo(rwkv_g2[d], 2 * LANES),
        row(rwkv_k_k[d]), row(rwkv_k_a[d]), row(rwkv_r_k[d]), row(rwkv_ln_g[d]), row(rwkv_ln_b[d]),
        e_mat, e_mat.T,
    )
    wts = dict(w_in=w_in_p, b_in=b_in_p, conv_w=conv_w[d], conv_b=row(conv_b[d]),
               conv_ln_g=row(conv_ln_g[d]), conv_ln_b=row(conv_ln_b[d]), rwkv_params=rwkv_params)

    zero_conv = jnp.zeros((n_p, CONV_WIDTH - 1, C_CONV), x_prompt.dtype)
    zero_shift = jnp.zeros((n_p, 1, N_SHIFT), x_prompt.dtype)
    zero_wkv = jnp.zeros((n_p, N_HEADS, HEAD, HEAD), state_wkv.dtype)
    expert_w = (w_gate[d].reshape(-1, D_FF), w_up[d].reshape(-1, D_FF), w_down[d].reshape(-1, D_MODEL))
    c_p, y_p, conv_p, shift_p, wkv_p, expert_w = _mixer_group(x_prompt, zero_conv, zero_shift, zero_wkv, wts,
                                                              expert_w)
    wg_b = expert_w[0].reshape(N_EXPERTS, D_MODEL, D_FF)
    wu_b = expert_w[1].reshape(N_EXPERTS, D_MODEL, D_FF)
    wd_b = expert_w[2].reshape(N_EXPERTS, D_FF, D_MODEL)
    c_s, y_s, conv_s, shift_s, wkv_s, _ = _mixer_group(x_sample, state_conv[d], state_shift[d], state_wkv[d], wts)

    w_out_b = w_out[d].astype(BF16)
    rw = _pad_cols(router_w[d], ROUTER_PAD)
    rw_hi = rw.astype(BF16)
    rw_lo = (rw - rw_hi.astype(F32)).astype(BF16)
    rb = jnp.concatenate([router_b[d], jnp.full((ROUTER_PAD - N_EXPERTS,), -jnp.inf, F32)])[None, :]

    n_tok_p = n_p * t_p
    n_tok_s = n_s * t_s
    n_tok = n_tok_p + n_tok_s

    def out_proj(c2, y2, x3, x1_base, row0):
        n = c2.shape[0]
        tm = math.gcd(_pick(n, (OUT_TM, 256, 128)), row0) if row0 else _pick(n, (OUT_TM, 256, 128))
        return _out_proj(c2, y2, x3.reshape(n, D_MODEL), w_out_b[:C_CONV], w_out_b[C_CONV:], row(ln1_g[d]),
                         row(ln1_b[d]), rw_hi, rw_lo, rb, tm, n_tok, x1_base, row0)

    x1, idx_p, gate_p = out_proj(c_p, y_p, x_prompt, None, 0)
    x1, idx_s, gate_s = out_proj(c_s, y_s, x_sample, x1, n_tok_p)
    top_idx = jnp.concatenate([idx_p[:, :TOP_K], idx_s[:, :TOP_K]], axis=0)
    dest, row_tok, block_exp, n_used, n_blocks = _route(top_idx, n_tok)
    xb = _gather_rows(n_used, row_tok.reshape(n_blocks, 1, MOE_TM), x1)
    yb = _experts(block_exp, n_used, xb, wg_b, b_gate[d], wu_b, b_up[d], wd_b, b_down[d])

    def combine(dest_g, gate_g, n, row0):
        tc = _pick(n, (128, 64, 32, 16, 8))
        tc = math.gcd(tc, row0) if row0 else tc
        return _combine(dest_g.reshape(n // tc, 1, tc * TOP_K), gate_g, x1, row(ln2_g[d]), row(ln2_b[d]), yb, tc,
                        row0)

    y_prompt = combine(dest[:n_tok_p * TOP_K], gate_p, n_tok_p, 0).reshape(n_p, t_p, D_MODEL)
    y_sample = combine(dest[n_tok_p * TOP_K:], gate_s, n_tok_s, n_tok_p).reshape(n_s, t_s, D_MODEL)
    return (y_prompt, y_sample, conv_p[None], shift_p[None], wkv_p[None], conv_s[None], shift_s[None], wkv_s[None])
```

```python
import functools
import math

import jax
import jax.numpy as jnp
from jax import lax
from jax.experimental import pallas as pl
from jax.experimental.pallas import tpu as pltpu

F32 = jnp.float32
BF16 = jnp.bfloat16

D_MODEL = 2048
C_CONV = 1024
C_RWKV = 1024
HEAD = 64
N_HEADS = C_RWKV // HEAD
CONV_WIDTH = 31
R_DECAY = 64
R_ICLR = 64
R_GATE = 160
N_SHIFT = 3 * C_RWKV + R_DECAY + R_ICLR + R_GATE
N_EXPERTS = 32
TOP_K = 4
D_FF = 2048
SWIGLU_LIMIT = 7.0
SWIGLU_ALPHA = 1.702
LN_EPS = 1e-5
GN_EPS = 64e-5
ALPHA = 2.0 ** 0.25

LANES = 128
SUBLANES = 8

HIST = 32
LORA_PAD = 512
P_PAD = 2 * C_CONV + 3 * C_RWKV + LORA_PAD
GROUP_HEADS = 2
PACK = GROUP_HEADS * HEAD
N_GROUPS = N_HEADS // GROUP_HEADS
MOE_TM = 512
MOE_NJ = 4
OUT_TM = 512
RWKV_CHUNK = 64
RWKV_BLOCK = 128
ROUTER_PAD = LANES


def _dot(a, b, prec=1, dims=(((1,), (0,)), ((), ()))):
    if prec == 6:
        return lax.dot_general(a.astype(F32), b.astype(F32), dims, precision=lax.Precision.HIGHEST,
                               preferred_element_type=F32)
    d = lambda x, y: lax.dot_general(x, y, dims, preferred_element_type=F32)
    if prec == 1:
        return d(a.astype(BF16), b.astype(BF16))
    a_hi = a.astype(BF16)
    a_lo = (a - a_hi.astype(F32)).astype(BF16)
    b_hi = b.astype(BF16)
    b_lo = (b - b_hi.astype(F32)).astype(BF16)
    return d(a_hi, b_hi) + d(a_hi, b_lo) + d(a_lo, b_hi)


_NT = (((1,), (1,)), ((), ()))
_BNN = (((2,), (1,)), ((0,), (0,)))
_BNT = (((2,), (2,)), ((0,), (0,)))


def _split3(x):
    p1 = x.astype(BF16)
    r1 = x - p1.astype(F32)
    p2 = r1.astype(BF16)
    p3 = (r1 - p2.astype(F32)).astype(BF16)
    return p1, p2, p3


def _dot_exact_rhs(x, m_bf16):
    d = lambda a: jnp.dot(a, m_bf16, preferred_element_type=F32)
    p1, p2, p3 = _split3(x)
    return d(p1) + d(p2) + d(p3)


def _dot_exact_lhs(m_bf16, x):
    d = lambda a: jnp.dot(m_bf16, a, preferred_element_type=F32)
    p1, p2, p3 = _split3(x)
    return d(p1) + d(p2) + d(p3)


def _dot_split_w(x, w_hi, w_lo):
    x_hi = x.astype(BF16)
    x_lo = (x - x_hi.astype(F32)).astype(BF16)
    d = lambda a, b: jnp.dot(a, b, preferred_element_type=F32)
    return d(x_hi, w_hi) + d(x_hi, w_lo) + d(x_lo, w_hi)


def _mm_bias_kernel(x_ref, w_ref, b_ref, o_ref, xb_ref):
    @pl.when(pl.program_id(1) == 0)
    def _():
        xb_ref[...] = x_ref[...].astype(BF16)

    o_ref[...] = jnp.dot(xb_ref[...], w_ref[...], preferred_element_type=F32) + b_ref[...]


def _in_proj(x, w_bf16, b, tm, tn):
    n, k = x.shape
    p = w_bf16.shape[1]
    return pl.pallas_call(
        _mm_bias_kernel,
        out_shape=jax.ShapeDtypeStruct((n, p), F32),
        grid=(n // tm, p // tn),
        in_specs=[pl.BlockSpec((tm, k), lambda i, j: (i, 0)),
                  pl.BlockSpec((k, tn), lambda i, j: (0, j)),
                  pl.BlockSpec((1, tn), lambda i, j: (0, j))],
        out_specs=pl.BlockSpec((tm, tn), lambda i, j: (i, j)),
        scratch_shapes=[pltpu.VMEM((tm, k), BF16)],
        compiler_params=pltpu.CompilerParams(dimension_semantics=("arbitrary", "arbitrary")),
        name="in_proj",
    )(x, w_bf16, b)


def _conv_kernel(val_ref, gate_ref, hist_ref, w_ref, cb_ref, g_ref, b_ref, c_ref, tail_ref, ext_ref, sh_ref):
    t = pl.program_id(1)
    tt = val_ref.shape[1]

    @pl.when(t == 0)
    def _():
        ext_ref[0:HIST, :] = hist_ref[0]

    u = val_ref[0] * jax.nn.sigmoid(gate_ref[0])
    ext_ref[HIST:HIST + tt, :] = u
    span = tt + HIST - SUBLANES
    for s in range(1, SUBLANES):
        sh_ref[s, 0:span, :] = ext_ref[s:s + span, :]
    off = HIST - (CONV_WIDTH - 1)
    acc = jnp.broadcast_to(cb_ref[...], (tt, C_CONV))
    for j in range(CONV_WIDTH):
        base = (off + j) // SUBLANES * SUBLANES
        s = (off + j) % SUBLANES
        src = ext_ref[base:base + tt, :] if s == 0 else sh_ref[s, base:base + tt, :]
        acc = acc + w_ref[j:j + 1, :] * src
    mu = jnp.mean(acc, axis=-1, keepdims=True)
    xc = acc - mu
    var = jnp.mean(xc * xc, axis=-1, keepdims=True)
    y = xc * lax.rsqrt(var + LN_EPS) * g_ref[...] + b_ref[...]
    c_ref[0] = (y * jax.nn.sigmoid(y)).astype(c_ref.dtype)
    tail = ext_ref[tt:tt + HIST, :]
    ext_ref[0:HIST, :] = tail
    tail_ref[0] = tail


def _conv_module(proj3, hist, conv_w, conv_b, ln_g, ln_b, tt):
    bsz, t_len, _ = proj3.shape
    nblk = C_CONV // C_CONV
    return pl.pallas_call(
        _conv_kernel,
        out_shape=(jax.ShapeDtypeStruct((bsz, t_len, C_CONV), BF16),
                   jax.ShapeDtypeStruct((bsz, HIST, C_CONV), F32)),
        grid=(bsz, t_len // tt),
        in_specs=[pl.BlockSpec((1, tt, C_CONV), lambda b, t: (b, t, 0)),
                  pl.BlockSpec((1, tt, C_CONV), lambda b, t: (b, t, nblk)),
                  pl.BlockSpec((1, HIST, C_CONV), lambda b, t: (b, 0, 0)),
                  pl.BlockSpec((CONV_WIDTH, C_CONV), lambda b, t: (0, 0)),
                  pl.BlockSpec((1, C_CONV), lambda b, t: (0, 0)),
                  pl.BlockSpec((1, C_CONV), lambda b, t: (0, 0)),
                  pl.BlockSpec((1, C_CONV), lambda b, t: (0, 0))],
        out_specs=(pl.BlockSpec((1, tt, C_CONV), lambda b, t: (b, t, 0)),
                   pl.BlockSpec((1, HIST, C_CONV), lambda b, t: (b, 0, 0))),
        scratch_shapes=[pltpu.VMEM((HIST + tt, C_CONV), F32),
                        pltpu.VMEM((SUBLANES, HIST + tt, C_CONV), F32)],
        compiler_params=pltpu.CompilerParams(dimension_semantics=("arbitrary", "arbitrary")),
        name="conv_module",
    )(proj3, proj3, hist, conv_w, conv_b, ln_g, ln_b)


PREC_CHUNK = 1
PREC_STATE = 1


def _seg_sum(x, e_ref, et_ref):
    return _dot_exact_rhs(_dot_exact_rhs(x, e_ref[...]), et_ref[...])


def _rwkv_kernel(chunk, n_cast, *refs):
    n_in = 28
    (r_ref, k_ref, v_ref, lo_ref, shr_ref, shk_ref, shv_ref, shlo_ref, st0_ref,
     mur_ref, muk_ref, muv_ref, mulo_ref, w0_ref, w2h_ref, w2l_ref, a0_ref, a2h_ref, a2l_ref,
     g2h_ref, g2l_ref, kkw_ref, kaw_ref, rkw_ref, lng_ref, lnb_ref, e_ref, et_ref) = refs[:n_in]
    cast_in = refs[n_in:n_in + n_cast]
    y_ref, stout_ref, shout_ref = refs[n_in + n_cast:n_in + n_cast + 3]
    cast_out = refs[n_in + n_cast + 3:n_in + 2 * n_cast + 3]
    st_sc, pr_sc, pk_sc, pv_sc, plo_sc = refs[n_in + 2 * n_cast + 3:]

    for src, dst in zip(cast_in, cast_out):
        dst[...] = src[...].astype(dst.dtype)

    c = pl.program_id(1)
    n_chunks = pl.num_programs(1)
    Tb = r_ref.shape[1]
    L = chunk
    n_sub = Tb // L
    GL = GROUP_HEADS * L
    log2l = int(math.log2(L))

    @pl.when(c == 0)
    def _():
        st_sc[...] = st0_ref[0]
        pr_sc[...] = shr_ref[0]
        pk_sc[...] = shk_ref[0]
        pv_sc[...] = shv_ref[0]
        plo_sc[...] = shlo_ref[0]

    def token_shift(x_ref, prev_sc, mu_ref):
        x = x_ref[0]
        row = lax.broadcasted_iota(jnp.int32, x.shape, 0)
        xprev = jnp.where(row == 0, jnp.broadcast_to(prev_sc[...], x.shape), pltpu.roll(x, 1, 0))
        prev_sc[...] = x[Tb - 1:Tb, :]
        return x + mu_ref[...] * (xprev - x)

    r = token_shift(r_ref, pr_sc, mur_ref)
    k = token_shift(k_ref, pk_sc, muk_ref)
    v = token_shift(v_ref, pv_sc, muv_ref)
    lo = token_shift(lo_ref, plo_sc, mulo_ref)
    xw = lo[:, 0:LANES]
    xa = lo[:, LANES:2 * LANES]
    xg = lo[:, 2 * LANES:LORA_PAD]

    u_dec = w0_ref[...] + _dot_split_w(jnp.tanh(xw), w2h_ref[...], w2l_ref[...])
    logw = (-math.exp(-0.5)) * jax.nn.sigmoid(u_dec)
    a = jax.nn.sigmoid(a0_ref[...] + _dot_split_w(xa, a2h_ref[...], a2l_ref[...]))
    g = _dot_split_w(jax.nn.sigmoid(xg), g2h_ref[...], g2l_ref[...])

    kk = k * kkw_ref[...]
    nrm = jnp.sqrt(_seg_sum(kk * kk, e_ref, et_ref))
    kappa = kk / jnp.maximum(nrm, 1e-12)
    k2 = k * (1.0 + (a - 1.0) * kaw_ref[...])
    bvec = kappa * a
    bonus = _seg_sum(r * k2 * rkw_ref[...], e_ref, et_ref) * v

    ti = lax.broadcasted_iota(jnp.int32, (Tb, Tb), 0)
    tj = lax.broadcasted_iota(jnp.int32, (Tb, Tb), 1)
    tril = jnp.where((tj <= ti) & ((ti >> log2l) == (tj >> log2l)), 1.0, 0.0).astype(BF16)
    cum = _dot_exact_lhs(tril, logw)
    trow = lax.broadcasted_iota(jnp.int32, (Tb, C_RWKV), 0)
    cum_end = jnp.broadcast_to(cum[L - 1:L, :], (Tb, C_RWKV))
    for s in range(1, n_sub):
        cum_end = jnp.where(trow >= s * L, jnp.broadcast_to(cum[(s + 1) * L - 1:(s + 1) * L, :], (Tb, C_RWKV)),
                            cum_end)
    gam = jnp.exp(cum)
    ginv = jnp.exp(-cum)
    gprev = jnp.exp(cum - logw)
    gtail = jnp.exp(cum_end - cum)

    kt = kappa * gprev
    kinv = k2 * ginv
    binv = bvec * ginv
    rt = r * gam
    khat = k2 * gtail
    bhat = bvec * gtail

    rr = lax.broadcasted_iota(jnp.int32, (GL, GL), 0)
    cc = lax.broadcasted_iota(jnp.int32, (GL, GL), 1)
    same = (rr >> log2l) == (cc >> log2l)
    tpos = rr & (L - 1)
    jpos = cc & (L - 1)
    mask_s = same & (jpos < tpos)
    mask_i = same & (jpos <= tpos)
    eye = rr == cc
    srow = lax.broadcasted_iota(jnp.int32, (GL, PACK), 0)
    slane = lax.broadcasted_iota(jnp.int32, (GL, PACK), 1)
    bmask = (srow >> log2l) == (slane >> int(math.log2(HEAD)))
    drow = lax.broadcasted_iota(jnp.int32, (PACK, PACK), 0)
    dcol = lax.broadcasted_iota(jnp.int32, (PACK, PACK), 1)
    deye = drow == dcol

    lane_split = GL % LANES == 0

    def bdot(a_, b_, dims=_BNN):
        return lax.dot_general(a_.astype(BF16), b_.astype(BF16), dims, preferred_element_type=F32)

    def btrans(x):
        return jnp.stack([x[gi].T for gi in range(N_GROUPS)], axis=0)

    def chunk_step(row0, st):
        def stack(x):
            xc = x[row0:row0 + L]
            x3 = jnp.stack([xc[:, gi * PACK:(gi + 1) * PACK] for gi in range(N_GROUPS)], axis=0)
            return jnp.where(bmask[None], jnp.concatenate([x3] * GROUP_HEADS, axis=1), 0.0)

        kt_s = stack(kt)
        rt_s = stack(rt)
        binv_s = stack(binv)
        kinv_s = stack(kinv)
        v_s = stack(v)
        khat_s = stack(khat)
        bhat_s = stack(bhat)

        if lane_split:
            a_all = bdot(jnp.concatenate([kt_s, rt_s], axis=1), jnp.concatenate([binv_s, kinv_s], axis=1), _BNT)
            a_parts = (a_all[:, :GL, :GL], a_all[:, :GL, GL:], a_all[:, GL:, :GL], a_all[:, GL:, GL:])
        else:
            a_parts = (bdot(kt_s, binv_s, _BNT), bdot(kt_s, kinv_s, _BNT),
                       bdot(rt_s, binv_s, _BNT), bdot(rt_s, kinv_s, _BNT))
        n_mat = jnp.where(mask_s[None], a_parts[0], 0.0)
        a_kk = jnp.where(mask_s[None], a_parts[1], 0.0)
        a_br = jnp.where(mask_i[None], a_parts[2], 0.0)
        a_kr = jnp.where(mask_i[None], a_parts[3], 0.0)

        p_mat = -n_mat
        t_mat = jnp.where(eye[None], 1.0, 0.0) + p_mat
        if log2l > 1:
            p_mat = bdot(p_mat, p_mat)
        for lvl in range(1, log2l):
            if lvl == log2l - 1:
                t_mat = t_mat + bdot(p_mat, t_mat)
            elif lane_split:
                both = bdot(p_mat, jnp.concatenate([p_mat, t_mat], axis=2))
                t_mat = t_mat + both[:, :, GL:]
                p_mat = both[:, :, :GL]
            else:
                t_mat = t_mat + bdot(p_mat, t_mat)
                p_mat = bdot(p_mat, p_mat)

        av = bdot(jnp.concatenate([a_kk, a_kr], axis=1), v_s)
        wu = bdot(t_mat, jnp.concatenate([kt_s, av[:, :GL]], axis=2))
        br = bdot(a_br, wu)
        q_s = rt_s - br[:, :, :PACK]
        y0_s = av[:, GL:] - br[:, :, PACK:]
        bhat_t = btrans(bhat_s)
        khat_t = btrans(khat_s)
        gam_end = jnp.exp(cum[row0 + L - 1:row0 + L, :])
        gl3 = jnp.stack([gam_end[:, gi * PACK:(gi + 1) * PACK] for gi in range(N_GROUPS)], axis=0)
        bw = bdot(bhat_t, wu)
        m_mat = jnp.where(deye[None], jnp.broadcast_to(gl3, (N_GROUPS, PACK, PACK)), 0.0) - bw[:, :, :PACK]
        c_mat = bdot(khat_t, v_s) - bw[:, :, PACK:]

        qm = bdot(jnp.concatenate([q_s, m_mat], axis=1), st)
        ys = qm[:, :GL] + y0_s
        yg = ys[:, 0:L]
        for h in range(1, GROUP_HEADS):
            yg = yg + ys[:, h * L:(h + 1) * L]
        return jnp.concatenate([yg[gi] for gi in range(N_GROUPS)], axis=1), qm[:, GL:] + c_mat

    st = st_sc[...]
    y_chunks = []
    for s in range(n_sub):
        y_c, st = chunk_step(s * L, st)
        y_chunks.append(y_c)
    st_sc[...] = st
    y = y_chunks[0] if n_sub == 1 else jnp.concatenate(y_chunks, axis=0)

    inv_head = 1.0 / HEAD
    mu = _seg_sum(y, e_ref, et_ref) * inv_head
    yc = y - mu
    var = _seg_sum(yc * yc, e_ref, et_ref) * inv_head
    yn = yc * lax.rsqrt(var + GN_EPS) * lng_ref[...] + lnb_ref[...]
    y_ref[0] = ((yn + bonus) * g).astype(y_ref.dtype)

    @pl.when(c == n_chunks - 1)
    def _():
        stout_ref[0] = st_sc[...]
        shout_ref[0, :, 0:C_RWKV] = pr_sc[...]
        shout_ref[0, :, C_RWKV:2 * C_RWKV] = pk_sc[...]
        shout_ref[0, :, 2 * C_RWKV:3 * C_RWKV] = pv_sc[...]
        shout_ref[0, :, 3 * C_RWKV:3 * C_RWKV + LORA_PAD] = plo_sc[...]


def _rwkv_mix(proj3, shift_parts, st0, params, chunk, block, cast_arrays=()):
    bsz, t_len, _ = proj3.shape
    L = block
    n_steps = bsz * (t_len // L)
    n_cast = len(cast_arrays)
    rkv_blk0 = 2 * C_CONV // C_RWKV
    lora_blk = (2 * C_CONV + 3 * C_RWKV) // LORA_PAD
    row = lambda n: pl.BlockSpec((1, n), lambda b, c: (0, 0))
    full = lambda s: pl.BlockSpec(s, lambda b, c: tuple(0 for _ in s))
    sh = lambda n: pl.BlockSpec((1, 1, n), lambda b, c: (b, 0, 0))
    in_specs = [
        pl.BlockSpec((1, L, C_RWKV), lambda b, c: (b, c, rkv_blk0)),
        pl.BlockSpec((1, L, C_RWKV), lambda b, c: (b, c, rkv_blk0 + 1)),
        pl.BlockSpec((1, L, C_RWKV), lambda b, c: (b, c, rkv_blk0 + 2)),
        pl.BlockSpec((1, L, LORA_PAD), lambda b, c: (b, c, lora_blk)),
        sh(C_RWKV), sh(C_RWKV), sh(C_RWKV), sh(LORA_PAD),
        pl.BlockSpec((1, N_GROUPS, PACK, PACK), lambda b, c: (b, 0, 0, 0)),
        row(C_RWKV), row(C_RWKV), row(C_RWKV), row(LORA_PAD),
        row(C_RWKV), full((LANES, C_RWKV)), full((LANES, C_RWKV)),
        row(C_RWKV), full((LANES, C_RWKV)), full((LANES, C_RWKV)),
        full((2 * LANES, C_RWKV)), full((2 * LANES, C_RWKV)),
        row(C_RWKV), row(C_RWKV), row(C_RWKV), row(C_RWKV), row(C_RWKV),
        full((C_RWKV, LANES)), full((LANES, C_RWKV)),
    ]
    out_shape = (jax.ShapeDtypeStruct((bsz, t_len, C_RWKV), BF16),
                 jax.ShapeDtypeStruct((bsz, N_GROUPS, PACK, PACK), F32),
                 jax.ShapeDtypeStruct((bsz, 1, 3 * C_RWKV + LORA_PAD), F32))
    out_specs = (pl.BlockSpec((1, L, C_RWKV), lambda b, c: (b, c, 0)),
                 pl.BlockSpec((1, N_GROUPS, PACK, PACK), lambda b, c: (b, 0, 0, 0)),
                 pl.BlockSpec((1, 1, 3 * C_RWKV + LORA_PAD), lambda b, c: (b, 0, 0)))
    steps_per_b = t_len // L
    for arr in cast_arrays:
        rows, width = arr.shape
        win = rows // n_steps
        assert win * n_steps == rows
        spec = pl.BlockSpec((win, width), lambda b, c: (b * steps_per_b + c, 0))
        in_specs = in_specs + [spec]
        out_specs = out_specs + (spec,)
        out_shape = out_shape + (jax.ShapeDtypeStruct((rows, width), BF16),)
    return pl.pallas_call(
        functools.partial(_rwkv_kernel, chunk, n_cast),
        out_shape=out_shape,
        grid=(bsz, t_len // L),
        in_specs=in_specs,
        out_specs=out_specs,
        scratch_shapes=[pltpu.VMEM((N_GROUPS, PACK, PACK), F32),
                        pltpu.VMEM((1, C_RWKV), F32), pltpu.VMEM((1, C_RWKV), F32),
                        pltpu.VMEM((1, C_RWKV), F32), pltpu.VMEM((1, LORA_PAD), F32)],
        compiler_params=pltpu.CompilerParams(dimension_semantics=("arbitrary", "arbitrary")),
        name="rwkv7_mix",
    )(proj3, proj3, proj3, proj3, *shift_parts, st0, *params, *cast_arrays)


def _outproj_kernel(c_ref, y_ref, x_ref, wa_ref, wb_ref, g_ref, b_ref, rwh_ref, rwl_ref, rb_ref,
                    x1_ref, idx_ref, gate_ref):
    mix = (jnp.dot(c_ref[...], wa_ref[...], preferred_element_type=F32)
           + jnp.dot(y_ref[...], wb_ref[...], preferred_element_type=F32))
    h = ALPHA * x_ref[...] + mix
    mu = jnp.mean(h, axis=-1, keepdims=True)
    hc = h - mu
    var = jnp.mean(hc * hc, axis=-1, keepdims=True)
    x1 = hc * lax.rsqrt(var + LN_EPS) * g_ref[...] + b_ref[...]
    x1_ref[...] = x1
    logits = _dot_split_w(x1, rwh_ref[...], rwl_ref[...]) + rb_ref[...]
    lane = lax.broadcasted_iota(jnp.int32, logits.shape, 1)
    idx_out = jnp.zeros(logits.shape, jnp.int32)
    val_out = jnp.zeros(logits.shape, F32)
    vals = []
    for kk in range(TOP_K):
        m = jnp.max(logits, axis=-1, keepdims=True)
        sel = jnp.min(jnp.where(logits == m, lane, ROUTER_PAD), axis=-1, keepdims=True)
        vals.append(m)
        idx_out = jnp.where(lane == kk, sel, idx_out)
        logits = jnp.where(lane == sel, -jnp.inf, logits)
    exps = [jnp.exp(vv - vals[0]) for vv in vals]
    denom = exps[0]
    for ee in exps[1:]:
        denom = denom + ee
    for kk in range(TOP_K):
        val_out = jnp.where(lane == kk, exps[kk] / denom, val_out)
    idx_ref[...] = idx_out
    gate_ref[...] = val_out


def _outproj_into_kernel(base_ref, *refs):
    del base_ref
    _outproj_kernel(*refs)


def _out_proj(c2, y2, x2, wa, wb, ln_g, ln_b, rw_hi, rw_lo, rb, tm, n_total, x1_base=None, row0=0):
    n = x2.shape[0]
    blk0 = row0 // tm
    assert blk0 * tm == row0
    row = lambda w: pl.BlockSpec((1, w), lambda i: (0, 0))
    in_specs = [pl.BlockSpec((tm, C_CONV), lambda i: (i, 0)),
                pl.BlockSpec((tm, C_RWKV), lambda i: (i, 0)),
                pl.BlockSpec((tm, D_MODEL), lambda i: (i, 0)),
                pl.BlockSpec((C_CONV, D_MODEL), lambda i: (0, 0)),
                pl.BlockSpec((C_RWKV, D_MODEL), lambda i: (0, 0)),
                row(D_MODEL), row(D_MODEL),
                pl.BlockSpec((D_MODEL, ROUTER_PAD), lambda i: (0, 0)),
                pl.BlockSpec((D_MODEL, ROUTER_PAD), lambda i: (0, 0)),
                row(ROUTER_PAD)]
    args = (c2, y2, x2, wa, wb, ln_g, ln_b, rw_hi, rw_lo, rb)
    body, aliases = _outproj_kernel, {}
    if x1_base is not None:
        in_specs = [pl.BlockSpec(memory_space=pl.ANY)] + in_specs
        args = (x1_base,) + args
        body, aliases = _outproj_into_kernel, {0: 0}
    return pl.pallas_call(
        body,
        out_shape=(jax.ShapeDtypeStruct((n_total, D_MODEL), F32),
                   jax.ShapeDtypeStruct((n, ROUTER_PAD), jnp.int32),
                   jax.ShapeDtypeStruct((n, ROUTER_PAD), F32)),
        grid=(n // tm,),
        in_specs=in_specs,
        out_specs=(pl.BlockSpec((tm, D_MODEL), lambda i: (i + blk0, 0)),
                   pl.BlockSpec((tm, ROUTER_PAD), lambda i: (i, 0)),
                   pl.BlockSpec((tm, ROUTER_PAD), lambda i: (i, 0))),
        input_output_aliases=aliases,
        compiler_params=pltpu.CompilerParams(dimension_semantics=("arbitrary",)),
        name="out_proj_ln_router",
    )(*args)


DMA_PRIORITIES = 2


def _expert_kernel(bexp_ref, nused_ref, tokc_ref, tokn_ref, x_hbm, wg_ref, bg_ref, wu_ref, bu_ref, wd_ref, bd_ref,
                   o_ref, xg_ref, xb_ref, sems):
    del bexp_ref
    i = pl.program_id(0)
    j = pl.program_id(1)
    nj = pl.num_programs(1)
    n_used = nused_ref[0]
    rows = xb_ref.shape[0]
    slot = lax.rem(i, 2)

    def row_copy(tok_ref, s, r):
        return pltpu.make_async_copy(x_hbm.at[pl.ds(tok_ref[0, 0, r], 1)], xg_ref.at[s, pl.ds(r, 1)], sems.at[s])

    def issue(tok_ref, s, r0, n):
        def body(q, carry):
            for p in range(DMA_PRIORITIES):
                row_copy(tok_ref, s, r0 + q * DMA_PRIORITIES + p).start(priority=p)
            return carry

        lax.fori_loop(0, n // DMA_PRIORITIES, body, 0, unroll=4)

    @pl.when((i == 0) & (j == 0) & (n_used > 0))
    def _():
        issue(tokc_ref, 0, 0, rows)

    @pl.when(i + 1 < n_used)
    def _():
        per_step = rows // MOE_NJ
        issue(tokn_ref, 1 - slot, j * per_step, per_step)

    @pl.when(i < n_used)
    def _():
        @pl.when(j == 0)
        def _():
            def drain(r, carry):
                row_copy(tokc_ref, slot, r).wait()
                return carry

            lax.fori_loop(0, rows, drain, 0, unroll=8)
            xb_ref[...] = xg_ref[slot].astype(BF16)
            o_ref[...] = jnp.broadcast_to(bd_ref[0], o_ref.shape)

        x = xb_ref[...]
        gate = jnp.minimum(jnp.dot(x, wg_ref[0], preferred_element_type=F32) + bg_ref[0, pl.ds(j, 1), :],
                           SWIGLU_LIMIT)
        up = jnp.clip(jnp.dot(x, wu_ref[0], preferred_element_type=F32) + bu_ref[0, pl.ds(j, 1), :],
                      -SWIGLU_LIMIT, SWIGLU_LIMIT)
        hmid = (up + 1.0) * gate * jax.nn.sigmoid(SWIGLU_ALPHA * gate)
        o_ref[...] += jnp.dot(hmid.astype(BF16), wd_ref[0], preferred_element_type=F32)

    @pl.when((i >= nused_ref[0]) & (j == nj - 1))
    def _():
        o_ref[...] = jnp.zeros_like(o_ref)


def _experts(block_exp, n_used, row_tok3, x, wg, bg, wu, bu, wd, bd):
    nb, _, rows = row_tok3.shape
    n_exp = wg.shape[0]
    nj = MOE_NJ
    tf = D_FF // nj

    def jsel(i, j, nu):
        return jnp.where(i < nu[0], j, nj - 1)

    tok_spec = lambda off: pl.BlockSpec((1, 1, rows), lambda i, j, be, nu: (jnp.minimum(i + off, nb - 1), 0, 0),
                                        memory_space=pltpu.SMEM)
    grid_spec = pltpu.PrefetchScalarGridSpec(
        num_scalar_prefetch=2,
        grid=(nb, nj),
        in_specs=[tok_spec(0), tok_spec(1),
                  pl.BlockSpec(memory_space=pl.ANY),
                  pl.BlockSpec((1, D_MODEL, tf), lambda i, j, be, nu: (be[i], 0, jsel(i, j, nu))),
                  pl.BlockSpec((1, nj, tf), lambda i, j, be, nu: (be[i], 0, 0)),
                  pl.BlockSpec((1, D_MODEL, tf), lambda i, j, be, nu: (be[i], 0, jsel(i, j, nu))),
                  pl.BlockSpec((1, nj, tf), lambda i, j, be, nu: (be[i], 0, 0)),
                  pl.BlockSpec((1, tf, D_MODEL), lambda i, j, be, nu: (be[i], jsel(i, j, nu), 0)),
                  pl.BlockSpec((1, 1, D_MODEL), lambda i, j, be, nu: (be[i], 0, 0))],
        out_specs=pl.BlockSpec((rows, D_MODEL), lambda i, j, be, nu: (i, 0)),
        scratch_shapes=[pltpu.VMEM((2, rows, D_MODEL), x.dtype), pltpu.VMEM((rows, D_MODEL), BF16),
                        pltpu.SemaphoreType.DMA((2,))],
    )
    return pl.pallas_call(
        _expert_kernel,
        out_shape=jax.ShapeDtypeStruct((nb * rows, D_MODEL), F32),
        grid_spec=grid_spec,
        compiler_params=pltpu.CompilerParams(dimension_semantics=("arbitrary", "arbitrary")),
        name="moe_experts",
    )(block_exp, n_used, row_tok3, row_tok3, x, wg, bg.reshape(n_exp, nj, tf), wu, bu.reshape(n_exp, nj, tf),
      wd, bd.reshape(n_exp, 1, D_MODEL))


def _combine_kernel(dest_ref, gate_ref, x1_ref, g_ref, b_ref, yb_hbm, o_ref, buf_ref, sem):
    tc = x1_ref.shape[0]

    def issue(r, carry):
        for kk in range(TOP_K):
            d = dest_ref[0, 0, r * TOP_K + kk]
            pltpu.make_async_copy(yb_hbm.at[pl.ds(d, 1)], buf_ref.at[kk, pl.ds(r, 1)], sem).start(
                priority=kk % DMA_PRIORITIES)
        return carry

    lax.fori_loop(0, tc, issue, 0, unroll=2)

    def drain(r, carry):
        for kk in range(TOP_K):
            pltpu.make_async_copy(yb_hbm.at[pl.ds(0, 1)], buf_ref.at[kk, pl.ds(r, 1)], sem).wait()
        return carry

    lax.fori_loop(0, tc, drain, 0, unroll=2)

    gates = gate_ref[...]
    moe = gates[:, 0:1] * buf_ref[0]
    for kk in range(1, TOP_K):
        moe = moe + gates[:, kk:kk + 1] * buf_ref[kk]
    h = ALPHA * x1_ref[...] + moe
    mu = jnp.mean(h, axis=-1, keepdims=True)
    hc = h - mu
    var = jnp.mean(hc * hc, axis=-1, keepdims=True)
    o_ref[...] = hc * lax.rsqrt(var + LN_EPS) * g_ref[...] + b_ref[...]


def _combine(dest3, gates, x1, ln_g, ln_b, yb, tc, row0=0):
    n = dest3.shape[0] * tc
    blk0 = row0 // tc
    assert blk0 * tc == row0
    return pl.pallas_call(
        _combine_kernel,
        out_shape=jax.ShapeDtypeStruct((n, D_MODEL), F32),
        grid=(n // tc,),
        in_specs=[pl.BlockSpec((1, 1, tc * TOP_K), lambda i: (i, 0, 0), memory_space=pltpu.SMEM),
                  pl.BlockSpec((tc, ROUTER_PAD), lambda i: (i, 0)),
                  pl.BlockSpec((tc, D_MODEL), lambda i: (i + blk0, 0)),
                  pl.BlockSpec((1, D_MODEL), lambda i: (0, 0)),
                  pl.BlockSpec((1, D_MODEL), lambda i: (0, 0)),
                  pl.BlockSpec(memory_space=pl.ANY)],
        out_specs=pl.BlockSpec((tc, D_MODEL), lambda i: (i, 0)),
        scratch_shapes=[pltpu.VMEM((TOP_K, tc, D_MODEL), F32), pltpu.SemaphoreType.DMA],
        compiler_params=pltpu.CompilerParams(dimension_semantics=("arbitrary",)),
        name="moe_combine_ln2",
    )(dest3, gates, x1, ln_g, ln_b, yb)


def _pad_cols(w, width):
    return jnp.pad(w, ((0, 0), (0, width - w.shape[1])))


def _pad_rows(w, height):
    return jnp.pad(w, ((0, height - w.shape[0]), (0, 0)))


def _split_lora_cols(w):
    xw = w[..., 0:R_DECAY]
    xa = w[..., R_DECAY:R_DECAY + R_ICLR]
    xg = w[..., R_DECAY + R_ICLR:]
    pad = lambda x, n: jnp.pad(x, [(0, 0)] * (x.ndim - 1) + [(0, n - x.shape[-1])])
    return jnp.concatenate([pad(xw, LANES), pad(xa, LANES), pad(xg, 2 * LANES)], axis=-1)


def _pick(n, prefs):
    for p in prefs:
        if n % p == 0:
            return p
    return n


def _mixer_group(x, conv_buf, shift_buf, wkv_state, wts, cast_along=()):
    bsz, t_len, _ = x.shape
    n = bsz * t_len
    proj = _in_proj(x.reshape(n, D_MODEL), wts["w_in"], wts["b_in"], _pick(n, (1024, 512, 256, 128)), 512)
    proj3 = proj.reshape(bsz, t_len, P_PAD)

    hist = jnp.pad(conv_buf, ((0, 0), (HIST - (CONV_WIDTH - 1), 0), (0, 0)))
    c, tail = _conv_module(proj3, hist, wts["conv_w"], wts["conv_b"], wts["conv_ln_g"], wts["conv_ln_b"],
                           _pick(t_len, (128, 64, 32, 16, 8)))
    new_conv = tail[:, HIST - (CONV_WIDTH - 1):, :]

    sh_rkv = shift_buf[:, :, :3 * C_RWKV]
    sh_lo = _split_lora_cols(shift_buf[:, :, 3 * C_RWKV:])
    shift_parts = (sh_rkv[:, :, 0:C_RWKV], sh_rkv[:, :, C_RWKV:2 * C_RWKV], sh_rkv[:, :, 2 * C_RWKV:], sh_lo)
    st_t = jnp.swapaxes(wkv_state, -1, -2).reshape(bsz, N_GROUPS, GROUP_HEADS, HEAD, HEAD)
    eye_h = jnp.eye(GROUP_HEADS, dtype=F32)
    st0 = jnp.einsum("bghkv,hj->bghkjv", st_t, eye_h).reshape(bsz, N_GROUPS, PACK, PACK)
    chunk = _pick(t_len, (RWKV_CHUNK, 32, 16))
    block = RWKV_BLOCK if t_len % RWKV_BLOCK == 0 else chunk
    n_steps = bsz * (t_len // block)
    ride = tuple(w for w in cast_along if w.shape[0] % n_steps == 0 and (w.shape[0] // n_steps) % 16 == 0)
    yb, st_out, sh_out, *cast_done = _rwkv_mix(proj3, shift_parts, st0, wts["rwkv_params"], chunk, block, ride)
    if len(ride) != len(cast_along):
        cast_done = [w.astype(BF16) for w in cast_along]
    st5 = st_out.reshape(bsz, N_GROUPS, GROUP_HEADS, HEAD, GROUP_HEADS, HEAD)
    st_diag = jnp.einsum("bghkhv->bghkv", st5)
    new_wkv = jnp.swapaxes(st_diag, -1, -2).reshape(bsz, N_HEADS, HEAD, HEAD)
    lo = sh_out[:, :, 3 * C_RWKV:]
    new_shift = jnp.concatenate([sh_out[:, :, :3 * C_RWKV], lo[:, :, 0:R_DECAY], lo[:, :, LANES:LANES + R_ICLR],
                                 lo[:, :, 2 * LANES:2 * LANES + R_GATE]], axis=-1)
    return c.reshape(n, C_CONV), yb.reshape(n, C_RWKV), new_conv, new_shift, new_wkv, cast_done


def _route(top_idx, n_tok):
    n_assign = n_tok * TOP_K
    flat_e = top_idx.reshape(-1)
    onehot = (flat_e[:, None] == jnp.arange(N_EXPERTS, dtype=jnp.int32)[None, :]).astype(jnp.int32)
    csum = jnp.cumsum(onehot, axis=0)
    rank = jnp.take_along_axis(csum, flat_e[:, None], axis=1)[:, 0] - 1
    counts = csum[-1]
    padded = (counts + MOE_TM - 1) // MOE_TM * MOE_TM
    seg_end = jnp.cumsum(padded)
    seg_start = seg_end - padded
    dest = (seg_start[flat_e] + rank).astype(jnp.int32)
    n_rows = (n_assign + N_EXPERTS * (MOE_TM - 1) + MOE_TM - 1) // MOE_TM * MOE_TM
    n_blocks = n_rows // MOE_TM
    row_tok = jnp.zeros((n_rows,), jnp.int32).at[dest].set(jnp.arange(n_assign, dtype=jnp.int32) // TOP_K)
    block_start = jnp.arange(n_blocks, dtype=jnp.int32) * MOE_TM
    block_exp = jnp.minimum(jnp.sum((seg_end[None, :] <= block_start[:, None]).astype(jnp.int32), axis=1),
                            N_EXPERTS - 1).astype(jnp.int32)
    n_used = (seg_end[-1] // MOE_TM).astype(jnp.int32).reshape(1)
    return dest, row_tok, block_exp, n_used, n_blocks


def kernel(x_prompt, x_sample, state_conv, state_shift, state_wkv, w_in, b_in, mu_shift, conv_w, conv_b,
           conv_ln_g, conv_ln_b, rwkv_w0, rwkv_w2, rwkv_a0, rwkv_a2, rwkv_g2, rwkv_k_k, rwkv_k_a, rwkv_r_k,
           rwkv_ln_g, rwkv_ln_b, w_out, ln1_g, ln1_b, router_w, router_b, w_gate, b_gate, w_up, b_up,
           w_down, b_down, ln2_g, ln2_b):
    assert w_in.shape[0] == 1, "single layer"
    d = 0
    row = lambda v: v.reshape(1, -1)
    n_p, t_p, _ = x_prompt.shape
    n_s, t_s, _ = x_sample.shape

    w_rkv = w_in[d][:, 2 * C_CONV:2 * C_CONV + 3 * C_RWKV]
    w_lo = _split_lora_cols(w_in[d][:, 2 * C_CONV + 3 * C_RWKV:])
    w_in_p = jnp.concatenate([w_in[d][:, :2 * C_CONV], w_rkv, w_lo], axis=1).astype(BF16)
    b_in_p = jnp.concatenate([b_in[d][None, :2 * C_CONV], b_in[d][None, 2 * C_CONV:2 * C_CONV + 3 * C_RWKV],
                              _split_lora_cols(b_in[d][None, 2 * C_CONV + 3 * C_RWKV:])], axis=1)
    mu = mu_shift[d][None, :]
    mu_lo = _split_lora_cols(mu[:, 3 * C_RWKV:])
    head_of_lane = jnp.arange(C_RWKV, dtype=jnp.int32) // HEAD
    e_mat = (head_of_lane[:, None] == jnp.arange(LANES, dtype=jnp.int32)[None, :]).astype(BF16)

    def hi_lo(w, height):
        w = _pad_rows(w, height)
        w_hi = w.astype(BF16)
        return w_hi, (w - w_hi.astype(F32)).astype(BF16)

    rwkv_params = (
        mu[:, 0:C_RWKV], mu[:, C_RWKV:2 * C_RWKV], mu[:, 2 * C_RWKV:3 * C_RWKV], mu_lo,
        row(rwkv_w0[d]), *hi_lo(rwkv_w2[d], LANES), row(rwkv_a0[d]), *hi_lo(rwkv_a2[d], LANES),
        *hi_lo(rwkv_g2[d], 2 * LANES),
        row(rwkv_k_k[d]), row(rwkv_k_a[d]), row(rwkv_r_k[d]), row(rwkv_ln_g[d]), row(rwkv_ln_b[d]),
        e_mat, e_mat.T,
    )
    wts = dict(w_in=w_in_p, b_in=b_in_p, conv_w=conv_w[d], conv_b=row(conv_b[d]),
               conv_ln_g=row(conv_ln_g[d]), conv_ln_b=row(conv_ln_b[d]), rwkv_params=rwkv_params)

    zero_conv = jnp.zeros((n_p, CONV_WIDTH - 1, C_CONV), x_prompt.dtype)
    zero_shift = jnp.zeros((n_p, 1, N_SHIFT), x_prompt.dtype)
    zero_wkv = jnp.zeros((n_p, N_HEADS, HEAD, HEAD), state_wkv.dtype)
    expert_w = (w_gate[d].reshape(-1, D_FF), w_up[d].reshape(-1, D_FF), w_down[d].reshape(-1, D_MODEL))
    c_p, y_p, conv_p, shift_p, wkv_p, expert_w = _mixer_group(x_prompt, zero_conv, zero_shift, zero_wkv, wts,
                                                              expert_w)
    wg_b = expert_w[0].reshape(N_EXPERTS, D_MODEL, D_FF)
    wu_b = expert_w[1].reshape(N_EXPERTS, D_MODEL, D_FF)
    wd_b = expert_w[2].reshape(N_EXPERTS, D_FF, D_MODEL)
    c_s, y_s, conv_s, shift_s, wkv_s, _ = _mixer_group(x_sample, state_conv[d], state_shift[d], state_wkv[d], wts)

    w_out_b = w_out[d].astype(BF16)
    rw = _pad_cols(router_w[d], ROUTER_PAD)
    rw_hi = rw.astype(BF16)
    rw_lo = (rw - rw_hi.astype(F32)).astype(BF16)
    rb = jnp.concatenate([router_b[d], jnp.full((ROUTER_PAD - N_EXPERTS,), -jnp.inf, F32)])[None, :]

    n_tok_p = n_p * t_p
    n_tok_s = n_s * t_s
    n_tok = n_tok_p + n_tok_s

    def out_proj(c2, y2, x3, x1_base, row0):
        n = c2.shape[0]
        tm = math.gcd(_pick(n, (OUT_TM, 256, 128)), row0) if row0 else _pick(n, (OUT_TM, 256, 128))
        return _out_proj(c2, y2, x3.reshape(n, D_MODEL), w_out_b[:C_CONV], w_out_b[C_CONV:], row(ln1_g[d]),
                         row(ln1_b[d]), rw_hi, rw_lo, rb, tm, n_tok, x1_base, row0)

    x1, idx_p, gate_p = out_proj(c_p, y_p, x_prompt, None, 0)
    x1, idx_s, gate_s = out_proj(c_s, y_s, x_sample, x1, n_tok_p)
    top_idx = jnp.concatenate([idx_p[:, :TOP_K], idx_s[:, :TOP_K]], axis=0)
    dest, row_tok, block_exp, n_used, n_blocks = _route(top_idx, n_tok)
    yb = _experts(block_exp, n_used, row_tok.reshape(n_blocks, 1, MOE_TM), x1, wg_b, b_gate[d], wu_b, b_up[d],
                  wd_b, b_down[d])

    def combine(dest_g, gate_g, n, row0):
        tc = _pick(n, (128, 64, 32, 16, 8))
        tc = math.gcd(tc, row0) if row0 else tc
        return _combine(dest_g.reshape(n // tc, 1, tc * TOP_K), gate_g, x1, row(ln2_g[d]), row(ln2_b[d]), yb, tc,
                        row0)

    y_prompt = combine(dest[:n_tok_p * TOP_K], gate_p, n_tok_p, 0).reshape(n_p, t_p, D_MODEL)
    y_sample = combine(dest[n_tok_p * TOP_K:], gate_s, n_tok_s, n_tok_p).reshape(n_s, t_s, D_MODEL)
    return (y_prompt, y_sample, conv_p[None], shift_p[None], wkv_p[None], conv_s[None], shift_s[None], wkv_s[None])
```

```python
import functools
import math

import jax
import jax.numpy as jnp
from jax import lax
from jax.experimental import pallas as pl
from jax.experimental.pallas import tpu as pltpu

F32 = jnp.float32
BF16 = jnp.bfloat16

D_MODEL = 2048
C_CONV = 1024
C_RWKV = 1024
HEAD = 64
N_HEADS = C_RWKV // HEAD
CONV_WIDTH = 31
R_DECAY = 64
R_ICLR = 64
R_GATE = 160
N_SHIFT = 3 * C_RWKV + R_DECAY + R_ICLR + R_GATE
N_EXPERTS = 32
TOP_K = 4
D_FF = 2048
SWIGLU_LIMIT = 7.0
SWIGLU_ALPHA = 1.702
LN_EPS = 1e-5
GN_EPS = 64e-5
ALPHA = 2.0 ** 0.25

LANES = 128
SUBLANES = 8

HIST = 32
LORA_PAD = 512
P_PAD = 2 * C_CONV + 3 * C_RWKV + LORA_PAD
GROUP_HEADS = 2
PACK = GROUP_HEADS * HEAD
N_GROUPS = N_HEADS // GROUP_HEADS
MOE_TM = 512
MOE_NJ = 4
OUT_TM = 512
RWKV_CHUNK = 64
RWKV_BLOCK = 128
ROUTER_PAD = LANES


def _dot(a, b, prec=1, dims=(((1,), (0,)), ((), ()))):
    if prec == 6:
        return lax.dot_general(a.astype(F32), b.astype(F32), dims, precision=lax.Precision.HIGHEST,
                               preferred_element_type=F32)
    d = lambda x, y: lax.dot_general(x, y, dims, preferred_element_type=F32)
    if prec == 1:
        return d(a.astype(BF16), b.astype(BF16))
    a_hi = a.astype(BF16)
    a_lo = (a - a_hi.astype(F32)).astype(BF16)
    b_hi = b.astype(BF16)
    b_lo = (b - b_hi.astype(F32)).astype(BF16)
    return d(a_hi, b_hi) + d(a_hi, b_lo) + d(a_lo, b_hi)


_NT = (((1,), (1,)), ((), ()))
_BNN = (((2,), (1,)), ((0,), (0,)))
_BNT = (((2,), (2,)), ((0,), (0,)))


def _split3(x):
    p1 = x.astype(BF16)
    r1 = x - p1.astype(F32)
    p2 = r1.astype(BF16)
    p3 = (r1 - p2.astype(F32)).astype(BF16)
    return p1, p2, p3


def _dot_exact_rhs(x, m_bf16):
    d = lambda a: jnp.dot(a, m_bf16, preferred_element_type=F32)
    p1, p2, p3 = _split3(x)
    return d(p1) + d(p2) + d(p3)


def _dot_exact_lhs(m_bf16, x):
    d = lambda a: jnp.dot(m_bf16, a, preferred_element_type=F32)
    p1, p2, p3 = _split3(x)
    return d(p1) + d(p2) + d(p3)


def _dot_split_w(x, w_hi, w_lo):
    x_hi = x.astype(BF16)
    x_lo = (x - x_hi.astype(F32)).astype(BF16)
    d = lambda a, b: jnp.dot(a, b, preferred_element_type=F32)
    return d(x_hi, w_hi) + d(x_hi, w_lo) + d(x_lo, w_hi)


def _mm_bias_kernel(x_ref, w_ref, b_ref, o_ref, xb_ref):
    @pl.when(pl.program_id(1) == 0)
    def _():
        xb_ref[...] = x_ref[...].astype(BF16)

    o_ref[...] = jnp.dot(xb_ref[...], w_ref[...], preferred_element_type=F32) + b_ref[...]


def _in_proj(x, w_bf16, b, tm, tn):
    n, k = x.shape
    p = w_bf16.shape[1]
    return pl.pallas_call(
        _mm_bias_kernel,
        out_shape=jax.ShapeDtypeStruct((n, p), F32),
        grid=(n // tm, p // tn),
        in_specs=[pl.BlockSpec((tm, k), lambda i, j: (i, 0)),
                  pl.BlockSpec((k, tn), lambda i, j: (0, j)),
                  pl.BlockSpec((1, tn), lambda i, j: (0, j))],
        out_specs=pl.BlockSpec((tm, tn), lambda i, j: (i, j)),
        scratch_shapes=[pltpu.VMEM((tm, k), BF16)],
        compiler_params=pltpu.CompilerParams(dimension_semantics=("arbitrary", "arbitrary")),
        name="in_proj",
    )(x, w_bf16, b)


def _conv_kernel(val_ref, gate_ref, hist_ref, w_ref, cb_ref, g_ref, b_ref, c_ref, tail_ref, ext_ref, sh_ref):
    t = pl.program_id(1)
    tt = val_ref.shape[1]

    @pl.when(t == 0)
    def _():
        ext_ref[0:HIST, :] = hist_ref[0]

    u = val_ref[0] * jax.nn.sigmoid(gate_ref[0])
    ext_ref[HIST:HIST + tt, :] = u
    span = tt + HIST - SUBLANES
    for s in range(1, SUBLANES):
        sh_ref[s, 0:span, :] = ext_ref[s:s + span, :]
    off = HIST - (CONV_WIDTH - 1)
    acc = jnp.broadcast_to(cb_ref[...], (tt, C_CONV))
    for j in range(CONV_WIDTH):
        base = (off + j) // SUBLANES * SUBLANES
        s = (off + j) % SUBLANES
        src = ext_ref[base:base + tt, :] if s == 0 else sh_ref[s, base:base + tt, :]
        acc = acc + w_ref[j:j + 1, :] * src
    mu = jnp.mean(acc, axis=-1, keepdims=True)
    xc = acc - mu
    var = jnp.mean(xc * xc, axis=-1, keepdims=True)
    y = xc * lax.rsqrt(var + LN_EPS) * g_ref[...] + b_ref[...]
    c_ref[0] = (y * jax.nn.sigmoid(y)).astype(c_ref.dtype)
    tail = ext_ref[tt:tt + HIST, :]
    ext_ref[0:HIST, :] = tail
    tail_ref[0] = tail


def _conv_module(proj3, hist, conv_w, conv_b, ln_g, ln_b, tt):
    bsz, t_len, _ = proj3.shape
    nblk = C_CONV // C_CONV
    return pl.pallas_call(
        _conv_kernel,
        out_shape=(jax.ShapeDtypeStruct((bsz, t_len, C_CONV), BF16),
                   jax.ShapeDtypeStruct((bsz, HIST, C_CONV), F32)),
        grid=(bsz, t_len // tt),
        in_specs=[pl.BlockSpec((1, tt, C_CONV), lambda b, t: (b, t, 0)),
                  pl.BlockSpec((1, tt, C_CONV), lambda b, t: (b, t, nblk)),
                  pl.BlockSpec((1, HIST, C_CONV), lambda b, t: (b, 0, 0)),
                  pl.BlockSpec((CONV_WIDTH, C_CONV), lambda b, t: (0, 0)),
                  pl.BlockSpec((1, C_CONV), lambda b, t: (0, 0)),
                  pl.BlockSpec((1, C_CONV), lambda b, t: (0, 0)),
                  pl.BlockSpec((1, C_CONV), lambda b, t: (0, 0))],
        out_specs=(pl.BlockSpec((1, tt, C_CONV), lambda b, t: (b, t, 0)),
                   pl.BlockSpec((1, HIST, C_CONV), lambda b, t: (b, 0, 0))),
        scratch_shapes=[pltpu.VMEM((HIST + tt, C_CONV), F32),
                        pltpu.VMEM((SUBLANES, HIST + tt, C_CONV), F32)],
        compiler_params=pltpu.CompilerParams(dimension_semantics=("arbitrary", "arbitrary")),
        name="conv_module",
    )(proj3, proj3, hist, conv_w, conv_b, ln_g, ln_b)


PREC_CHUNK = 1
PREC_STATE = 1


def _seg_sum(x, e_ref, et_ref):
    return _dot_exact_rhs(_dot_exact_rhs(x, e_ref[...]), et_ref[...])


def _rwkv_kernel(chunk, n_cast, *refs):
    n_in = 28
    (r_ref, k_ref, v_ref, lo_ref, shr_ref, shk_ref, shv_ref, shlo_ref, st0_ref,
     mur_ref, muk_ref, muv_ref, mulo_ref, w0_ref, w2h_ref, w2l_ref, a0_ref, a2h_ref, a2l_ref,
     g2h_ref, g2l_ref, kkw_ref, kaw_ref, rkw_ref, lng_ref, lnb_ref, e_ref, et_ref) = refs[:n_in]
    cast_in = refs[n_in:n_in + n_cast]
    y_ref, stout_ref, shout_ref = refs[n_in + n_cast:n_in + n_cast + 3]
    cast_out = refs[n_in + n_cast + 3:n_in + 2 * n_cast + 3]
    st_sc, pr_sc, pk_sc, pv_sc, plo_sc = refs[n_in + 2 * n_cast + 3:]

    for src, dst in zip(cast_in, cast_out):
        dst[...] = src[...].astype(dst.dtype)

    c = pl.program_id(1)
    n_chunks = pl.num_programs(1)
    Tb = r_ref.shape[1]
    L = chunk
    n_sub = Tb // L
    GL = GROUP_HEADS * L
    log2l = int(math.log2(L))

    @pl.when(c == 0)
    def _():
        st_sc[...] = st0_ref[0]
        pr_sc[...] = shr_ref[0]
        pk_sc[...] = shk_ref[0]
        pv_sc[...] = shv_ref[0]
        plo_sc[...] = shlo_ref[0]

    def token_shift(x_ref, prev_sc, mu_ref):
        x = x_ref[0]
        row = lax.broadcasted_iota(jnp.int32, x.shape, 0)
        xprev = jnp.where(row == 0, jnp.broadcast_to(prev_sc[...], x.shape), pltpu.roll(x, 1, 0))
        prev_sc[...] = x[Tb - 1:Tb, :]
        return x + mu_ref[...] * (xprev - x)

    r = token_shift(r_ref, pr_sc, mur_ref)
    k = token_shift(k_ref, pk_sc, muk_ref)
    v = token_shift(v_ref, pv_sc, muv_ref)
    lo = token_shift(lo_ref, plo_sc, mulo_ref)
    xw = lo[:, 0:LANES]
    xa = lo[:, LANES:2 * LANES]
    xg = lo[:, 2 * LANES:LORA_PAD]

    u_dec = w0_ref[...] + _dot_split_w(jnp.tanh(xw), w2h_ref[...], w2l_ref[...])
    logw = (-math.exp(-0.5)) * jax.nn.sigmoid(u_dec)
    a = jax.nn.sigmoid(a0_ref[...] + _dot_split_w(xa, a2h_ref[...], a2l_ref[...]))
    g = _dot_split_w(jax.nn.sigmoid(xg), g2h_ref[...], g2l_ref[...])

    kk = k * kkw_ref[...]
    nrm = jnp.sqrt(_seg_sum(kk * kk, e_ref, et_ref))
    kappa = kk / jnp.maximum(nrm, 1e-12)
    k2 = k * (1.0 + (a - 1.0) * kaw_ref[...])
    bvec = kappa * a
    bonus = _seg_sum(r * k2 * rkw_ref[...], e_ref, et_ref) * v

    ti = lax.broadcasted_iota(jnp.int32, (Tb, Tb), 0)
    tj = lax.broadcasted_iota(jnp.int32, (Tb, Tb), 1)
    tril = jnp.where((tj <= ti) & ((ti >> log2l) == (tj >> log2l)), 1.0, 0.0).astype(BF16)
    cum = _dot_exact_lhs(tril, logw)
    trow = lax.broadcasted_iota(jnp.int32, (Tb, C_RWKV), 0)
    cum_end = jnp.broadcast_to(cum[L - 1:L, :], (Tb, C_RWKV))
    for s in range(1, n_sub):
        cum_end = jnp.where(trow >= s * L, jnp.broadcast_to(cum[(s + 1) * L - 1:(s + 1) * L, :], (Tb, C_RWKV)),
                            cum_end)
    gam = jnp.exp(cum)
    ginv = jnp.exp(-cum)
    gprev = jnp.exp(cum - logw)
    gtail = jnp.exp(cum_end - cum)

    kt = kappa * gprev
    kinv = k2 * ginv
    binv = bvec * ginv
    rt = r * gam
    khat = k2 * gtail
    bhat = bvec * gtail

    rr = lax.broadcasted_iota(jnp.int32, (GL, GL), 0)
    cc = lax.broadcasted_iota(jnp.int32, (GL, GL), 1)
    same = (rr >> log2l) == (cc >> log2l)
    tpos = rr & (L - 1)
    jpos = cc & (L - 1)
    mask_s = same & (jpos < tpos)
    mask_i = same & (jpos <= tpos)
    eye = rr == cc
    srow = lax.broadcasted_iota(jnp.int32, (GL, PACK), 0)
    slane = lax.broadcasted_iota(jnp.int32, (GL, PACK), 1)
    bmask = (srow >> log2l) == (slane >> int(math.log2(HEAD)))
    drow = lax.broadcasted_iota(jnp.int32, (PACK, PACK), 0)
    dcol = lax.broadcasted_iota(jnp.int32, (PACK, PACK), 1)
    deye = drow == dcol

    lane_split = GL % LANES == 0

    def bdot(a_, b_, dims=_BNN):
        return lax.dot_general(a_.astype(BF16), b_.astype(BF16), dims, preferred_element_type=F32)

    def btrans(x):
        return jnp.stack([x[gi].T for gi in range(N_GROUPS)], axis=0)

    def chunk_step(row0, st):
        def stack(x):
            xc = x[row0:row0 + L]
            x3 = jnp.stack([xc[:, gi * PACK:(gi + 1) * PACK] for gi in range(N_GROUPS)], axis=0)
            return jnp.where(bmask[None], jnp.concatenate([x3] * GROUP_HEADS, axis=1), 0.0)

        kt_s = stack(kt)
        rt_s = stack(rt)
        binv_s = stack(binv)
        kinv_s = stack(kinv)
        v_s = stack(v)
        khat_s = stack(khat)
        bhat_s = stack(bhat)

        if lane_split:
            a_all = bdot(jnp.concatenate([kt_s, rt_s], axis=1), jnp.concatenate([binv_s, kinv_s], axis=1), _BNT)
            a_parts = (a_all[:, :GL, :GL], a_all[:, :GL, GL:], a_all[:, GL:, :GL], a_all[:, GL:, GL:])
        else:
            a_parts = (bdot(kt_s, binv_s, _BNT), bdot(kt_s, kinv_s, _BNT),
                       bdot(rt_s, binv_s, _BNT), bdot(rt_s, kinv_s, _BNT))
        n_mat = jnp.where(mask_s[None], a_parts[0], 0.0)
        a_kk = jnp.where(mask_s[None], a_parts[1], 0.0)
        a_br = jnp.where(mask_i[None], a_parts[2], 0.0)
        a_kr = jnp.where(mask_i[None], a_parts[3], 0.0)

        p_mat = -n_mat
        t_mat = jnp.where(eye[None], 1.0, 0.0) + p_mat
        if log2l > 1:
            p_mat = bdot(p_mat, p_mat)
        for lvl in range(1, log2l):
            if lvl == log2l - 1:
                t_mat = t_mat + bdot(p_mat, t_mat)
            elif lane_split:
                both = bdot(p_mat, jnp.concatenate([p_mat, t_mat], axis=2))
                t_mat = t_mat + both[:, :, GL:]
                p_mat = both[:, :, :GL]
            else:
                t_mat = t_mat + bdot(p_mat, t_mat)
                p_mat = bdot(p_mat, p_mat)

        av = bdot(jnp.concatenate([a_kk, a_kr], axis=1), v_s)
        wu = bdot(t_mat, jnp.concatenate([kt_s, av[:, :GL]], axis=2))
        br = bdot(a_br, wu)
        q_s = rt_s - br[:, :, :PACK]
        y0_s = av[:, GL:] - br[:, :, PACK:]
        bhat_t = btrans(bhat_s)
        khat_t = btrans(khat_s)
        gam_end = jnp.exp(cum[row0 + L - 1:row0 + L, :])
        gl3 = jnp.stack([gam_end[:, gi * PACK:(gi + 1) * PACK] for gi in range(N_GROUPS)], axis=0)
        bw = bdot(bhat_t, wu)
        m_mat = jnp.where(deye[None], jnp.broadcast_to(gl3, (N_GROUPS, PACK, PACK)), 0.0) - bw[:, :, :PACK]
        c_mat = bdot(khat_t, v_s) - bw[:, :, PACK:]

        qm = bdot(jnp.concatenate([q_s, m_mat], axis=1), st)
        ys = qm[:, :GL] + y0_s
        yg = ys[:, 0:L]
        for h in range(1, GROUP_HEADS):
            yg = yg + ys[:, h * L:(h + 1) * L]
        return jnp.concatenate([yg[gi] for gi in range(N_GROUPS)], axis=1), qm[:, GL:] + c_mat

    st = st_sc[...]
    y_chunks = []
    for s in range(n_sub):
        y_c, st = chunk_step(s * L, st)
        y_chunks.append(y_c)
    st_sc[...] = st
    y = y_chunks[0] if n_sub == 1 else jnp.concatenate(y_chunks, axis=0)

    inv_head = 1.0 / HEAD
    mu = _seg_sum(y, e_ref, et_ref) * inv_head
    yc = y - mu
    var = _seg_sum(yc * yc, e_ref, et_ref) * inv_head
    yn = yc * lax.rsqrt(var + GN_EPS) * lng_ref[...] + lnb_ref[...]
    y_ref[0] = ((yn + bonus) * g).astype(y_ref.dtype)

    @pl.when(c == n_chunks - 1)
    def _():
        stout_ref[0] = st_sc[...]
        shout_ref[0, :, 0:C_RWKV] = pr_sc[...]
        shout_ref[0, :, C_RWKV:2 * C_RWKV] = pk_sc[...]
        shout_ref[0, :, 2 * C_RWKV:3 * C_RWKV] = pv_sc[...]
        shout_ref[0, :, 3 * C_RWKV:3 * C_RWKV + LORA_PAD] = plo_sc[...]


def _rwkv_mix(proj3, shift_parts, st0, params, chunk, block, cast_arrays=()):
    bsz, t_len, _ = proj3.shape
    L = block
    n_steps = bsz * (t_len // L)
    n_cast = len(cast_arrays)
    rkv_blk0 = 2 * C_CONV // C_RWKV
    lora_blk = (2 * C_CONV + 3 * C_RWKV) // LORA_PAD
    row = lambda n: pl.BlockSpec((1, n), lambda b, c: (0, 0))
    full = lambda s: pl.BlockSpec(s, lambda b, c: tuple(0 for _ in s))
    sh = lambda n: pl.BlockSpec((1, 1, n), lambda b, c: (b, 0, 0))
    in_specs = [
        pl.BlockSpec((1, L, C_RWKV), lambda b, c: (b, c, rkv_blk0)),
        pl.BlockSpec((1, L, C_RWKV), lambda b, c: (b, c, rkv_blk0 + 1)),
        pl.BlockSpec((1, L, C_RWKV), lambda b, c: (b, c, rkv_blk0 + 2)),
        pl.BlockSpec((1, L, LORA_PAD), lambda b, c: (b, c, lora_blk)),
        sh(C_RWKV), sh(C_RWKV), sh(C_RWKV), sh(LORA_PAD),
        pl.BlockSpec((1, N_GROUPS, PACK, PACK), lambda b, c: (b, 0, 0, 0)),
        row(C_RWKV), row(C_RWKV), row(C_RWKV), row(LORA_PAD),
        row(C_RWKV), full((LANES, C_RWKV)), full((LANES, C_RWKV)),
        row(C_RWKV), full((LANES, C_RWKV)), full((LANES, C_RWKV)),
        full((2 * LANES, C_RWKV)), full((2 * LANES, C_RWKV)),
        row(C_RWKV), row(C_RWKV), row(C_RWKV), row(C_RWKV), row(C_RWKV),
        full((C_RWKV, LANES)), full((LANES, C_RWKV)),
    ]
    out_shape = (jax.ShapeDtypeStruct((bsz, t_len, C_RWKV), BF16),
                 jax.ShapeDtypeStruct((bsz, N_GROUPS, PACK, PACK), F32),
                 jax.ShapeDtypeStruct((bsz, 1, 3 * C_RWKV + LORA_PAD), F32))
    out_specs = (pl.BlockSpec((1, L, C_RWKV), lambda b, c: (b, c, 0)),
                 pl.BlockSpec((1, N_GROUPS, PACK, PACK), lambda b, c: (b, 0, 0, 0)),
                 pl.BlockSpec((1, 1, 3 * C_RWKV + LORA_PAD), lambda b, c: (b, 0, 0)))
    steps_per_b = t_len // L
    for arr in cast_arrays:
        rows, width = arr.shape
        win = rows // n_steps
        assert win * n_steps == rows
        spec = pl.BlockSpec((win, width), lambda b, c: (b * steps_per_b + c, 0))
        in_specs = in_specs + [spec]
        out_specs = out_specs + (spec,)
        out_shape = out_shape + (jax.ShapeDtypeStruct((rows, width), BF16),)
    return pl.pallas_call(
        functools.partial(_rwkv_kernel, chunk, n_cast),
        out_shape=out_shape,
        grid=(bsz, t_len // L),
        in_specs=in_specs,
        out_specs=out_specs,
        scratch_shapes=[pltpu.VMEM((N_GROUPS, PACK, PACK), F32),
                        pltpu.VMEM((1, C_RWKV), F32), pltpu.VMEM((1, C_RWKV), F32),
                        pltpu.VMEM((1, C_RWKV), F32), pltpu.VMEM((1, LORA_PAD), F32)],
        compiler_params=pltpu.CompilerParams(dimension_semantics=("arbitrary", "arbitrary")),
        name="rwkv7_mix",
    )(proj3, proj3, proj3, proj3, *shift_parts, st0, *params, *cast_arrays)


def _outproj_kernel(c_ref, y_ref, x_ref, wa_ref, wb_ref, g_ref, b_ref, rwh_ref, rwl_ref, rb_ref,
                    x1_ref, idx_ref, gate_ref):
    mix = (jnp.dot(c_ref[...], wa_ref[...], preferred_element_type=F32)
           + jnp.dot(y_ref[...], wb_ref[...], preferred_element_type=F32))
    h = ALPHA * x_ref[...] + mix
    mu = jnp.mean(h, axis=-1, keepdims=True)
    hc = h - mu
    var = jnp.mean(hc * hc, axis=-1, keepdims=True)
    x1 = hc * lax.rsqrt(var + LN_EPS) * g_ref[...] + b_ref[...]
    x1_ref[...] = x1
    logits = _dot_split_w(x1, rwh_ref[...], rwl_ref[...]) + rb_ref[...]
    lane = lax.broadcasted_iota(jnp.int32, logits.shape, 1)
    idx_out = jnp.zeros(logits.shape, jnp.int32)
    val_out = jnp.zeros(logits.shape, F32)
    vals = []
    for kk in range(TOP_K):
        m = jnp.max(logits, axis=-1, keepdims=True)
        sel = jnp.min(jnp.where(logits == m, lane, ROUTER_PAD), axis=-1, keepdims=True)
        vals.append(m)
        idx_out = jnp.where(lane == kk, sel, idx_out)
        logits = jnp.where(lane == sel, -jnp.inf, logits)
    exps = [jnp.exp(vv - vals[0]) for vv in vals]
    denom = exps[0]
    for ee in exps[1:]:
        denom = denom + ee
    for kk in range(TOP_K):
        val_out = jnp.where(lane == kk, exps[kk] / denom, val_out)
    idx_ref[...] = idx_out
    gate_ref[...] = val_out


def _outproj_into_kernel(base_ref, *refs):
    del base_ref
    _outproj_kernel(*refs)


def _out_proj(c2, y2, x2, wa, wb, ln_g, ln_b, rw_hi, rw_lo, rb, tm, n_total, x1_base=None, row0=0):
    n = x2.shape[0]
    blk0 = row0 // tm
    assert blk0 * tm == row0
    row = lambda w: pl.BlockSpec((1, w), lambda i: (0, 0))
    in_specs = [pl.BlockSpec((tm, C_CONV), lambda i: (i, 0)),
                pl.BlockSpec((tm, C_RWKV), lambda i: (i, 0)),
                pl.BlockSpec((tm, D_MODEL), lambda i: (i, 0)),
                pl.BlockSpec((C_CONV, D_MODEL), lambda i: (0, 0)),
                pl.BlockSpec((C_RWKV, D_MODEL), lambda i: (0, 0)),
                row(D_MODEL), row(D_MODEL),
                pl.BlockSpec((D_MODEL, ROUTER_PAD), lambda i: (0, 0)),
                pl.BlockSpec((D_MODEL, ROUTER_PAD), lambda i: (0, 0)),
                row(ROUTER_PAD)]
    args = (c2, y2, x2, wa, wb, ln_g, ln_b, rw_hi, rw_lo, rb)
    body, aliases = _outproj_kernel, {}
    if x1_base is not None:
        in_specs = [pl.BlockSpec(memory_space=pl.ANY)] + in_specs
        args = (x1_base,) + args
        body, aliases = _outproj_into_kernel, {0: 0}
    return pl.pallas_call(
        body,
        out_shape=(jax.ShapeDtypeStruct((n_total, D_MODEL), F32),
                   jax.ShapeDtypeStruct((n, ROUTER_PAD), jnp.int32),
                   jax.ShapeDtypeStruct((n, ROUTER_PAD), F32)),
        grid=(n // tm,),
        in_specs=in_specs,
        out_specs=(pl.BlockSpec((tm, D_MODEL), lambda i: (i + blk0, 0)),
                   pl.BlockSpec((tm, ROUTER_PAD), lambda i: (i, 0)),
                   pl.BlockSpec((tm, ROUTER_PAD), lambda i: (i, 0))),
        input_output_aliases=aliases,
        compiler_params=pltpu.CompilerParams(dimension_semantics=("arbitrary",)),
        name="out_proj_ln_router",
    )(*args)


DMA_PRIORITIES = 2


def _expert_kernel(bexp_ref, nused_ref, tokc_ref, tokn_ref, x_hbm, wg_ref, bg_ref, wu_ref, bu_ref, wd_ref, bd_ref,
                   o_ref, xg_ref, xb_ref, sems):
    del bexp_ref
    i = pl.program_id(0)
    j = pl.program_id(1)
    nj = pl.num_programs(1)
    n_used = nused_ref[0]
    rows = xb_ref.shape[0]
    slot = lax.rem(i, 2)

    def row_copy(tok_ref, s, g, sub):
        tok = tok_ref[0, 0, g * SUBLANES + sub]
        return pltpu.make_async_copy(x_hbm.at[pl.ds(tok, 1)], xg_ref.at[s, g, pl.ds(sub, 1)], sems.at[s])

    def issue(tok_ref, s, g0, n_groups):
        def body(g, carry):
            for sub in range(SUBLANES):
                row_copy(tok_ref, s, g0 + g, sub).start(priority=sub % DMA_PRIORITIES)
            return carry

        lax.fori_loop(0, n_groups, body, 0)

    groups = rows // SUBLANES

    @pl.when((i == 0) & (j == 0) & (n_used > 0))
    def _():
        issue(tokc_ref, 0, 0, groups)

    @pl.when(i + 1 < n_used)
    def _():
        per_step = groups // MOE_NJ
        issue(tokn_ref, 1 - slot, j * per_step, per_step)

    @pl.when(i < n_used)
    def _():
        @pl.when(j == 0)
        def _():
            def drain(g, carry):
                for sub in range(SUBLANES):
                    row_copy(tokc_ref, slot, g, sub).wait()
                return carry

            lax.fori_loop(0, groups, drain, 0)
            xb_ref[...] = xg_ref[slot].reshape(rows, D_MODEL).astype(BF16)
            o_ref[...] = jnp.broadcast_to(bd_ref[0], o_ref.shape)

        x = xb_ref[...]
        gate = jnp.minimum(jnp.dot(x, wg_ref[0], preferred_element_type=F32) + bg_ref[0, pl.ds(j, 1), :],
                           SWIGLU_LIMIT)
        up = jnp.clip(jnp.dot(x, wu_ref[0], preferred_element_type=F32) + bu_ref[0, pl.ds(j, 1), :],
                      -SWIGLU_LIMIT, SWIGLU_LIMIT)
        hmid = (up + 1.0) * gate * jax.nn.sigmoid(SWIGLU_ALPHA * gate)
        o_ref[...] += jnp.dot(hmid.astype(BF16), wd_ref[0], preferred_element_type=F32)

    @pl.when((i >= nused_ref[0]) & (j == nj - 1))
    def _():
        o_ref[...] = jnp.zeros_like(o_ref)


def _experts(block_exp, n_used, row_tok3, x, wg, bg, wu, bu, wd, bd):
    nb, _, rows = row_tok3.shape
    n_exp = wg.shape[0]
    nj = MOE_NJ
    tf = D_FF // nj

    def jsel(i, j, nu):
        return jnp.where(i < nu[0], j, nj - 1)

    tok_spec = lambda off: pl.BlockSpec((1, 1, rows), lambda i, j, be, nu: (jnp.minimum(i + off, nb - 1), 0, 0),
                                        memory_space=pltpu.SMEM)
    grid_spec = pltpu.PrefetchScalarGridSpec(
        num_scalar_prefetch=2,
        grid=(nb, nj),
        in_specs=[tok_spec(0), tok_spec(1),
                  pl.BlockSpec(memory_space=pl.ANY),
                  pl.BlockSpec((1, D_MODEL, tf), lambda i, j, be, nu: (be[i], 0, jsel(i, j, nu))),
                  pl.BlockSpec((1, nj, tf), lambda i, j, be, nu: (be[i], 0, 0)),
                  pl.BlockSpec((1, D_MODEL, tf), lambda i, j, be, nu: (be[i], 0, jsel(i, j, nu))),
                  pl.BlockSpec((1, nj, tf), lambda i, j, be, nu: (be[i], 0, 0)),
                  pl.BlockSpec((1, tf, D_MODEL), lambda i, j, be, nu: (be[i], jsel(i, j, nu), 0)),
                  pl.BlockSpec((1, 1, D_MODEL), lambda i, j, be, nu: (be[i], 0, 0))],
        out_specs=pl.BlockSpec((rows, D_MODEL), lambda i, j, be, nu: (i, 0)),
        scratch_shapes=[pltpu.VMEM((2, rows // SUBLANES, SUBLANES, D_MODEL), x.dtype),
                        pltpu.VMEM((rows, D_MODEL), BF16), pltpu.SemaphoreType.DMA((2,))],
    )
    return pl.pallas_call(
        _expert_kernel,
        out_shape=jax.ShapeDtypeStruct((nb * rows, D_MODEL), F32),
        grid_spec=grid_spec,
        compiler_params=pltpu.CompilerParams(dimension_semantics=("arbitrary", "arbitrary")),
        name="moe_experts",
    )(block_exp, n_used, row_tok3, row_tok3, x, wg, bg.reshape(n_exp, nj, tf), wu, bu.reshape(n_exp, nj, tf),
      wd, bd.reshape(n_exp, 1, D_MODEL))


def _combine_kernel(dest_ref, gate_ref, x1_ref, g_ref, b_ref, yb_hbm, o_ref, buf_ref, sem):
    tc = x1_ref.shape[0]

    def row_copy(d, kk, g, sub):
        return pltpu.make_async_copy(yb_hbm.at[pl.ds(d, 1)], buf_ref.at[kk, g, pl.ds(sub, 1)], sem)

    def issue(g, carry):
        for sub in range(SUBLANES):
            for kk in range(TOP_K):
                d = dest_ref[0, 0, (g * SUBLANES + sub) * TOP_K + kk]
                row_copy(d, kk, g, sub).start(priority=kk % DMA_PRIORITIES)
        return carry

    lax.fori_loop(0, tc // SUBLANES, issue, 0)

    def drain(g, carry):
        for sub in range(SUBLANES):
            for kk in range(TOP_K):
                row_copy(0, kk, g, sub).wait()
        return carry

    lax.fori_loop(0, tc // SUBLANES, drain, 0)

    gates = gate_ref[...]
    moe = gates[:, 0:1] * buf_ref[0].reshape(tc, D_MODEL)
    for kk in range(1, TOP_K):
        moe = moe + gates[:, kk:kk + 1] * buf_ref[kk].reshape(tc, D_MODEL)
    h = ALPHA * x1_ref[...] + moe
    mu = jnp.mean(h, axis=-1, keepdims=True)
    hc = h - mu
    var = jnp.mean(hc * hc, axis=-1, keepdims=True)
    o_ref[...] = hc * lax.rsqrt(var + LN_EPS) * g_ref[...] + b_ref[...]


def _combine(dest3, gates, x1, ln_g, ln_b, yb, tc, row0=0):
    n = dest3.shape[0] * tc
    blk0 = row0 // tc
    assert blk0 * tc == row0
    return pl.pallas_call(
        _combine_kernel,
        out_shape=jax.ShapeDtypeStruct((n, D_MODEL), F32),
        grid=(n // tc,),
        in_specs=[pl.BlockSpec((1, 1, tc * TOP_K), lambda i: (i, 0, 0), memory_space=pltpu.SMEM),
                  pl.BlockSpec((tc, ROUTER_PAD), lambda i: (i, 0)),
                  pl.BlockSpec((tc, D_MODEL), lambda i: (i + blk0, 0)),
                  pl.BlockSpec((1, D_MODEL), lambda i: (0, 0)),
                  pl.BlockSpec((1, D_MODEL), lambda i: (0, 0)),
                  pl.BlockSpec(memory_space=pl.ANY)],
        out_specs=pl.BlockSpec((tc, D_MODEL), lambda i: (i, 0)),
        scratch_shapes=[pltpu.VMEM((TOP_K, tc // SUBLANES, SUBLANES, D_MODEL), F32), pltpu.SemaphoreType.DMA],
        compiler_params=pltpu.CompilerParams(dimension_semantics=("arbitrary",)),
        name="moe_combine_ln2",
    )(dest3, gates, x1, ln_g, ln_b, yb)


def _pad_cols(w, width):
    return jnp.pad(w, ((0, 0), (0, width - w.shape[1])))


def _pad_rows(w, height):
    return jnp.pad(w, ((0, height - w.shape[0]), (0, 0)))


def _split_lora_cols(w):
    xw = w[..., 0:R_DECAY]
    xa = w[..., R_DECAY:R_DECAY + R_ICLR]
    xg = w[..., R_DECAY + R_ICLR:]
    pad = lambda x, n: jnp.pad(x, [(0, 0)] * (x.ndim - 1) + [(0, n - x.shape[-1])])
    return jnp.concatenate([pad(xw, LANES), pad(xa, LANES), pad(xg, 2 * LANES)], axis=-1)


def _pick(n, prefs):
    for p in prefs:
        if n % p == 0:
            return p
    return n


def _mixer_group(x, conv_buf, shift_buf, wkv_state, wts, cast_along=()):
    bsz, t_len, _ = x.shape
    n = bsz * t_len
    proj = _in_proj(x.reshape(n, D_MODEL), wts["w_in"], wts["b_in"], _pick(n, (1024, 512, 256, 128)), 512)
    proj3 = proj.reshape(bsz, t_len, P_PAD)

    hist = jnp.pad(conv_buf, ((0, 0), (HIST - (CONV_WIDTH - 1), 0), (0, 0)))
    c, tail = _conv_module(proj3, hist, wts["conv_w"], wts["conv_b"], wts["conv_ln_g"], wts["conv_ln_b"],
                           _pick(t_len, (128, 64, 32, 16, 8)))
    new_conv = tail[:, HIST - (CONV_WIDTH - 1):, :]

    sh_rkv = shift_buf[:, :, :3 * C_RWKV]
    sh_lo = _split_lora_cols(shift_buf[:, :, 3 * C_RWKV:])
    shift_parts = (sh_rkv[:, :, 0:C_RWKV], sh_rkv[:, :, C_RWKV:2 * C_RWKV], sh_rkv[:, :, 2 * C_RWKV:], sh_lo)
    st_t = jnp.swapaxes(wkv_state, -1, -2).reshape(bsz, N_GROUPS, GROUP_HEADS, HEAD, HEAD)
    eye_h = jnp.eye(GROUP_HEADS, dtype=F32)
    st0 = jnp.einsum("bghkv,hj->bghkjv", st_t, eye_h).reshape(bsz, N_GROUPS, PACK, PACK)
    chunk = _pick(t_len, (RWKV_CHUNK, 32, 16))
    block = RWKV_BLOCK if t_len % RWKV_BLOCK == 0 else chunk
    n_steps = bsz * (t_len // block)
    ride = tuple(w for w in cast_along if w.shape[0] % n_steps == 0 and (w.shape[0] // n_steps) % 16 == 0)
    yb, st_out, sh_out, *cast_done = _rwkv_mix(proj3, shift_parts, st0, wts["rwkv_params"], chunk, block, ride)
    if len(ride) != len(cast_along):
        cast_done = [w.astype(BF16) for w in cast_along]
    st5 = st_out.reshape(bsz, N_GROUPS, GROUP_HEADS, HEAD, GROUP_HEADS, HEAD)
    st_diag = jnp.einsum("bghkhv->bghkv", st5)
    new_wkv = jnp.swapaxes(st_diag, -1, -2).reshape(bsz, N_HEADS, HEAD, HEAD)
    lo = sh_out[:, :, 3 * C_RWKV:]
    new_shift = jnp.concatenate([sh_out[:, :, :3 * C_RWKV], lo[:, :, 0:R_DECAY], lo[:, :, LANES:LANES + R_ICLR],
                                 lo[:, :, 2 * LANES:2 * LANES + R_GATE]], axis=-1)
    return c.reshape(n, C_CONV), yb.reshape(n, C_RWKV), new_conv, new_shift, new_wkv, cast_done


def _route(top_idx, n_tok):
    n_assign = n_tok * TOP_K
    flat_e = top_idx.reshape(-1)
    onehot = (flat_e[:, None] == jnp.arange(N_EXPERTS, dtype=jnp.int32)[None, :]).astype(jnp.int32)
    csum = jnp.cumsum(onehot, axis=0)
    rank = jnp.take_along_axis(csum, flat_e[:, None], axis=1)[:, 0] - 1
    counts = csum[-1]
    padded = (counts + MOE_TM - 1) // MOE_TM * MOE_TM
    seg_end = jnp.cumsum(padded)
    seg_start = seg_end - padded
    dest = (seg_start[flat_e] + rank).astype(jnp.int32)
    n_rows = (n_assign + N_EXPERTS * (MOE_TM - 1) + MOE_TM - 1) // MOE_TM * MOE_TM
    n_blocks = n_rows // MOE_TM
    block_start = jnp.arange(n_blocks, dtype=jnp.int32) * MOE_TM
    block_exp = jnp.minimum(jnp.sum((seg_end[None, :] <= block_start[:, None]).astype(jnp.int32), axis=1),
                            N_EXPERTS - 1).astype(jnp.int32)
    order = jnp.argsort(flat_e, stable=True).astype(jnp.int32)
    start = jnp.cumsum(counts) - counts
    local = jnp.arange(MOE_TM, dtype=jnp.int32)[None, :] + (block_start - seg_start[block_exp])[:, None]
    valid = local < counts[block_exp][:, None]
    pos = jnp.clip(start[block_exp][:, None] + local, 0, n_assign - 1)
    row_tok = jnp.where(valid, order[pos.reshape(-1)].reshape(n_blocks, MOE_TM) // TOP_K, 0).reshape(-1)
    n_used = (seg_end[-1] // MOE_TM).astype(jnp.int32).reshape(1)
    return dest, row_tok, block_exp, n_used, n_blocks


def kernel(x_prompt, x_sample, state_conv, state_shift, state_wkv, w_in, b_in, mu_shift, conv_w, conv_b,
           conv_ln_g, conv_ln_b, rwkv_w0, rwkv_w2, rwkv_a0, rwkv_a2, rwkv_g2, rwkv_k_k, rwkv_k_a, rwkv_r_k,
           rwkv_ln_g, rwkv_ln_b, w_out, ln1_g, ln1_b, router_w, router_b, w_gate, b_gate, w_up, b_up,
           w_down, b_down, ln2_g, ln2_b):
    assert w_in.shape[0] == 1, "single layer"
    d = 0
    row = lambda v: v.reshape(1, -1)
    n_p, t_p, _ = x_prompt.shape
    n_s, t_s, _ = x_sample.shape

    w_rkv = w_in[d][:, 2 * C_CONV:2 * C_CONV + 3 * C_RWKV]
    w_lo = _split_lora_cols(w_in[d][:, 2 * C_CONV + 3 * C_RWKV:])
    w_in_p = jnp.concatenate([w_in[d][:, :2 * C_CONV], w_rkv, w_lo], axis=1).astype(BF16)
    b_in_p = jnp.concatenate([b_in[d][None, :2 * C_CONV], b_in[d][None, 2 * C_CONV:2 * C_CONV + 3 * C_RWKV],
                              _split_lora_cols(b_in[d][None, 2 * C_CONV + 3 * C_RWKV:])], axis=1)
    mu = mu_shift[d][None, :]
    mu_lo = _split_lora_cols(mu[:, 3 * C_RWKV:])
    head_of_lane = jnp.arange(C_RWKV, dtype=jnp.int32) // HEAD
    e_mat = (head_of_lane[:, None] == jnp.arange(LANES, dtype=jnp.int32)[None, :]).astype(BF16)

    def hi_lo(w, height):
        w = _pad_rows(w, height)
        w_hi = w.astype(BF16)
        return w_hi, (w - w_hi.astype(F32)).astype(BF16)

    rwkv_params = (
        mu[:, 0:C_RWKV], mu[:, C_RWKV:2 * C_RWKV], mu[:, 2 * C_RWKV:3 * C_RWKV], mu_lo,
        row(rwkv_w0[d]), *hi_lo(rwkv_w2[d], LANES), row(rwkv_a0[d]), *hi_lo(rwkv_a2[d], LANES),
        *hi_lo(rwkv_g2[d], 2 * LANES),
        row(rwkv_k_k[d]), row(rwkv_k_a[d]), row(rwkv_r_k[d]), row(rwkv_ln_g[d]), row(rwkv_ln_b[d]),
        e_mat, e_mat.T,
    )
    wts = dict(w_in=w_in_p, b_in=b_in_p, conv_w=conv_w[d], conv_b=row(conv_b[d]),
               conv_ln_g=row(conv_ln_g[d]), conv_ln_b=row(conv_ln_b[d]), rwkv_params=rwkv_params)

    zero_conv = jnp.zeros((n_p, CONV_WIDTH - 1, C_CONV), x_prompt.dtype)
    zero_shift = jnp.zeros((n_p, 1, N_SHIFT), x_prompt.dtype)
    zero_wkv = jnp.zeros((n_p, N_HEADS, HEAD, HEAD), state_wkv.dtype)
    expert_w = (w_gate[d].reshape(-1, D_FF), w_up[d].reshape(-1, D_FF), w_down[d].reshape(-1, D_MODEL))
    c_p, y_p, conv_p, shift_p, wkv_p, expert_w = _mixer_group(x_prompt, zero_conv, zero_shift, zero_wkv, wts,
                                                              expert_w)
    wg_b = expert_w[0].reshape(N_EXPERTS, D_MODEL, D_FF)
    wu_b = expert_w[1].reshape(N_EXPERTS, D_MODEL, D_FF)
    wd_b = expert_w[2].reshape(N_EXPERTS, D_FF, D_MODEL)
    c_s, y_s, conv_s, shift_s, wkv_s, _ = _mixer_group(x_sample, state_conv[d], state_shift[d], state_wkv[d], wts)

    w_out_b = w_out[d].astype(BF16)
    rw = _pad_cols(router_w[d], ROUTER_PAD)
    rw_hi = rw.astype(BF16)
    rw_lo = (rw - rw_hi.astype(F32)).astype(BF16)
    rb = jnp.concatenate([router_b[d], jnp.full((ROUTER_PAD - N_EXPERTS,), -jnp.inf, F32)])[None, :]

    n_tok_p = n_p * t_p
    n_tok_s = n_s * t_s
    n_tok = n_tok_p + n_tok_s

    def out_proj(c2, y2, x3, x1_base, row0):
        n = c2.shape[0]
        tm = math.gcd(_pick(n, (OUT_TM, 256, 128)), row0) if row0 else _pick(n, (OUT_TM, 256, 128))
        return _out_proj(c2, y2, x3.reshape(n, D_MODEL), w_out_b[:C_CONV], w_out_b[C_CONV:], row(ln1_g[d]),
                         row(ln1_b[d]), rw_hi, rw_lo, rb, tm, n_tok, x1_base, row0)

    x1, idx_p, gate_p = out_proj(c_p, y_p, x_prompt, None, 0)
    x1, idx_s, gate_s = out_proj(c_s, y_s, x_sample, x1, n_tok_p)
    top_idx = jnp.concatenate([idx_p[:, :TOP_K], idx_s[:, :TOP_K]], axis=0)
    dest, row_tok, block_exp, n_used, n_blocks = _route(top_idx, n_tok)
    yb = _experts(block_exp, n_used, row_tok.reshape(n_blocks, 1, MOE_TM), x1, wg_b, b_gate[d], wu_b, b_up[d],
                  wd_b, b_down[d])

    def combine(dest_g, gate_g, n, row0):
        tc = _pick(n, (128, 64, 32, 16, 8))
        tc = math.gcd(tc, row0) if row0 else tc
        return _combine(dest_g.reshape(n // tc, 1, tc * TOP_K), gate_g, x1, row(ln2_g[d]), row(ln2_b[d]), yb, tc,
                        row0)

    y_prompt = combine(dest[:n_tok_p * TOP_K], gate_p, n_tok_p, 0).reshape(n_p, t_p, D_MODEL)
    y_sample = combine(dest[n_tok_p * TOP_K:], gate_s, n_tok_s, n_tok_p).reshape(n_s, t_s, D_MODEL)
    return (y_prompt, y_sample, conv_p[None], shift_p[None], wkv_p[None], conv_s[None], shift_s[None], wkv_s[None])
```

```python
import functools
import math

import jax
import jax.numpy as jnp
from jax import lax
from jax.experimental import pallas as pl
from jax.experimental.pallas import tpu as pltpu

F32 = jnp.float32
BF16 = jnp.bfloat16

D_MODEL = 2048
C_CONV = 1024
C_RWKV = 1024
HEAD = 64
N_HEADS = C_RWKV // HEAD
CONV_WIDTH = 31
R_DECAY = 64
R_ICLR = 64
R_GATE = 160
N_SHIFT = 3 * C_RWKV + R_DECAY + R_ICLR + R_GATE
N_EXPERTS = 32
TOP_K = 4
D_FF = 2048
SWIGLU_LIMIT = 7.0
SWIGLU_ALPHA = 1.702
LN_EPS = 1e-5
GN_EPS = 64e-5
ALPHA = 2.0 ** 0.25

LANES = 128
SUBLANES = 8

HIST = 32
LORA_PAD = 512
P_PAD = 2 * C_CONV + 3 * C_RWKV + LORA_PAD
GROUP_HEADS = 2
PACK = GROUP_HEADS * HEAD
N_GROUPS = N_HEADS // GROUP_HEADS
MOE_TM = 512
MOE_NJ = 2
OUT_TM = 512
IN_TN = 512
RWKV_CHUNK = 64
RWKV_BLOCK = 128
ROUTER_PAD = LANES


def _dot(a, b, prec=1, dims=(((1,), (0,)), ((), ()))):
    if prec == 6:
        return lax.dot_general(a.astype(F32), b.astype(F32), dims, precision=lax.Precision.HIGHEST,
                               preferred_element_type=F32)
    d = lambda x, y: lax.dot_general(x, y, dims, preferred_element_type=F32)
    if prec == 1:
        return d(a.astype(BF16), b.astype(BF16))
    a_hi = a.astype(BF16)
    a_lo = (a - a_hi.astype(F32)).astype(BF16)
    b_hi = b.astype(BF16)
    b_lo = (b - b_hi.astype(F32)).astype(BF16)
    return d(a_hi, b_hi) + d(a_hi, b_lo) + d(a_lo, b_hi)


_NT = (((1,), (1,)), ((), ()))
_BNN = (((2,), (1,)), ((0,), (0,)))
_BNT = (((2,), (2,)), ((0,), (0,)))


def _split3(x):
    p1 = x.astype(BF16)
    r1 = x - p1.astype(F32)
    p2 = r1.astype(BF16)
    p3 = (r1 - p2.astype(F32)).astype(BF16)
    return p1, p2, p3


def _dot_exact_rhs(x, m_bf16):
    d = lambda a: jnp.dot(a, m_bf16, preferred_element_type=F32)
    p1, p2, p3 = _split3(x)
    return d(p1) + d(p2) + d(p3)


def _dot_exact_lhs(m_bf16, x):
    d = lambda a: jnp.dot(m_bf16, a, preferred_element_type=F32)
    p1, p2, p3 = _split3(x)
    return d(p1) + d(p2) + d(p3)


def _dot_split_w(x, w_hi, w_lo):
    x_hi = x.astype(BF16)
    x_lo = (x - x_hi.astype(F32)).astype(BF16)
    d = lambda a, b: jnp.dot(a, b, preferred_element_type=F32)
    return d(x_hi, w_hi) + d(x_hi, w_lo) + d(x_lo, w_hi)


def _mm_bias_kernel(x_ref, w_ref, b_ref, o_ref, xb_ref):
    @pl.when(pl.program_id(1) == 0)
    def _():
        xb_ref[...] = x_ref[...].astype(BF16)

    o_ref[...] = jnp.dot(xb_ref[...], w_ref[...], preferred_element_type=F32) + b_ref[...]


def _in_proj(x, w_bf16, b, tm, tn):
    n, k = x.shape
    p = w_bf16.shape[1]
    return pl.pallas_call(
        _mm_bias_kernel,
        out_shape=jax.ShapeDtypeStruct((n, p), F32),
        grid=(n // tm, p // tn),
        in_specs=[pl.BlockSpec((tm, k), lambda i, j: (i, 0)),
                  pl.BlockSpec((k, tn), lambda i, j: (0, j)),
                  pl.BlockSpec((1, tn), lambda i, j: (0, j))],
        out_specs=pl.BlockSpec((tm, tn), lambda i, j: (i, j)),
        scratch_shapes=[pltpu.VMEM((tm, k), BF16)],
        compiler_params=pltpu.CompilerParams(dimension_semantics=("arbitrary", "arbitrary")),
        name="in_proj",
    )(x, w_bf16, b)


def _conv_kernel(val_ref, gate_ref, hist_ref, w_ref, cb_ref, g_ref, b_ref, c_ref, tail_ref, ext_ref, sh_ref):
    t = pl.program_id(1)
    tt = val_ref.shape[1]

    @pl.when(t == 0)
    def _():
        ext_ref[0:HIST, :] = hist_ref[0]

    u = val_ref[0] * jax.nn.sigmoid(gate_ref[0])
    ext_ref[HIST:HIST + tt, :] = u
    span = tt + HIST - SUBLANES
    for s in range(1, SUBLANES):
        sh_ref[s, 0:span, :] = ext_ref[s:s + span, :]
    off = HIST - (CONV_WIDTH - 1)
    acc = jnp.broadcast_to(cb_ref[...], (tt, C_CONV))
    for j in range(CONV_WIDTH):
        base = (off + j) // SUBLANES * SUBLANES
        s = (off + j) % SUBLANES
        src = ext_ref[base:base + tt, :] if s == 0 else sh_ref[s, base:base + tt, :]
        acc = acc + w_ref[j:j + 1, :] * src
    mu = jnp.mean(acc, axis=-1, keepdims=True)
    xc = acc - mu
    var = jnp.mean(xc * xc, axis=-1, keepdims=True)
    y = xc * lax.rsqrt(var + LN_EPS) * g_ref[...] + b_ref[...]
    c_ref[0] = (y * jax.nn.sigmoid(y)).astype(c_ref.dtype)
    tail = ext_ref[tt:tt + HIST, :]
    ext_ref[0:HIST, :] = tail
    tail_ref[0] = tail


def _conv_module(proj3, hist, conv_w, conv_b, ln_g, ln_b, tt):
    bsz, t_len, _ = proj3.shape
    nblk = C_CONV // C_CONV
    return pl.pallas_call(
        _conv_kernel,
        out_shape=(jax.ShapeDtypeStruct((bsz, t_len, C_CONV), BF16),
                   jax.ShapeDtypeStruct((bsz, HIST, C_CONV), F32)),
        grid=(bsz, t_len // tt),
        in_specs=[pl.BlockSpec((1, tt, C_CONV), lambda b, t: (b, t, 0)),
                  pl.BlockSpec((1, tt, C_CONV), lambda b, t: (b, t, nblk)),
                  pl.BlockSpec((1, HIST, C_CONV), lambda b, t: (b, 0, 0)),
                  pl.BlockSpec((CONV_WIDTH, C_CONV), lambda b, t: (0, 0)),
                  pl.BlockSpec((1, C_CONV), lambda b, t: (0, 0)),
                  pl.BlockSpec((1, C_CONV), lambda b, t: (0, 0)),
                  pl.BlockSpec((1, C_CONV), lambda b, t: (0, 0))],
        out_specs=(pl.BlockSpec((1, tt, C_CONV), lambda b, t: (b, t, 0)),
                   pl.BlockSpec((1, HIST, C_CONV), lambda b, t: (b, 0, 0))),
        scratch_shapes=[pltpu.VMEM((HIST + tt, C_CONV), F32),
                        pltpu.VMEM((SUBLANES, HIST + tt, C_CONV), F32)],
        compiler_params=pltpu.CompilerParams(dimension_semantics=("arbitrary", "arbitrary")),
        name="conv_module",
    )(proj3, proj3, hist, conv_w, conv_b, ln_g, ln_b)


PREC_CHUNK = 1
PREC_STATE = 1


def _seg_sum(x, e_ref, et_ref):
    return _dot_exact_rhs(_dot_exact_rhs(x, e_ref[...]), et_ref[...])


def _rwkv_kernel(chunk, n_cast, *refs):
    n_in = 28
    (r_ref, k_ref, v_ref, lo_ref, shr_ref, shk_ref, shv_ref, shlo_ref, st0_ref,
     mur_ref, muk_ref, muv_ref, mulo_ref, w0_ref, w2h_ref, w2l_ref, a0_ref, a2h_ref, a2l_ref,
     g2h_ref, g2l_ref, kkw_ref, kaw_ref, rkw_ref, lng_ref, lnb_ref, e_ref, et_ref) = refs[:n_in]
    cast_in = refs[n_in:n_in + n_cast]
    y_ref, stout_ref, shout_ref = refs[n_in + n_cast:n_in + n_cast + 3]
    cast_out = refs[n_in + n_cast + 3:n_in + 2 * n_cast + 3]
    st_sc, pr_sc, pk_sc, pv_sc, plo_sc = refs[n_in + 2 * n_cast + 3:]

    for src, dst in zip(cast_in, cast_out):
        dst[...] = src[...].astype(dst.dtype)

    c = pl.program_id(1)
    n_chunks = pl.num_programs(1)
    Tb = r_ref.shape[1]
    L = chunk
    n_sub = Tb // L
    GL = GROUP_HEADS * L
    log2l = int(math.log2(L))

    @pl.when(c == 0)
    def _():
        st_sc[...] = st0_ref[0]
        pr_sc[...] = shr_ref[0]
        pk_sc[...] = shk_ref[0]
        pv_sc[...] = shv_ref[0]
        plo_sc[...] = shlo_ref[0]

    def token_shift(x_ref, prev_sc, mu_ref):
        x = x_ref[0]
        row = lax.broadcasted_iota(jnp.int32, x.shape, 0)
        xprev = jnp.where(row == 0, jnp.broadcast_to(prev_sc[...], x.shape), pltpu.roll(x, 1, 0))
        prev_sc[...] = x[Tb - 1:Tb, :]
        return x + mu_ref[...] * (xprev - x)

    r = token_shift(r_ref, pr_sc, mur_ref)
    k = token_shift(k_ref, pk_sc, muk_ref)
    v = token_shift(v_ref, pv_sc, muv_ref)
    lo = token_shift(lo_ref, plo_sc, mulo_ref)
    xw = lo[:, 0:LANES]
    xa = lo[:, LANES:2 * LANES]
    xg = lo[:, 2 * LANES:LORA_PAD]

    u_dec = w0_ref[...] + _dot_split_w(jnp.tanh(xw), w2h_ref[...], w2l_ref[...])
    logw = (-math.exp(-0.5)) * jax.nn.sigmoid(u_dec)
    a = jax.nn.sigmoid(a0_ref[...] + _dot_split_w(xa, a2h_ref[...], a2l_ref[...]))
    g = _dot_split_w(jax.nn.sigmoid(xg), g2h_ref[...], g2l_ref[...])

    kk = k * kkw_ref[...]
    nrm = jnp.sqrt(_seg_sum(kk * kk, e_ref, et_ref))
    kappa = kk / jnp.maximum(nrm, 1e-12)
    k2 = k * (1.0 + (a - 1.0) * kaw_ref[...])
    bvec = kappa * a
    bonus = _seg_sum(r * k2 * rkw_ref[...], e_ref, et_ref) * v

    ti = lax.broadcasted_iota(jnp.int32, (Tb, Tb), 0)
    tj = lax.broadcasted_iota(jnp.int32, (Tb, Tb), 1)
    tril = jnp.where((tj <= ti) & ((ti >> log2l) == (tj >> log2l)), 1.0, 0.0).astype(BF16)
    cum = _dot_exact_lhs(tril, logw)
    trow = lax.broadcasted_iota(jnp.int32, (Tb, C_RWKV), 0)
    cum_end = jnp.broadcast_to(cum[L - 1:L, :], (Tb, C_RWKV))
    for s in range(1, n_sub):
        cum_end = jnp.where(trow >= s * L, jnp.broadcast_to(cum[(s + 1) * L - 1:(s + 1) * L, :], (Tb, C_RWKV)),
                            cum_end)
    gam = jnp.exp(cum)
    ginv = jnp.exp(-cum)
    gprev = jnp.exp(cum - logw)
    gtail = jnp.exp(cum_end - cum)

    kt = kappa * gprev
    kinv = k2 * ginv
    binv = bvec * ginv
    rt = r * gam
    khat = k2 * gtail
    bhat = bvec * gtail

    rr = lax.broadcasted_iota(jnp.int32, (GL, GL), 0)
    cc = lax.broadcasted_iota(jnp.int32, (GL, GL), 1)
    same = (rr >> log2l) == (cc >> log2l)
    tpos = rr & (L - 1)
    jpos = cc & (L - 1)
    mask_s = same & (jpos < tpos)
    mask_i = same & (jpos <= tpos)
    eye = rr == cc
    srow = lax.broadcasted_iota(jnp.int32, (GL, PACK), 0)
    slane = lax.broadcasted_iota(jnp.int32, (GL, PACK), 1)
    bmask = (srow >> log2l) == (slane >> int(math.log2(HEAD)))
    drow = lax.broadcasted_iota(jnp.int32, (PACK, PACK), 0)
    dcol = lax.broadcasted_iota(jnp.int32, (PACK, PACK), 1)
    deye = drow == dcol

    lane_split = GL % LANES == 0

    def bdot(a_, b_, dims=_BNN):
        return lax.dot_general(a_.astype(BF16), b_.astype(BF16), dims, preferred_element_type=F32)

    def btrans(x):
        return jnp.stack([x[gi].T for gi in range(N_GROUPS)], axis=0)

    def chunk_step(row0, st):
        def stack(x, dtype=F32):
            xc = x[row0:row0 + L].astype(dtype)
            x3 = jnp.stack([xc[:, gi * PACK:(gi + 1) * PACK] for gi in range(N_GROUPS)], axis=0)
            return jnp.where(bmask[None], jnp.concatenate([x3] * GROUP_HEADS, axis=1), jnp.zeros((), dtype))

        kt_s = stack(kt, BF16)
        rt_s = stack(rt)
        binv_s = stack(binv, BF16)
        kinv_s = stack(kinv, BF16)
        v_s = stack(v, BF16)
        khat_s = stack(khat)
        bhat_s = stack(bhat)

        if lane_split:
            a_all = bdot(jnp.concatenate([kt_s, rt_s.astype(BF16)], axis=1),
                         jnp.concatenate([binv_s, kinv_s], axis=1), _BNT)
            a_parts = (a_all[:, :GL, :GL], a_all[:, :GL, GL:], a_all[:, GL:, :GL], a_all[:, GL:, GL:])
        else:
            a_parts = (bdot(kt_s, binv_s, _BNT), bdot(kt_s, kinv_s, _BNT),
                       bdot(rt_s, binv_s, _BNT), bdot(rt_s, kinv_s, _BNT))
        n_mat = jnp.where(mask_s[None], a_parts[0], 0.0)
        a_kk = jnp.where(mask_s[None], a_parts[1], 0.0)
        a_br = jnp.where(mask_i[None], a_parts[2], 0.0)
        a_kr = jnp.where(mask_i[None], a_parts[3], 0.0)

        p_mat = -n_mat
        t_mat = jnp.where(eye[None], 1.0, 0.0) + p_mat
        if log2l > 1:
            p_mat = bdot(p_mat, p_mat)
        for lvl in range(1, log2l):
            if lvl == log2l - 1:
                t_mat = t_mat + bdot(p_mat, t_mat)
            elif lane_split:
                both = bdot(p_mat, jnp.concatenate([p_mat, t_mat], axis=2))
                t_mat = t_mat + both[:, :, GL:]
                p_mat = both[:, :, :GL]
            else:
                t_mat = t_mat + bdot(p_mat, t_mat)
                p_mat = bdot(p_mat, p_mat)

        av = bdot(jnp.concatenate([a_kk, a_kr], axis=1), v_s)
        wu = bdot(t_mat, jnp.concatenate([kt_s, av[:, :GL].astype(BF16)], axis=2))
        br = bdot(a_br, wu)
        q_s = rt_s - br[:, :, :PACK]
        y0_s = av[:, GL:] - br[:, :, PACK:]
        bhat_t = btrans(bhat_s)
        khat_t = btrans(khat_s)
        gam_end = jnp.exp(cum[row0 + L - 1:row0 + L, :])
        gl3 = jnp.stack([gam_end[:, gi * PACK:(gi + 1) * PACK] for gi in range(N_GROUPS)], axis=0)
        bw = bdot(bhat_t, wu)
        m_mat = jnp.where(deye[None], jnp.broadcast_to(gl3, (N_GROUPS, PACK, PACK)), 0.0) - bw[:, :, :PACK]
        c_mat = bdot(khat_t, v_s) - bw[:, :, PACK:]

        qm = bdot(jnp.concatenate([q_s, m_mat], axis=1), st)
        ys = qm[:, :GL] + y0_s
        yg = ys[:, 0:L]
        for h in range(1, GROUP_HEADS):
            yg = yg + ys[:, h * L:(h + 1) * L]
        return jnp.concatenate([yg[gi] for gi in range(N_GROUPS)], axis=1), qm[:, GL:] + c_mat

    st = st_sc[...]
    y_chunks = []
    for s in range(n_sub):
        y_c, st = chunk_step(s * L, st)
        y_chunks.append(y_c)
    st_sc[...] = st
    y = y_chunks[0] if n_sub == 1 else jnp.concatenate(y_chunks, axis=0)

    inv_head = 1.0 / HEAD
    mu = _seg_sum(y, e_ref, et_ref) * inv_head
    yc = y - mu
    var = _seg_sum(yc * yc, e_ref, et_ref) * inv_head
    yn = yc * lax.rsqrt(var + GN_EPS) * lng_ref[...] + lnb_ref[...]
    y_ref[0] = ((yn + bonus) * g).astype(y_ref.dtype)

    @pl.when(c == n_chunks - 1)
    def _():
        stout_ref[0] = st_sc[...]
        shout_ref[0, :, 0:C_RWKV] = pr_sc[...]
        shout_ref[0, :, C_RWKV:2 * C_RWKV] = pk_sc[...]
        shout_ref[0, :, 2 * C_RWKV:3 * C_RWKV] = pv_sc[...]
        shout_ref[0, :, 3 * C_RWKV:3 * C_RWKV + LORA_PAD] = plo_sc[...]


def _rwkv_mix(proj3, shift_parts, st0, params, chunk, block, cast_arrays=()):
    bsz, t_len, _ = proj3.shape
    L = block
    n_steps = bsz * (t_len // L)
    n_cast = len(cast_arrays)
    rkv_blk0 = 2 * C_CONV // C_RWKV
    lora_blk = (2 * C_CONV + 3 * C_RWKV) // LORA_PAD
    row = lambda n: pl.BlockSpec((1, n), lambda b, c: (0, 0))
    full = lambda s: pl.BlockSpec(s, lambda b, c: tuple(0 for _ in s))
    sh = lambda n: pl.BlockSpec((1, 1, n), lambda b, c: (b, 0, 0))
    in_specs = [
        pl.BlockSpec((1, L, C_RWKV), lambda b, c: (b, c, rkv_blk0)),
        pl.BlockSpec((1, L, C_RWKV), lambda b, c: (b, c, rkv_blk0 + 1)),
        pl.BlockSpec((1, L, C_RWKV), lambda b, c: (b, c, rkv_blk0 + 2)),
        pl.BlockSpec((1, L, LORA_PAD), lambda b, c: (b, c, lora_blk)),
        sh(C_RWKV), sh(C_RWKV), sh(C_RWKV), sh(LORA_PAD),
        pl.BlockSpec((1, N_GROUPS, PACK, PACK), lambda b, c: (b, 0, 0, 0)),
        row(C_RWKV), row(C_RWKV), row(C_RWKV), row(LORA_PAD),
        row(C_RWKV), full((LANES, C_RWKV)), full((LANES, C_RWKV)),
        row(C_RWKV), full((LANES, C_RWKV)), full((LANES, C_RWKV)),
        full((2 * LANES, C_RWKV)), full((2 * LANES, C_RWKV)),
        row(C_RWKV), row(C_RWKV), row(C_RWKV), row(C_RWKV), row(C_RWKV),
        full((C_RWKV, LANES)), full((LANES, C_RWKV)),
    ]
    out_shape = (jax.ShapeDtypeStruct((bsz, t_len, C_RWKV), BF16),
                 jax.ShapeDtypeStruct((bsz, N_GROUPS, PACK, PACK), F32),
                 jax.ShapeDtypeStruct((bsz, 1, 3 * C_RWKV + LORA_PAD), F32))
    out_specs = (pl.BlockSpec((1, L, C_RWKV), lambda b, c: (b, c, 0)),
                 pl.BlockSpec((1, N_GROUPS, PACK, PACK), lambda b, c: (b, 0, 0, 0)),
                 pl.BlockSpec((1, 1, 3 * C_RWKV + LORA_PAD), lambda b, c: (b, 0, 0)))
    steps_per_b = t_len // L
    for arr in cast_arrays:
        rows, width = arr.shape
        win = rows // n_steps
        assert win * n_steps == rows
        spec = pl.BlockSpec((win, width), lambda b, c: (b * steps_per_b + c, 0))
        in_specs = in_specs + [spec]
        out_specs = out_specs + (spec,)
        out_shape = out_shape + (jax.ShapeDtypeStruct((rows, width), BF16),)
    return pl.pallas_call(
        functools.partial(_rwkv_kernel, chunk, n_cast),
        out_shape=out_shape,
        grid=(bsz, t_len // L),
        in_specs=in_specs,
        out_specs=out_specs,
        scratch_shapes=[pltpu.VMEM((N_GROUPS, PACK, PACK), F32),
                        pltpu.VMEM((1, C_RWKV), F32), pltpu.VMEM((1, C_RWKV), F32),
                        pltpu.VMEM((1, C_RWKV), F32), pltpu.VMEM((1, LORA_PAD), F32)],
        compiler_params=pltpu.CompilerParams(dimension_semantics=("arbitrary", "arbitrary")),
        name="rwkv7_mix",
    )(proj3, proj3, proj3, proj3, *shift_parts, st0, *params, *cast_arrays)


def _outproj_kernel(c_ref, y_ref, x_ref, wa_ref, wb_ref, g_ref, b_ref, rwh_ref, rwl_ref, rb_ref,
                    x1_ref, idx_ref, gate_ref):
    mix = (jnp.dot(c_ref[...], wa_ref[...], preferred_element_type=F32)
           + jnp.dot(y_ref[...], wb_ref[...], preferred_element_type=F32))
    h = ALPHA * x_ref[...] + mix
    mu = jnp.mean(h, axis=-1, keepdims=True)
    hc = h - mu
    var = jnp.mean(hc * hc, axis=-1, keepdims=True)
    x1 = hc * lax.rsqrt(var + LN_EPS) * g_ref[...] + b_ref[...]
    x1_ref[...] = x1
    logits = _dot_split_w(x1, rwh_ref[...], rwl_ref[...]) + rb_ref[...]
    lane = lax.broadcasted_iota(jnp.int32, logits.shape, 1)
    idx_out = jnp.zeros(logits.shape, jnp.int32)
    val_out = jnp.zeros(logits.shape, F32)
    vals = []
    for kk in range(TOP_K):
        m = jnp.max(logits, axis=-1, keepdims=True)
        sel = jnp.min(jnp.where(logits == m, lane, ROUTER_PAD), axis=-1, keepdims=True)
        vals.append(m)
        idx_out = jnp.where(lane == kk, sel, idx_out)
        logits = jnp.where(lane == sel, -jnp.inf, logits)
    exps = [jnp.exp(vv - vals[0]) for vv in vals]
    denom = exps[0]
    for ee in exps[1:]:
        denom = denom + ee
    for kk in range(TOP_K):
        val_out = jnp.where(lane == kk, exps[kk] / denom, val_out)
    idx_ref[...] = idx_out
    gate_ref[...] = val_out


def _outproj_into_kernel(base_ref, *refs):
    del base_ref
    _outproj_kernel(*refs)


def _out_proj(c2, y2, x2, wa, wb, ln_g, ln_b, rw_hi, rw_lo, rb, tm, n_total, x1_base=None, row0=0):
    n = x2.shape[0]
    blk0 = row0 // tm
    assert blk0 * tm == row0
    row = lambda w: pl.BlockSpec((1, w), lambda i: (0, 0))
    in_specs = [pl.BlockSpec((tm, C_CONV), lambda i: (i, 0)),
                pl.BlockSpec((tm, C_RWKV), lambda i: (i, 0)),
                pl.BlockSpec((tm, D_MODEL), lambda i: (i, 0)),
                pl.BlockSpec((C_CONV, D_MODEL), lambda i: (0, 0)),
                pl.BlockSpec((C_RWKV, D_MODEL), lambda i: (0, 0)),
                row(D_MODEL), row(D_MODEL),
                pl.BlockSpec((D_MODEL, ROUTER_PAD), lambda i: (0, 0)),
                pl.BlockSpec((D_MODEL, ROUTER_PAD), lambda i: (0, 0)),
                row(ROUTER_PAD)]
    args = (c2, y2, x2, wa, wb, ln_g, ln_b, rw_hi, rw_lo, rb)
    body, aliases = _outproj_kernel, {}
    if x1_base is not None:
        in_specs = [pl.BlockSpec(memory_space=pl.ANY)] + in_specs
        args = (x1_base,) + args
        body, aliases = _outproj_into_kernel, {0: 0}
    return pl.pallas_call(
        body,
        out_shape=(jax.ShapeDtypeStruct((n_total, D_MODEL), F32),
                   jax.ShapeDtypeStruct((n, ROUTER_PAD), jnp.int32),
                   jax.ShapeDtypeStruct((n, ROUTER_PAD), F32)),
        grid=(n // tm,),
        in_specs=in_specs,
        out_specs=(pl.BlockSpec((tm, D_MODEL), lambda i: (i + blk0, 0)),
                   pl.BlockSpec((tm, ROUTER_PAD), lambda i: (i, 0)),
                   pl.BlockSpec((tm, ROUTER_PAD), lambda i: (i, 0))),
        input_output_aliases=aliases,
        compiler_params=pltpu.CompilerParams(dimension_semantics=("arbitrary",)),
        name="out_proj_ln_router",
    )(*args)


DMA_PRIORITIES = 2


def _expert_kernel(bexp_ref, nused_ref, tokc_ref, tokn_ref, x_hbm, wg_ref, bg_ref, wu_ref, bu_ref, wd_ref, bd_ref,
                   o_ref, xg_ref, xb_ref, sems):
    del bexp_ref
    i = pl.program_id(0)
    j = pl.program_id(1)
    nj = pl.num_programs(1)
    n_used = nused_ref[0]
    rows = xb_ref.shape[0]
    slot = lax.rem(i, 2)

    def row_copy(tok_ref, s, g, sub):
        tok = tok_ref[0, 0, g * SUBLANES + sub]
        return pltpu.make_async_copy(x_hbm.at[pl.ds(tok, 1)], xg_ref.at[s, g, pl.ds(sub, 1)], sems.at[s])

    def issue(tok_ref, s, g0, n_groups):
        def body(g, carry):
            for sub in range(SUBLANES):
                row_copy(tok_ref, s, g0 + g, sub).start(priority=sub % DMA_PRIORITIES)
            return carry

        lax.fori_loop(0, n_groups, body, 0)

    groups = rows // SUBLANES

    @pl.when((i == 0) & (j == 0) & (n_used > 0))
    def _():
        issue(tokc_ref, 0, 0, groups)

    @pl.when(i + 1 < n_used)
    def _():
        per_step = groups // MOE_NJ
        issue(tokn_ref, 1 - slot, j * per_step, per_step)

    @pl.when(i < n_used)
    def _():
        @pl.when(j == 0)
        def _():
            def drain(g, carry):
                for sub in range(SUBLANES):
                    row_copy(tokc_ref, slot, g, sub).wait()
                return carry

            lax.fori_loop(0, groups, drain, 0)
            xb_ref[...] = xg_ref[slot].reshape(rows, D_MODEL).astype(BF16)
            o_ref[...] = jnp.broadcast_to(bd_ref[0], o_ref.shape)

        x = xb_ref[...]
        gate = jnp.minimum(jnp.dot(x, wg_ref[0], preferred_element_type=F32) + bg_ref[0, pl.ds(j, 1), :],
                           SWIGLU_LIMIT)
        up = jnp.clip(jnp.dot(x, wu_ref[0], preferred_element_type=F32) + bu_ref[0, pl.ds(j, 1), :],
                      -SWIGLU_LIMIT, SWIGLU_LIMIT)
        hmid = (up + 1.0) * gate * jax.nn.sigmoid(SWIGLU_ALPHA * gate)
        o_ref[...] += jnp.dot(hmid.astype(BF16), wd_ref[0], preferred_element_type=F32)

    @pl.when((i >= nused_ref[0]) & (j == nj - 1))
    def _():
        o_ref[...] = jnp.zeros_like(o_ref)


def _experts(block_exp, n_used, row_tok3, x, wg, bg, wu, bu, wd, bd):
    nb, _, rows = row_tok3.shape
    n_exp = wg.shape[0]
    nj = MOE_NJ
    tf = D_FF // nj

    def jsel(i, j, nu):
        return jnp.where(i < nu[0], j, nj - 1)

    tok_spec = lambda off: pl.BlockSpec((1, 1, rows), lambda i, j, be, nu: (jnp.minimum(i + off, nb - 1), 0, 0),
                                        memory_space=pltpu.SMEM)
    grid_spec = pltpu.PrefetchScalarGridSpec(
        num_scalar_prefetch=2,
        grid=(nb, nj),
        in_specs=[tok_spec(0), tok_spec(1),
                  pl.BlockSpec(memory_space=pl.ANY),
                  pl.BlockSpec((1, D_MODEL, tf), lambda i, j, be, nu: (be[i], 0, jsel(i, j, nu))),
                  pl.BlockSpec((1, nj, tf), lambda i, j, be, nu: (be[i], 0, 0)),
                  pl.BlockSpec((1, D_MODEL, tf), lambda i, j, be, nu: (be[i], 0, jsel(i, j, nu))),
                  pl.BlockSpec((1, nj, tf), lambda i, j, be, nu: (be[i], 0, 0)),
                  pl.BlockSpec((1, tf, D_MODEL), lambda i, j, be, nu: (be[i], jsel(i, j, nu), 0)),
                  pl.BlockSpec((1, 1, D_MODEL), lambda i, j, be, nu: (be[i], 0, 0))],
        out_specs=pl.BlockSpec((rows, D_MODEL), lambda i, j, be, nu: (i, 0)),
        scratch_shapes=[pltpu.VMEM((2, rows // SUBLANES, SUBLANES, D_MODEL), x.dtype),
                        pltpu.VMEM((rows, D_MODEL), BF16), pltpu.SemaphoreType.DMA((2,))],
    )
    return pl.pallas_call(
        _expert_kernel,
        out_shape=jax.ShapeDtypeStruct((nb * rows, D_MODEL), F32),
        grid_spec=grid_spec,
        compiler_params=pltpu.CompilerParams(dimension_semantics=("arbitrary", "arbitrary")),
        name="moe_experts",
    )(block_exp, n_used, row_tok3, row_tok3, x, wg, bg.reshape(n_exp, nj, tf), wu, bu.reshape(n_exp, nj, tf),
      wd, bd.reshape(n_exp, 1, D_MODEL))


def _combine_kernel(dest_ref, gate_ref, x1_ref, g_ref, b_ref, yb_hbm, o_ref, buf_ref, sem):
    tc = x1_ref.shape[0]

    def row_copy(d, kk, g, sub):
        return pltpu.make_async_copy(yb_hbm.at[pl.ds(d, 1)], buf_ref.at[kk, g, pl.ds(sub, 1)], sem)

    def issue(g, carry):
        for sub in range(SUBLANES):
            for kk in range(TOP_K):
                d = dest_ref[0, 0, (g * SUBLANES + sub) * TOP_K + kk]
                row_copy(d, kk, g, sub).start(priority=kk % DMA_PRIORITIES)
        return carry

    lax.fori_loop(0, tc // SUBLANES, issue, 0)

    def drain(g, carry):
        for sub in range(SUBLANES):
            for kk in range(TOP_K):
                row_copy(0, kk, g, sub).wait()
        return carry

    lax.fori_loop(0, tc // SUBLANES, drain, 0)

    gates = gate_ref[...]
    moe = gates[:, 0:1] * buf_ref[0].reshape(tc, D_MODEL)
    for kk in range(1, TOP_K):
        moe = moe + gates[:, kk:kk + 1] * buf_ref[kk].reshape(tc, D_MODEL)
    h = ALPHA * x1_ref[...] + moe
    mu = jnp.mean(h, axis=-1, keepdims=True)
    hc = h - mu
    var = jnp.mean(hc * hc, axis=-1, keepdims=True)
    o_ref[...] = hc * lax.rsqrt(var + LN_EPS) * g_ref[...] + b_ref[...]


def _combine(dest3, gates, x1, ln_g, ln_b, yb, tc, row0=0):
    n = dest3.shape[0] * tc
    blk0 = row0 // tc
    assert blk0 * tc == row0
    return pl.pallas_call(
        _combine_kernel,
        out_shape=jax.ShapeDtypeStruct((n, D_MODEL), F32),
        grid=(n // tc,),
        in_specs=[pl.BlockSpec((1, 1, tc * TOP_K), lambda i: (i, 0, 0), memory_space=pltpu.SMEM),
                  pl.BlockSpec((tc, ROUTER_PAD), lambda i: (i, 0)),
                  pl.BlockSpec((tc, D_MODEL), lambda i: (i + blk0, 0)),
                  pl.BlockSpec((1, D_MODEL), lambda i: (0, 0)),
                  pl.BlockSpec((1, D_MODEL), lambda i: (0, 0)),
                  pl.BlockSpec(memory_space=pl.ANY)],
        out_specs=pl.BlockSpec((tc, D_MODEL), lambda i: (i, 0)),
        scratch_shapes=[pltpu.VMEM((TOP_K, tc // SUBLANES, SUBLANES, D_MODEL), F32), pltpu.SemaphoreType.DMA],
        compiler_params=pltpu.CompilerParams(dimension_semantics=("arbitrary",)),
        name="moe_combine_ln2",
    )(dest3, gates, x1, ln_g, ln_b, yb)


def _pad_cols(w, width):
    return jnp.pad(w, ((0, 0), (0, width - w.shape[1])))


def _pad_rows(w, height):
    return jnp.pad(w, ((0, height - w.shape[0]), (0, 0)))


def _split_lora_cols(w):
    xw = w[..., 0:R_DECAY]
    xa = w[..., R_DECAY:R_DECAY + R_ICLR]
    xg = w[..., R_DECAY + R_ICLR:]
    pad = lambda x, n: jnp.pad(x, [(0, 0)] * (x.ndim - 1) + [(0, n - x.shape[-1])])
    return jnp.concatenate([pad(xw, LANES), pad(xa, LANES), pad(xg, 2 * LANES)], axis=-1)


def _pick(n, prefs):
    for p in prefs:
        if n % p == 0:
            return p
    return n


def _mixer_group(x, conv_buf, shift_buf, wkv_state, wts, cast_along=()):
    bsz, t_len, _ = x.shape
    n = bsz * t_len
    proj = _in_proj(x.reshape(n, D_MODEL), wts["w_in"], wts["b_in"], _pick(n, (1024, 512, 256, 128)), IN_TN)
    proj3 = proj.reshape(bsz, t_len, P_PAD)

    hist = jnp.pad(conv_buf, ((0, 0), (HIST - (CONV_WIDTH - 1), 0), (0, 0)))
    c, tail = _conv_module(proj3, hist, wts["conv_w"], wts["conv_b"], wts["conv_ln_g"], wts["conv_ln_b"],
                           _pick(t_len, (128, 64, 32, 16, 8)))
    new_conv = tail[:, HIST - (CONV_WIDTH - 1):, :]

    sh_rkv = shift_buf[:, :, :3 * C_RWKV]
    sh_lo = _split_lora_cols(shift_buf[:, :, 3 * C_RWKV:])
    shift_parts = (sh_rkv[:, :, 0:C_RWKV], sh_rkv[:, :, C_RWKV:2 * C_RWKV], sh_rkv[:, :, 2 * C_RWKV:], sh_lo)
    st_t = jnp.swapaxes(wkv_state, -1, -2).reshape(bsz, N_GROUPS, GROUP_HEADS, HEAD, HEAD)
    eye_h = jnp.eye(GROUP_HEADS, dtype=F32)
    st0 = jnp.einsum("bghkv,hj->bghkjv", st_t, eye_h).reshape(bsz, N_GROUPS, PACK, PACK)
    chunk = _pick(t_len, (RWKV_CHUNK, 32, 16))
    block = RWKV_BLOCK if t_len % RWKV_BLOCK == 0 else chunk
    n_steps = bsz * (t_len // block)
    ride = tuple(w for w in cast_along if w.shape[0] % n_steps == 0 and (w.shape[0] // n_steps) % 16 == 0)
    yb, st_out, sh_out, *cast_done = _rwkv_mix(proj3, shift_parts, st0, wts["rwkv_params"], chunk, block, ride)
    if len(ride) != len(cast_along):
        cast_done = [w.astype(BF16) for w in cast_along]
    st5 = st_out.reshape(bsz, N_GROUPS, GROUP_HEADS, HEAD, GROUP_HEADS, HEAD)
    st_diag = jnp.einsum("bghkhv->bghkv", st5)
    new_wkv = jnp.swapaxes(st_diag, -1, -2).reshape(bsz, N_HEADS, HEAD, HEAD)
    lo = sh_out[:, :, 3 * C_RWKV:]
    new_shift = jnp.concatenate([sh_out[:, :, :3 * C_RWKV], lo[:, :, 0:R_DECAY], lo[:, :, LANES:LANES + R_ICLR],
                                 lo[:, :, 2 * LANES:2 * LANES + R_GATE]], axis=-1)
    return c.reshape(n, C_CONV), yb.reshape(n, C_RWKV), new_conv, new_shift, new_wkv, cast_done


def _route(top_idx, n_tok):
    n_assign = n_tok * TOP_K
    flat_e = top_idx.reshape(-1)
    onehot = (flat_e[:, None] == jnp.arange(N_EXPERTS, dtype=jnp.int32)[None, :]).astype(jnp.int32)
    csum = jnp.cumsum(onehot, axis=0)
    rank = jnp.take_along_axis(csum, flat_e[:, None], axis=1)[:, 0] - 1
    counts = csum[-1]
    padded = (counts + MOE_TM - 1) // MOE_TM * MOE_TM
    seg_end = jnp.cumsum(padded)
    seg_start = seg_end - padded
    dest = (seg_start[flat_e] + rank).astype(jnp.int32)
    n_rows = (n_assign + N_EXPERTS * (MOE_TM - 1) + MOE_TM - 1) // MOE_TM * MOE_TM
    n_blocks = n_rows // MOE_TM
    block_start = jnp.arange(n_blocks, dtype=jnp.int32) * MOE_TM
    block_exp = jnp.minimum(jnp.sum((seg_end[None, :] <= block_start[:, None]).astype(jnp.int32), axis=1),
                            N_EXPERTS - 1).astype(jnp.int32)
    order = jnp.argsort(flat_e, stable=True).astype(jnp.int32)
    start = jnp.cumsum(counts) - counts
    local = jnp.arange(MOE_TM, dtype=jnp.int32)[None, :] + (block_start - seg_start[block_exp])[:, None]
    valid = local < counts[block_exp][:, None]
    pos = jnp.clip(start[block_exp][:, None] + local, 0, n_assign - 1)
    row_tok = jnp.where(valid, order[pos.reshape(-1)].reshape(n_blocks, MOE_TM) // TOP_K, 0).reshape(-1)
    n_used = (seg_end[-1] // MOE_TM).astype(jnp.int32).reshape(1)
    return dest, row_tok, block_exp, n_used, n_blocks


def kernel(x_prompt, x_sample, state_conv, state_shift, state_wkv, w_in, b_in, mu_shift, conv_w, conv_b,
           conv_ln_g, conv_ln_b, rwkv_w0, rwkv_w2, rwkv_a0, rwkv_a2, rwkv_g2, rwkv_k_k, rwkv_k_a, rwkv_r_k,
           rwkv_ln_g, rwkv_ln_b, w_out, ln1_g, ln1_b, router_w, router_b, w_gate, b_gate, w_up, b_up,
           w_down, b_down, ln2_g, ln2_b):
    assert w_in.shape[0] == 1, "single layer"
    d = 0
    row = lambda v: v.reshape(1, -1)
    n_p, t_p, _ = x_prompt.shape
    n_s, t_s, _ = x_sample.shape

    w_rkv = w_in[d][:, 2 * C_CONV:2 * C_CONV + 3 * C_RWKV]
    w_lo = _split_lora_cols(w_in[d][:, 2 * C_CONV + 3 * C_RWKV:])
    w_in_p = jnp.concatenate([w_in[d][:, :2 * C_CONV], w_rkv, w_lo], axis=1).astype(BF16)
    b_in_p = jnp.concatenate([b_in[d][None, :2 * C_CONV], b_in[d][None, 2 * C_CONV:2 * C_CONV + 3 * C_RWKV],
                              _split_lora_cols(b_in[d][None, 2 * C_CONV + 3 * C_RWKV:])], axis=1)
    mu = mu_shift[d][None, :]
    mu_lo = _split_lora_cols(mu[:, 3 * C_RWKV:])
    head_of_lane = jnp.arange(C_RWKV, dtype=jnp.int32) // HEAD
    e_mat = (head_of_lane[:, None] == jnp.arange(LANES, dtype=jnp.int32)[None, :]).astype(BF16)

    def hi_lo(w, height):
        w = _pad_rows(w, height)
        w_hi = w.astype(BF16)
        return w_hi, (w - w_hi.astype(F32)).astype(BF16)

    rwkv_params = (
        mu[:, 0:C_RWKV], mu[:, C_RWKV:2 * C_RWKV], mu[:, 2 * C_RWKV:3 * C_RWKV], mu_lo,
        row(rwkv_w0[d]), *hi_lo(rwkv_w2[d], LANES), row(rwkv_a0[d]), *hi_lo(rwkv_a2[d], LANES),
        *hi_lo(rwkv_g2[d], 2 * LANES),
        row(rwkv_k_k[d]), row(rwkv_k_a[d]), row(rwkv_r_k[d]), row(rwkv_ln_g[d]), row(rwkv_ln_b[d]),
        e_mat, e_mat.T,
    )
    wts = dict(w_in=w_in_p, b_in=b_in_p, conv_w=conv_w[d], conv_b=row(conv_b[d]),
               conv_ln_g=row(conv_ln_g[d]), conv_ln_b=row(conv_ln_b[d]), rwkv_params=rwkv_params)

    zero_conv = jnp.zeros((n_p, CONV_WIDTH - 1, C_CONV), x_prompt.dtype)
    zero_shift = jnp.zeros((n_p, 1, N_SHIFT), x_prompt.dtype)
    zero_wkv = jnp.zeros((n_p, N_HEADS, HEAD, HEAD), state_wkv.dtype)
    expert_w = (w_gate[d].reshape(-1, D_FF), w_up[d].reshape(-1, D_FF), w_down[d].reshape(-1, D_MODEL))
    c_p, y_p, conv_p, shift_p, wkv_p, expert_w = _mixer_group(x_prompt, zero_conv, zero_shift, zero_wkv, wts,
                                                              expert_w)
    wg_b = expert_w[0].reshape(N_EXPERTS, D_MODEL, D_FF)
    wu_b = expert_w[1].reshape(N_EXPERTS, D_MODEL, D_FF)
    wd_b = expert_w[2].reshape(N_EXPERTS, D_FF, D_MODEL)
    c_s, y_s, conv_s, shift_s, wkv_s, _ = _mixer_group(x_sample, state_conv[d], state_shift[d], state_wkv[d], wts)

    w_out_b = w_out[d].astype(BF16)
    rw = _pad_cols(router_w[d], ROUTER_PAD)
    rw_hi = rw.astype(BF16)
    rw_lo = (rw - rw_hi.astype(F32)).astype(BF16)
    rb = jnp.concatenate([router_b[d], jnp.full((ROUTER_PAD - N_EXPERTS,), -jnp.inf, F32)])[None, :]

    n_tok_p = n_p * t_p
    n_tok_s = n_s * t_s
    n_tok = n_tok_p + n_tok_s

    def out_proj(c2, y2, x3, x1_base, row0):
        n = c2.shape[0]
        tm = math.gcd(_pick(n, (OUT_TM, 256, 128)), row0) if row0 else _pick(n, (OUT_TM, 256, 128))
        return _out_proj(c2, y2, x3.reshape(n, D_MODEL), w_out_b[:C_CONV], w_out_b[C_CONV:], row(ln1_g[d]),
                         row(ln1_b[d]), rw_hi, rw_lo, rb, tm, n_tok, x1_base, row0)

    x1, idx_p, gate_p = out_proj(c_p, y_p, x_prompt, None, 0)
    x1, idx_s, gate_s = out_proj(c_s, y_s, x_sample, x1, n_tok_p)
    top_idx = jnp.concatenate([idx_p[:, :TOP_K], idx_s[:, :TOP_K]], axis=0)
    dest, row_tok, block_exp, n_used, n_blocks = _route(top_idx, n_tok)
    yb = _experts(block_exp, n_used, row_tok.reshape(n_blocks, 1, MOE_TM), x1, wg_b, b_gate[d], wu_b, b_up[d],
                  wd_b, b_down[d])

    def combine(dest_g, gate_g, n, row0):
        tc = _pick(n, (128, 64, 32, 16, 8))
        tc = math.gcd(tc, row0) if row0 else tc
        return _combine(dest_g.reshape(n // tc, 1, tc * TOP_K), gate_g, x1, row(ln2_g[d]), row(ln2_b[d]), yb, tc,
                        row0)

    y_prompt = combine(dest[:n_tok_p * TOP_K], gate_p, n_tok_p, 0).reshape(n_p, t_p, D_MODEL)
    y_sample = combine(dest[n_tok_p * TOP_K:], gate_s, n_tok_s, n_tok_p).reshape(n_s, t_s, D_MODEL)
    return (y_prompt, y_sample, conv_p[None], shift_p[None], wkv_p[None], conv_s[None], shift_s[None], wkv_s[None])
```

```python
import functools
import math

import jax
import jax.numpy as jnp
from jax import lax
from jax.experimental import pallas as pl
from jax.experimental.pallas import tpu as pltpu

F32 = jnp.float32
BF16 = jnp.bfloat16

D_MODEL = 2048
C_CONV = 1024
C_RWKV = 1024
HEAD = 64
N_HEADS = C_RWKV // HEAD
CONV_WIDTH = 31
R_DECAY = 64
R_ICLR = 64
R_GATE = 160
N_SHIFT = 3 * C_RWKV + R_DECAY + R_ICLR + R_GATE
N_EXPERTS = 32
TOP_K = 4
D_FF = 2048
SWIGLU_LIMIT = 7.0
SWIGLU_ALPHA = 1.702
LN_EPS = 1e-5
GN_EPS = 64e-5
ALPHA = 2.0 ** 0.25

LANES = 128
SUBLANES = 8

HIST = 32
LORA_PAD = 512
P_PAD = 2 * C_CONV + 3 * C_RWKV + LORA_PAD
GROUP_HEADS = 2
PACK = GROUP_HEADS * HEAD
N_GROUPS = N_HEADS // GROUP_HEADS
MOE_TM = 512
MOE_NJ = 2
OUT_TM = 512
IN_TN = P_PAD // 2
IN_TM = 512
RWKV_CHUNK = 64
RWKV_BLOCK = 128
ROUTER_PAD = LANES


def _dot(a, b, prec=1, dims=(((1,), (0,)), ((), ()))):
    if prec == 6:
        return lax.dot_general(a.astype(F32), b.astype(F32), dims, precision=lax.Precision.HIGHEST,
                               preferred_element_type=F32)
    d = lambda x, y: lax.dot_general(x, y, dims, preferred_element_type=F32)
    if prec == 1:
        return d(a.astype(BF16), b.astype(BF16))
    a_hi = a.astype(BF16)
    a_lo = (a - a_hi.astype(F32)).astype(BF16)
    b_hi = b.astype(BF16)
    b_lo = (b - b_hi.astype(F32)).astype(BF16)
    return d(a_hi, b_hi) + d(a_hi, b_lo) + d(a_lo, b_hi)


_NT = (((1,), (1,)), ((), ()))
_BNN = (((2,), (1,)), ((0,), (0,)))
_BNT = (((2,), (2,)), ((0,), (0,)))


def _split3(x):
    p1 = x.astype(BF16)
    r1 = x - p1.astype(F32)
    p2 = r1.astype(BF16)
    p3 = (r1 - p2.astype(F32)).astype(BF16)
    return p1, p2, p3


def _dot_exact_rhs(x, m_bf16):
    d = lambda a: jnp.dot(a, m_bf16, preferred_element_type=F32)
    p1, p2, p3 = _split3(x)
    return d(p1) + d(p2) + d(p3)


def _dot_exact_lhs(m_bf16, x):
    d = lambda a: jnp.dot(m_bf16, a, preferred_element_type=F32)
    p1, p2, p3 = _split3(x)
    return d(p1) + d(p2) + d(p3)


def _dot_split_w(x, w_hi, w_lo):
    x_hi = x.astype(BF16)
    x_lo = (x - x_hi.astype(F32)).astype(BF16)
    d = lambda a, b: jnp.dot(a, b, preferred_element_type=F32)
    return d(x_hi, w_hi) + d(x_hi, w_lo) + d(x_lo, w_hi)


def _mm_bias_kernel(x_ref, w_ref, b_ref, o_ref, xb_ref):
    @pl.when(pl.program_id(1) == 0)
    def _():
        xb_ref[...] = x_ref[...].astype(BF16)

    o_ref[...] = jnp.dot(xb_ref[...], w_ref[...], preferred_element_type=F32) + b_ref[...]


def _in_proj(x, w_bf16, b, tm, tn):
    n, k = x.shape
    p = w_bf16.shape[1]
    return pl.pallas_call(
        _mm_bias_kernel,
        out_shape=jax.ShapeDtypeStruct((n, p), F32),
        grid=(n // tm, p // tn),
        in_specs=[pl.BlockSpec((tm, k), lambda i, j: (i, 0)),
                  pl.BlockSpec((k, tn), lambda i, j: (0, j)),
                  pl.BlockSpec((1, tn), lambda i, j: (0, j))],
        out_specs=pl.BlockSpec((tm, tn), lambda i, j: (i, j)),
        scratch_shapes=[pltpu.VMEM((tm, k), BF16)],
        compiler_params=pltpu.CompilerParams(dimension_semantics=("arbitrary", "arbitrary")),
        name="in_proj",
    )(x, w_bf16, b)


def _conv_kernel(val_ref, gate_ref, hist_ref, w_ref, cb_ref, g_ref, b_ref, c_ref, tail_ref, ext_ref, sh_ref):
    t = pl.program_id(1)
    tt = val_ref.shape[1]

    @pl.when(t == 0)
    def _():
        ext_ref[0:HIST, :] = hist_ref[0]

    u = val_ref[0] * jax.nn.sigmoid(gate_ref[0])
    ext_ref[HIST:HIST + tt, :] = u
    span = tt + HIST - SUBLANES
    for s in range(1, SUBLANES):
        sh_ref[s, 0:span, :] = ext_ref[s:s + span, :]
    off = HIST - (CONV_WIDTH - 1)
    acc = jnp.broadcast_to(cb_ref[...], (tt, C_CONV))
    for j in range(CONV_WIDTH):
        base = (off + j) // SUBLANES * SUBLANES
        s = (off + j) % SUBLANES
        src = ext_ref[base:base + tt, :] if s == 0 else sh_ref[s, base:base + tt, :]
        acc = acc + w_ref[j:j + 1, :] * src
    mu = jnp.mean(acc, axis=-1, keepdims=True)
    xc = acc - mu
    var = jnp.mean(xc * xc, axis=-1, keepdims=True)
    y = xc * lax.rsqrt(var + LN_EPS) * g_ref[...] + b_ref[...]
    c_ref[0] = (y * jax.nn.sigmoid(y)).astype(c_ref.dtype)
    tail = ext_ref[tt:tt + HIST, :]
    ext_ref[0:HIST, :] = tail
    tail_ref[0] = tail


def _conv_module(proj3, hist, conv_w, conv_b, ln_g, ln_b, tt):
    bsz, t_len, _ = proj3.shape
    nblk = C_CONV // C_CONV
    return pl.pallas_call(
        _conv_kernel,
        out_shape=(jax.ShapeDtypeStruct((bsz, t_len, C_CONV), BF16),
                   jax.ShapeDtypeStruct((bsz, HIST, C_CONV), F32)),
        grid=(bsz, t_len // tt),
        in_specs=[pl.BlockSpec((1, tt, C_CONV), lambda b, t: (b, t, 0)),
                  pl.BlockSpec((1, tt, C_CONV), lambda b, t: (b, t, nblk)),
                  pl.BlockSpec((1, HIST, C_CONV), lambda b, t: (b, 0, 0)),
                  pl.BlockSpec((CONV_WIDTH, C_CONV), lambda b, t: (0, 0)),
                  pl.BlockSpec((1, C_CONV), lambda b, t: (0, 0)),
                  pl.BlockSpec((1, C_CONV), lambda b, t: (0, 0)),
                  pl.BlockSpec((1, C_CONV), lambda b, t: (0, 0))],
        out_specs=(pl.BlockSpec((1, tt, C_CONV), lambda b, t: (b, t, 0)),
                   pl.BlockSpec((1, HIST, C_CONV), lambda b, t: (b, 0, 0))),
        scratch_shapes=[pltpu.VMEM((HIST + tt, C_CONV), F32),
                        pltpu.VMEM((SUBLANES, HIST + tt, C_CONV), F32)],
        compiler_params=pltpu.CompilerParams(dimension_semantics=("arbitrary", "arbitrary")),
        name="conv_module",
    )(proj3, proj3, hist, conv_w, conv_b, ln_g, ln_b)


PREC_CHUNK = 1
PREC_STATE = 1


def _seg_sum(x, e_ref, et_ref):
    return _dot_exact_rhs(_dot_exact_rhs(x, e_ref[...]), et_ref[...])


def _rwkv_kernel(chunk, n_cast, *refs):
    n_in = 28
    (r_ref, k_ref, v_ref, lo_ref, shr_ref, shk_ref, shv_ref, shlo_ref, st0_ref,
     mur_ref, muk_ref, muv_ref, mulo_ref, w0_ref, w2h_ref, w2l_ref, a0_ref, a2h_ref, a2l_ref,
     g2h_ref, g2l_ref, kkw_ref, kaw_ref, rkw_ref, lng_ref, lnb_ref, e_ref, et_ref) = refs[:n_in]
    cast_in = refs[n_in:n_in + n_cast]
    y_ref, stout_ref, shout_ref = refs[n_in + n_cast:n_in + n_cast + 3]
    cast_out = refs[n_in + n_cast + 3:n_in + 2 * n_cast + 3]
    st_sc, pr_sc, pk_sc, pv_sc, plo_sc = refs[n_in + 2 * n_cast + 3:]

    for src, dst in zip(cast_in, cast_out):
        dst[...] = src[...].astype(dst.dtype)

    c = pl.program_id(1)
    n_chunks = pl.num_programs(1)
    Tb = r_ref.shape[1]
    L = chunk
    n_sub = Tb // L
    GL = GROUP_HEADS * L
    log2l = int(math.log2(L))

    @pl.when(c == 0)
    def _():
        st_sc[...] = st0_ref[0]
        pr_sc[...] = shr_ref[0]
        pk_sc[...] = shk_ref[0]
        pv_sc[...] = shv_ref[0]
        plo_sc[...] = shlo_ref[0]

    def token_shift(x_ref, prev_sc, mu_ref):
        x = x_ref[0]
        row = lax.broadcasted_iota(jnp.int32, x.shape, 0)
        xprev = jnp.where(row == 0, jnp.broadcast_to(prev_sc[...], x.shape), pltpu.roll(x, 1, 0))
        prev_sc[...] = x[Tb - 1:Tb, :]
        return x + mu_ref[...] * (xprev - x)

    r = token_shift(r_ref, pr_sc, mur_ref)
    k = token_shift(k_ref, pk_sc, muk_ref)
    v = token_shift(v_ref, pv_sc, muv_ref)
    lo = token_shift(lo_ref, plo_sc, mulo_ref)
    xw = lo[:, 0:LANES]
    xa = lo[:, LANES:2 * LANES]
    xg = lo[:, 2 * LANES:LORA_PAD]

    u_dec = w0_ref[...] + _dot_split_w(jnp.tanh(xw), w2h_ref[...], w2l_ref[...])
    logw = (-math.exp(-0.5)) * jax.nn.sigmoid(u_dec)
    a = jax.nn.sigmoid(a0_ref[...] + _dot_split_w(xa, a2h_ref[...], a2l_ref[...]))
    g = _dot_split_w(jax.nn.sigmoid(xg), g2h_ref[...], g2l_ref[...])

    kk = k * kkw_ref[...]
    nrm = jnp.sqrt(_seg_sum(kk * kk, e_ref, et_ref))
    kappa = kk / jnp.maximum(nrm, 1e-12)
    k2 = k * (1.0 + (a - 1.0) * kaw_ref[...])
    bvec = kappa * a
    bonus = _seg_sum(r * k2 * rkw_ref[...], e_ref, et_ref) * v

    ti = lax.broadcasted_iota(jnp.int32, (Tb, Tb), 0)
    tj = lax.broadcasted_iota(jnp.int32, (Tb, Tb), 1)
    tril = jnp.where((tj <= ti) & ((ti >> log2l) == (tj >> log2l)), 1.0, 0.0).astype(BF16)
    cum = _dot_exact_lhs(tril, logw)
    trow = lax.broadcasted_iota(jnp.int32, (Tb, C_RWKV), 0)
    cum_end = jnp.broadcast_to(cum[L - 1:L, :], (Tb, C_RWKV))
    for s in range(1, n_sub):
        cum_end = jnp.where(trow >= s * L, jnp.broadcast_to(cum[(s + 1) * L - 1:(s + 1) * L, :], (Tb, C_RWKV)),
                            cum_end)
    gam = jnp.exp(cum)
    ginv = jnp.exp(-cum)
    gprev = jnp.exp(cum - logw)
    gtail = jnp.exp(cum_end - cum)

    kt = kappa * gprev
    kinv = k2 * ginv
    binv = bvec * ginv
    rt = r * gam
    khat = k2 * gtail
    bhat = bvec * gtail

    rr = lax.broadcasted_iota(jnp.int32, (GL, GL), 0)
    cc = lax.broadcasted_iota(jnp.int32, (GL, GL), 1)
    same = (rr >> log2l) == (cc >> log2l)
    tpos = rr & (L - 1)
    jpos = cc & (L - 1)
    mask_s = same & (jpos < tpos)
    mask_i = same & (jpos <= tpos)
    eye = rr == cc
    srow = lax.broadcasted_iota(jnp.int32, (GL, PACK), 0)
    slane = lax.broadcasted_iota(jnp.int32, (GL, PACK), 1)
    bmask = (srow >> log2l) == (slane >> int(math.log2(HEAD)))
    drow = lax.broadcasted_iota(jnp.int32, (PACK, PACK), 0)
    dcol = lax.broadcasted_iota(jnp.int32, (PACK, PACK), 1)
    deye = drow == dcol

    lane_split = GL % LANES == 0

    def bdot(a_, b_, dims=_BNN):
        return lax.dot_general(a_.astype(BF16), b_.astype(BF16), dims, preferred_element_type=F32)

    def btrans(x):
        return jnp.stack([x[gi].T for gi in range(N_GROUPS)], axis=0)

    def chunk_step(row0, st):
        def stack(x, dtype=F32):
            xc = x[row0:row0 + L].astype(dtype)
            x3 = jnp.stack([xc[:, gi * PACK:(gi + 1) * PACK] for gi in range(N_GROUPS)], axis=0)
            return jnp.where(bmask[None], jnp.concatenate([x3] * GROUP_HEADS, axis=1), jnp.zeros((), dtype))

        kt_s = stack(kt, BF16)
        rt_s = stack(rt)
        binv_s = stack(binv, BF16)
        kinv_s = stack(kinv, BF16)
        v_s = stack(v, BF16)
        khat_s = stack(khat)
        bhat_s = stack(bhat)

        if lane_split:
            a_all = bdot(jnp.concatenate([kt_s, rt_s.astype(BF16)], axis=1),
                         jnp.concatenate([binv_s, kinv_s], axis=1), _BNT)
            a_parts = (a_all[:, :GL, :GL], a_all[:, :GL, GL:], a_all[:, GL:, :GL], a_all[:, GL:, GL:])
        else:
            a_parts = (bdot(kt_s, binv_s, _BNT), bdot(kt_s, kinv_s, _BNT),
                       bdot(rt_s, binv_s, _BNT), bdot(rt_s, kinv_s, _BNT))
        n_mat = jnp.where(mask_s[None], a_parts[0], 0.0)
        a_kk = jnp.where(mask_s[None], a_parts[1], 0.0)
        a_br = jnp.where(mask_i[None], a_parts[2], 0.0)
        a_kr = jnp.where(mask_i[None], a_parts[3], 0.0)

        p_mat = -n_mat
        t_mat = jnp.where(eye[None], 1.0, 0.0) + p_mat
        if log2l > 1:
            p_mat = bdot(p_mat, p_mat)
        for lvl in range(1, log2l):
            if lvl == log2l - 1:
                t_mat = t_mat + bdot(p_mat, t_mat)
            elif lane_split:
                both = bdot(p_mat, jnp.concatenate([p_mat, t_mat], axis=2))
                t_mat = t_mat + both[:, :, GL:]
                p_mat = both[:, :, :GL]
            else:
                t_mat = t_mat + bdot(p_mat, t_mat)
                p_mat = bdot(p_mat, p_mat)

        av = bdot(jnp.concatenate([a_kk, a_kr], axis=1), v_s)
        wu = bdot(t_mat, jnp.concatenate([kt_s, av[:, :GL].astype(BF16)], axis=2))
        br = bdot(a_br, wu)
        q_s = rt_s - br[:, :, :PACK]
        y0_s = av[:, GL:] - br[:, :, PACK:]
        bhat_t = btrans(bhat_s)
        khat_t = btrans(khat_s)
        gam_end = jnp.exp(cum[row0 + L - 1:row0 + L, :])
        gl3 = jnp.stack([gam_end[:, gi * PACK:(gi + 1) * PACK] for gi in range(N_GROUPS)], axis=0)
        bw = bdot(bhat_t, wu)
        m_mat = jnp.where(deye[None], jnp.broadcast_to(gl3, (N_GROUPS, PACK, PACK)), 0.0) - bw[:, :, :PACK]
        c_mat = bdot(khat_t, v_s) - bw[:, :, PACK:]

        qm = bdot(jnp.concatenate([q_s, m_mat], axis=1), st)
        ys = qm[:, :GL] + y0_s
        yg = ys[:, 0:L]
        for h in range(1, GROUP_HEADS):
            yg = yg + ys[:, h * L:(h + 1) * L]
        return jnp.concatenate([yg[gi] for gi in range(N_GROUPS)], axis=1), qm[:, GL:] + c_mat

    st = st_sc[...]
    y_chunks = []
    for s in range(n_sub):
        y_c, st = chunk_step(s * L, st)
        y_chunks.append(y_c)
    st_sc[...] = st
    y = y_chunks[0] if n_sub == 1 else jnp.concatenate(y_chunks, axis=0)

    inv_head = 1.0 / HEAD
    mu = _seg_sum(y, e_ref, et_ref) * inv_head
    yc = y - mu
    var = _seg_sum(yc * yc, e_ref, et_ref) * inv_head
    yn = yc * lax.rsqrt(var + GN_EPS) * lng_ref[...] + lnb_ref[...]
    y_ref[0] = ((yn + bonus) * g).astype(y_ref.dtype)

    @pl.when(c == n_chunks - 1)
    def _():
        stout_ref[0] = st_sc[...]
        shout_ref[0, :, 0:C_RWKV] = pr_sc[...]
        shout_ref[0, :, C_RWKV:2 * C_RWKV] = pk_sc[...]
        shout_ref[0, :, 2 * C_RWKV:3 * C_RWKV] = pv_sc[...]
        shout_ref[0, :, 3 * C_RWKV:3 * C_RWKV + LORA_PAD] = plo_sc[...]


def _rwkv_mix(proj3, shift_parts, st0, params, chunk, block, cast_arrays=()):
    bsz, t_len, _ = proj3.shape
    L = block
    n_steps = bsz * (t_len // L)
    n_cast = len(cast_arrays)
    rkv_blk0 = 2 * C_CONV // C_RWKV
    lora_blk = (2 * C_CONV + 3 * C_RWKV) // LORA_PAD
    row = lambda n: pl.BlockSpec((1, n), lambda b, c: (0, 0))
    full = lambda s: pl.BlockSpec(s, lambda b, c: tuple(0 for _ in s))
    sh = lambda n: pl.BlockSpec((1, 1, n), lambda b, c: (b, 0, 0))
    in_specs = [
        pl.BlockSpec((1, L, C_RWKV), lambda b, c: (b, c, rkv_blk0)),
        pl.BlockSpec((1, L, C_RWKV), lambda b, c: (b, c, rkv_blk0 + 1)),
        pl.BlockSpec((1, L, C_RWKV), lambda b, c: (b, c, rkv_blk0 + 2)),
        pl.BlockSpec((1, L, LORA_PAD), lambda b, c: (b, c, lora_blk)),
        sh(C_RWKV), sh(C_RWKV), sh(C_RWKV), sh(LORA_PAD),
        pl.BlockSpec((1, N_GROUPS, PACK, PACK), lambda b, c: (b, 0, 0, 0)),
        row(C_RWKV), row(C_RWKV), row(C_RWKV), row(LORA_PAD),
        row(C_RWKV), full((LANES, C_RWKV)), full((LANES, C_RWKV)),
        row(C_RWKV), full((LANES, C_RWKV)), full((LANES, C_RWKV)),
        full((2 * LANES, C_RWKV)), full((2 * LANES, C_RWKV)),
        row(C_RWKV), row(C_RWKV), row(C_RWKV), row(C_RWKV), row(C_RWKV),
        full((C_RWKV, LANES)), full((LANES, C_RWKV)),
    ]
    out_shape = (jax.ShapeDtypeStruct((bsz, t_len, C_RWKV), BF16),
                 jax.ShapeDtypeStruct((bsz, N_GROUPS, PACK, PACK), F32),
                 jax.ShapeDtypeStruct((bsz, 1, 3 * C_RWKV + LORA_PAD), F32))
    out_specs = (pl.BlockSpec((1, L, C_RWKV), lambda b, c: (b, c, 0)),
                 pl.BlockSpec((1, N_GROUPS, PACK, PACK), lambda b, c: (b, 0, 0, 0)),
                 pl.BlockSpec((1, 1, 3 * C_RWKV + LORA_PAD), lambda b, c: (b, 0, 0)))
    steps_per_b = t_len // L
    for arr in cast_arrays:
        rows, width = arr.shape
        win = rows // n_steps
        assert win * n_steps == rows
        spec = pl.BlockSpec((win, width), lambda b, c: (b * steps_per_b + c, 0))
        in_specs = in_specs + [spec]
        out_specs = out_specs + (spec,)
        out_shape = out_shape + (jax.ShapeDtypeStruct((rows, width), BF16),)
    return pl.pallas_call(
        functools.partial(_rwkv_kernel, chunk, n_cast),
        out_shape=out_shape,
        grid=(bsz, t_len // L),
        in_specs=in_specs,
        out_specs=out_specs,
        scratch_shapes=[pltpu.VMEM((N_GROUPS, PACK, PACK), F32),
                        pltpu.VMEM((1, C_RWKV), F32), pltpu.VMEM((1, C_RWKV), F32),
                        pltpu.VMEM((1, C_RWKV), F32), pltpu.VMEM((1, LORA_PAD), F32)],
        compiler_params=pltpu.CompilerParams(dimension_semantics=("arbitrary", "arbitrary")),
        name="rwkv7_mix",
    )(proj3, proj3, proj3, proj3, *shift_parts, st0, *params, *cast_arrays)


def _outproj_kernel(c_ref, y_ref, x_ref, wa_ref, wb_ref, g_ref, b_ref, rwh_ref, rwl_ref, rb_ref,
                    x1_ref, idx_ref, gate_ref):
    mix = (jnp.dot(c_ref[...], wa_ref[...], preferred_element_type=F32)
           + jnp.dot(y_ref[...], wb_ref[...], preferred_element_type=F32))
    h = ALPHA * x_ref[...] + mix
    mu = jnp.mean(h, axis=-1, keepdims=True)
    hc = h - mu
    var = jnp.mean(hc * hc, axis=-1, keepdims=True)
    x1 = hc * lax.rsqrt(var + LN_EPS) * g_ref[...] + b_ref[...]
    x1_ref[...] = x1
    logits = _dot_split_w(x1, rwh_ref[...], rwl_ref[...]) + rb_ref[...]
    lane = lax.broadcasted_iota(jnp.int32, logits.shape, 1)
    idx_out = jnp.zeros(logits.shape, jnp.int32)
    val_out = jnp.zeros(logits.shape, F32)
    vals = []
    for kk in range(TOP_K):
        m = jnp.max(logits, axis=-1, keepdims=True)
        sel = jnp.min(jnp.where(logits == m, lane, ROUTER_PAD), axis=-1, keepdims=True)
        vals.append(m)
        idx_out = jnp.where(lane == kk, sel, idx_out)
        logits = jnp.where(lane == sel, -jnp.inf, logits)
    exps = [jnp.exp(vv - vals[0]) for vv in vals]
    denom = exps[0]
    for ee in exps[1:]:
        denom = denom + ee
    for kk in range(TOP_K):
        val_out = jnp.where(lane == kk, exps[kk] / denom, val_out)
    idx_ref[...] = idx_out
    gate_ref[...] = val_out


def _outproj_into_kernel(base_ref, *refs):
    del base_ref
    _outproj_kernel(*refs)


def _out_proj(c2, y2, x2, wa, wb, ln_g, ln_b, rw_hi, rw_lo, rb, tm, n_total, x1_base=None, row0=0):
    n = x2.shape[0]
    blk0 = row0 // tm
    assert blk0 * tm == row0
    row = lambda w: pl.BlockSpec((1, w), lambda i: (0, 0))
    in_specs = [pl.BlockSpec((tm, C_CONV), lambda i: (i, 0)),
                pl.BlockSpec((tm, C_RWKV), lambda i: (i, 0)),
                pl.BlockSpec((tm, D_MODEL), lambda i: (i, 0)),
                pl.BlockSpec((C_CONV, D_MODEL), lambda i: (0, 0)),
                pl.BlockSpec((C_RWKV, D_MODEL), lambda i: (0, 0)),
                row(D_MODEL), row(D_MODEL),
                pl.BlockSpec((D_MODEL, ROUTER_PAD), lambda i: (0, 0)),
                pl.BlockSpec((D_MODEL, ROUTER_PAD), lambda i: (0, 0)),
                row(ROUTER_PAD)]
    args = (c2, y2, x2, wa, wb, ln_g, ln_b, rw_hi, rw_lo, rb)
    body, aliases = _outproj_kernel, {}
    if x1_base is not None:
        in_specs = [pl.BlockSpec(memory_space=pl.ANY)] + in_specs
        args = (x1_base,) + args
        body, aliases = _outproj_into_kernel, {0: 0}
    return pl.pallas_call(
        body,
        out_shape=(jax.ShapeDtypeStruct((n_total, D_MODEL), F32),
                   jax.ShapeDtypeStruct((n, ROUTER_PAD), jnp.int32),
                   jax.ShapeDtypeStruct((n, ROUTER_PAD), F32)),
        grid=(n // tm,),
        in_specs=in_specs,
        out_specs=(pl.BlockSpec((tm, D_MODEL), lambda i: (i + blk0, 0)),
                   pl.BlockSpec((tm, ROUTER_PAD), lambda i: (i, 0)),
                   pl.BlockSpec((tm, ROUTER_PAD), lambda i: (i, 0))),
        input_output_aliases=aliases,
        compiler_params=pltpu.CompilerParams(dimension_semantics=("arbitrary",)),
        name="out_proj_ln_router",
    )(*args)


DMA_PRIORITIES = 2


def _expert_kernel(bexp_ref, nused_ref, tokc_ref, tokn_ref, x_hbm, wg_ref, bg_ref, wu_ref, bu_ref, wd_ref, bd_ref,
                   o_ref, xg_ref, xb_ref, sems):
    del bexp_ref
    i = pl.program_id(0)
    j = pl.program_id(1)
    nj = pl.num_programs(1)
    n_used = nused_ref[0]
    rows = xb_ref.shape[0]
    slot = lax.rem(i, 2)

    def row_copy(tok_ref, s, g, sub):
        tok = tok_ref[0, 0, g * SUBLANES + sub]
        return pltpu.make_async_copy(x_hbm.at[pl.ds(tok, 1)], xg_ref.at[s, g, pl.ds(sub, 1)], sems.at[s])

    def issue(tok_ref, s, g0, n_groups):
        def body(g, carry):
            for sub in range(SUBLANES):
                row_copy(tok_ref, s, g0 + g, sub).start(priority=sub % DMA_PRIORITIES)
            return carry

        lax.fori_loop(0, n_groups, body, 0)

    groups = rows // SUBLANES

    @pl.when((i == 0) & (j == 0) & (n_used > 0))
    def _():
        issue(tokc_ref, 0, 0, groups)

    @pl.when(i + 1 < n_used)
    def _():
        per_step = groups // MOE_NJ
        issue(tokn_ref, 1 - slot, j * per_step, per_step)

    @pl.when(i < n_used)
    def _():
        @pl.when(j == 0)
        def _():
            def drain(g, carry):
                for sub in range(SUBLANES):
                    row_copy(tokc_ref, slot, g, sub).wait()
                return carry

            lax.fori_loop(0, groups, drain, 0)
            xb_ref[...] = xg_ref[slot].reshape(rows, D_MODEL).astype(BF16)
            o_ref[...] = jnp.broadcast_to(bd_ref[0], o_ref.shape)

        x = xb_ref[...]
        gate = jnp.minimum(jnp.dot(x, wg_ref[0], preferred_element_type=F32) + bg_ref[0, pl.ds(j, 1), :],
                           SWIGLU_LIMIT)
        up = jnp.clip(jnp.dot(x, wu_ref[0], preferred_element_type=F32) + bu_ref[0, pl.ds(j, 1), :],
                      -SWIGLU_LIMIT, SWIGLU_LIMIT)
        hmid = (up + 1.0) * gate * jax.nn.sigmoid(SWIGLU_ALPHA * gate)
        o_ref[...] += jnp.dot(hmid.astype(BF16), wd_ref[0], preferred_element_type=F32)

    @pl.when((i >= nused_ref[0]) & (j == nj - 1))
    def _():
        o_ref[...] = jnp.zeros_like(o_ref)


def _experts(block_exp, n_used, row_tok3, x, wg, bg, wu, bu, wd, bd):
    nb, _, rows = row_tok3.shape
    n_exp = wg.shape[0]
    nj = MOE_NJ
    tf = D_FF // nj

    def jsel(i, j, nu):
        return jnp.where(i < nu[0], j, nj - 1)

    tok_spec = lambda off: pl.BlockSpec((1, 1, rows), lambda i, j, be, nu: (jnp.minimum(i + off, nb - 1), 0, 0),
                                        memory_space=pltpu.SMEM)
    grid_spec = pltpu.PrefetchScalarGridSpec(
        num_scalar_prefetch=2,
        grid=(nb, nj),
        in_specs=[tok_spec(0), tok_spec(1),
                  pl.BlockSpec(memory_space=pl.ANY),
                  pl.BlockSpec((1, D_MODEL, tf), lambda i, j, be, nu: (be[i], 0, jsel(i, j, nu))),
                  pl.BlockSpec((1, nj, tf), lambda i, j, be, nu: (be[i], 0, 0)),
                  pl.BlockSpec((1, D_MODEL, tf), lambda i, j, be, nu: (be[i], 0, jsel(i, j, nu))),
                  pl.BlockSpec((1, nj, tf), lambda i, j, be, nu: (be[i], 0, 0)),
                  pl.BlockSpec((1, tf, D_MODEL), lambda i, j, be, nu: (be[i], jsel(i, j, nu), 0)),
                  pl.BlockSpec((1, 1, D_MODEL), lambda i, j, be, nu: (be[i], 0, 0))],
        out_specs=pl.BlockSpec((rows, D_MODEL), lambda i, j, be, nu: (i, 0)),
        scratch_shapes=[pltpu.VMEM((2, rows // SUBLANES, SUBLANES, D_MODEL), x.dtype),
                        pltpu.VMEM((rows, D_MODEL), BF16), pltpu.SemaphoreType.DMA((2,))],
    )
    return pl.pallas_call(
        _expert_kernel,
        out_shape=jax.ShapeDtypeStruct((nb * rows, D_MODEL), F32),
        grid_spec=grid_spec,
        compiler_params=pltpu.CompilerParams(dimension_semantics=("arbitrary", "arbitrary")),
        name="moe_experts",
    )(block_exp, n_used, row_tok3, row_tok3, x, wg, bg.reshape(n_exp, nj, tf), wu, bu.reshape(n_exp, nj, tf),
      wd, bd.reshape(n_exp, 1, D_MODEL))


def _combine_kernel(dest_ref, gate_ref, x1_ref, g_ref, b_ref, yb_hbm, o_ref, buf_ref, sem):
    tc = x1_ref.shape[0]

    def row_copy(d, kk, g, sub):
        return pltpu.make_async_copy(yb_hbm.at[pl.ds(d, 1)], buf_ref.at[kk, g, pl.ds(sub, 1)], sem)

    def issue(g, carry):
        for sub in range(SUBLANES):
            for kk in range(TOP_K):
                d = dest_ref[0, 0, (g * SUBLANES + sub) * TOP_K + kk]
                row_copy(d, kk, g, sub).start(priority=kk % DMA_PRIORITIES)
        return carry

    lax.fori_loop(0, tc // SUBLANES, issue, 0)

    def drain(g, carry):
        for sub in range(SUBLANES):
            for kk in range(TOP_K):
                row_copy(0, kk, g, sub).wait()
        return carry

    lax.fori_loop(0, tc // SUBLANES, drain, 0)

    gates = gate_ref[...]
    moe = gates[:, 0:1] * buf_ref[0].reshape(tc, D_MODEL)
    for kk in range(1, TOP_K):
        moe = moe + gates[:, kk:kk + 1] * buf_ref[kk].reshape(tc, D_MODEL)
    h = ALPHA * x1_ref[...] + moe
    mu = jnp.mean(h, axis=-1, keepdims=True)
    hc = h - mu
    var = jnp.mean(hc * hc, axis=-1, keepdims=True)
    o_ref[...] = hc * lax.rsqrt(var + LN_EPS) * g_ref[...] + b_ref[...]


def _combine(dest3, gates, x1, ln_g, ln_b, yb, tc, row0=0):
    n = dest3.shape[0] * tc
    blk0 = row0 // tc
    assert blk0 * tc == row0
    return pl.pallas_call(
        _combine_kernel,
        out_shape=jax.ShapeDtypeStruct((n, D_MODEL), F32),
        grid=(n // tc,),
        in_specs=[pl.BlockSpec((1, 1, tc * TOP_K), lambda i: (i, 0, 0), memory_space=pltpu.SMEM),
                  pl.BlockSpec((tc, ROUTER_PAD), lambda i: (i, 0)),
                  pl.BlockSpec((tc, D_MODEL), lambda i: (i + blk0, 0)),
                  pl.BlockSpec((1, D_MODEL), lambda i: (0, 0)),
                  pl.BlockSpec((1, D_MODEL), lambda i: (0, 0)),
                  pl.BlockSpec(memory_space=pl.ANY)],
        out_specs=pl.BlockSpec((tc, D_MODEL), lambda i: (i, 0)),
        scratch_shapes=[pltpu.VMEM((TOP_K, tc // SUBLANES, SUBLANES, D_MODEL), F32), pltpu.SemaphoreType.DMA],
        compiler_params=pltpu.CompilerParams(dimension_semantics=("arbitrary",)),
        name="moe_combine_ln2",
    )(dest3, gates, x1, ln_g, ln_b, yb)


def _pad_cols(w, width):
    return jnp.pad(w, ((0, 0), (0, width - w.shape[1])))


def _pad_rows(w, height):
    return jnp.pad(w, ((0, height - w.shape[0]), (0, 0)))


def _split_lora_cols(w):
    xw = w[..., 0:R_DECAY]
    xa = w[..., R_DECAY:R_DECAY + R_ICLR]
    xg = w[..., R_DECAY + R_ICLR:]
    pad = lambda x, n: jnp.pad(x, [(0, 0)] * (x.ndim - 1) + [(0, n - x.shape[-1])])
    return jnp.concatenate([pad(xw, LANES), pad(xa, LANES), pad(xg, 2 * LANES)], axis=-1)


def _pick(n, prefs):
    for p in prefs:
        if n % p == 0:
            return p
    return n


def _mixer_group(x, conv_buf, shift_buf, wkv_state, wts, cast_along=()):
    bsz, t_len, _ = x.shape
    n = bsz * t_len
    proj = _in_proj(x.reshape(n, D_MODEL), wts["w_in"], wts["b_in"], _pick(n, (IN_TM, 256, 128)), IN_TN)
    proj3 = proj.reshape(bsz, t_len, P_PAD)

    hist = jnp.pad(conv_buf, ((0, 0), (HIST - (CONV_WIDTH - 1), 0), (0, 0)))
    c, tail = _conv_module(proj3, hist, wts["conv_w"], wts["conv_b"], wts["conv_ln_g"], wts["conv_ln_b"],
                           _pick(t_len, (256, 128, 64, 32, 16, 8)))
    new_conv = tail[:, HIST - (CONV_WIDTH - 1):, :]

    sh_rkv = shift_buf[:, :, :3 * C_RWKV]
    sh_lo = _split_lora_cols(shift_buf[:, :, 3 * C_RWKV:])
    shift_parts = (sh_rkv[:, :, 0:C_RWKV], sh_rkv[:, :, C_RWKV:2 * C_RWKV], sh_rkv[:, :, 2 * C_RWKV:], sh_lo)
    st_t = jnp.swapaxes(wkv_state, -1, -2).reshape(bsz, N_GROUPS, GROUP_HEADS, HEAD, HEAD)
    eye_h = jnp.eye(GROUP_HEADS, dtype=F32)
    st0 = jnp.einsum("bghkv,hj->bghkjv", st_t, eye_h).reshape(bsz, N_GROUPS, PACK, PACK)
    chunk = _pick(t_len, (RWKV_CHUNK, 32, 16))
    block = RWKV_BLOCK if t_len % RWKV_BLOCK == 0 else chunk
    n_steps = bsz * (t_len // block)
    ride = tuple(w for w in cast_along if w.shape[0] % n_steps == 0 and (w.shape[0] // n_steps) % 16 == 0)
    yb, st_out, sh_out, *cast_done = _rwkv_mix(proj3, shift_parts, st0, wts["rwkv_params"], chunk, block, ride)
    if len(ride) != len(cast_along):
        cast_done = [w.astype(BF16) for w in cast_along]
    st5 = st_out.reshape(bsz, N_GROUPS, GROUP_HEADS, HEAD, GROUP_HEADS, HEAD)
    st_diag = jnp.einsum("bghkhv->bghkv", st5)
    new_wkv = jnp.swapaxes(st_diag, -1, -2).reshape(bsz, N_HEADS, HEAD, HEAD)
    lo = sh_out[:, :, 3 * C_RWKV:]
    new_shift = jnp.concatenate([sh_out[:, :, :3 * C_RWKV], lo[:, :, 0:R_DECAY], lo[:, :, LANES:LANES + R_ICLR],
                                 lo[:, :, 2 * LANES:2 * LANES + R_GATE]], axis=-1)
    return c.reshape(n, C_CONV), yb.reshape(n, C_RWKV), new_conv, new_shift, new_wkv, cast_done


def _route(top_idx, n_tok):
    n_assign = n_tok * TOP_K
    flat_e = top_idx.reshape(-1)
    onehot = (flat_e[:, None] == jnp.arange(N_EXPERTS, dtype=jnp.int32)[None, :]).astype(jnp.int32)
    csum = jnp.cumsum(onehot, axis=0)
    rank = jnp.take_along_axis(csum, flat_e[:, None], axis=1)[:, 0] - 1
    counts = csum[-1]
    padded = (counts + MOE_TM - 1) // MOE_TM * MOE_TM
    seg_end = jnp.cumsum(padded)
    seg_start = seg_end - padded
    dest = (seg_start[flat_e] + rank).astype(jnp.int32)
    n_rows = (n_assign + N_EXPERTS * (MOE_TM - 1) + MOE_TM - 1) // MOE_TM * MOE_TM
    n_blocks = n_rows // MOE_TM
    block_start = jnp.arange(n_blocks, dtype=jnp.int32) * MOE_TM
    block_exp = jnp.minimum(jnp.sum((seg_end[None, :] <= block_start[:, None]).astype(jnp.int32), axis=1),
                            N_EXPERTS - 1).astype(jnp.int32)
    order = jnp.argsort(flat_e, stable=True).astype(jnp.int32)
    start = jnp.cumsum(counts) - counts
    local = jnp.arange(MOE_TM, dtype=jnp.int32)[None, :] + (block_start - seg_start[block_exp])[:, None]
    valid = local < counts[block_exp][:, None]
    pos = jnp.clip(start[block_exp][:, None] + local, 0, n_assign - 1)
    row_tok = jnp.where(valid, order[pos.reshape(-1)].reshape(n_blocks, MOE_TM) // TOP_K, 0).reshape(-1)
    n_used = (seg_end[-1] // MOE_TM).astype(jnp.int32).reshape(1)
    return dest, row_tok, block_exp, n_used, n_blocks


def kernel(x_prompt, x_sample, state_conv, state_shift, state_wkv, w_in, b_in, mu_shift, conv_w, conv_b,
           conv_ln_g, conv_ln_b, rwkv_w0, rwkv_w2, rwkv_a0, rwkv_a2, rwkv_g2, rwkv_k_k, rwkv_k_a, rwkv_r_k,
           rwkv_ln_g, rwkv_ln_b, w_out, ln1_g, ln1_b, router_w, router_b, w_gate, b_gate, w_up, b_up,
           w_down, b_down, ln2_g, ln2_b):
    assert w_in.shape[0] == 1, "single layer"
    d = 0
    row = lambda v: v.reshape(1, -1)
    n_p, t_p, _ = x_prompt.shape
    n_s, t_s, _ = x_sample.shape

    w_rkv = w_in[d][:, 2 * C_CONV:2 * C_CONV + 3 * C_RWKV]
    w_lo = _split_lora_cols(w_in[d][:, 2 * C_CONV + 3 * C_RWKV:])
    w_in_p = jnp.concatenate([w_in[d][:, :2 * C_CONV], w_rkv, w_lo], axis=1).astype(BF16)
    b_in_p = jnp.concatenate([b_in[d][None, :2 * C_CONV], b_in[d][None, 2 * C_CONV:2 * C_CONV + 3 * C_RWKV],
                              _split_lora_cols(b_in[d][None, 2 * C_CONV + 3 * C_RWKV:])], axis=1)
    mu = mu_shift[d][None, :]
    mu_lo = _split_lora_cols(mu[:, 3 * C_RWKV:])
    head_of_lane = jnp.arange(C_RWKV, dtype=jnp.int32) // HEAD
    e_mat = (head_of_lane[:, None] == jnp.arange(LANES, dtype=jnp.int32)[None, :]).astype(BF16)

    def hi_lo(w, height):
        w = _pad_rows(w, height)
        w_hi = w.astype(BF16)
        return w_hi, (w - w_hi.astype(F32)).astype(BF16)

    rwkv_params = (
        mu[:, 0:C_RWKV], mu[:, C_RWKV:2 * C_RWKV], mu[:, 2 * C_RWKV:3 * C_RWKV], mu_lo,
        row(rwkv_w0[d]), *hi_lo(rwkv_w2[d], LANES), row(rwkv_a0[d]), *hi_lo(rwkv_a2[d], LANES),
        *hi_lo(rwkv_g2[d], 2 * LANES),
        row(rwkv_k_k[d]), row(rwkv_k_a[d]), row(rwkv_r_k[d]), row(rwkv_ln_g[d]), row(rwkv_ln_b[d]),
        e_mat, e_mat.T,
    )
    wts = dict(w_in=w_in_p, b_in=b_in_p, conv_w=conv_w[d], conv_b=row(conv_b[d]),
               conv_ln_g=row(conv_ln_g[d]), conv_ln_b=row(conv_ln_b[d]), rwkv_params=rwkv_params)

    zero_conv = jnp.zeros((n_p, CONV_WIDTH - 1, C_CONV), x_prompt.dtype)
    zero_shift = jnp.zeros((n_p, 1, N_SHIFT), x_prompt.dtype)
    zero_wkv = jnp.zeros((n_p, N_HEADS, HEAD, HEAD), state_wkv.dtype)
    expert_w = (w_gate[d].reshape(-1, D_FF), w_up[d].reshape(-1, D_FF), w_down[d].reshape(-1, D_MODEL))
    c_p, y_p, conv_p, shift_p, wkv_p, expert_w = _mixer_group(x_prompt, zero_conv, zero_shift, zero_wkv, wts,
                                                              expert_w)
    wg_b = expert_w[0].reshape(N_EXPERTS, D_MODEL, D_FF)
    wu_b = expert_w[1].reshape(N_EXPERTS, D_MODEL, D_FF)
    wd_b = expert_w[2].reshape(N_EXPERTS, D_FF, D_MODEL)
    c_s, y_s, conv_s, shift_s, wkv_s, _ = _mixer_group(x_sample, state_conv[d], state_shift[d], state_wkv[d], wts)

    w_out_b = w_out[d].astype(BF16)
    rw = _pad_cols(router_w[d], ROUTER_PAD)
    rw_hi = rw.astype(BF16)
    rw_lo = (rw - rw_hi.astype(F32)).astype(BF16)
    rb = jnp.concatenate([router_b[d], jnp.full((ROUTER_PAD - N_EXPERTS,), -jnp.inf, F32)])[None, :]

    n_tok_p = n_p * t_p
    n_tok_s = n_s * t_s
    n_tok = n_tok_p + n_tok_s

    def out_proj(c2, y2, x3, x1_base, row0):
        n = c2.shape[0]
        tm = math.gcd(_pick(n, (OUT_TM, 256, 128)), row0) if row0 else _pick(n, (OUT_TM, 256, 128))
        return _out_proj(c2, y2, x3.reshape(n, D_MODEL), w_out_b[:C_CONV], w_out_b[C_CONV:], row(ln1_g[d]),
                         row(ln1_b[d]), rw_hi, rw_lo, rb, tm, n_tok, x1_base, row0)

    x1, idx_p, gate_p = out_proj(c_p, y_p, x_prompt, None, 0)
    x1, idx_s, gate_s = out_proj(c_s, y_s, x_sample, x1, n_tok_p)
    top_idx = jnp.concatenate([idx_p[:, :TOP_K], idx_s[:, :TOP_K]], axis=0)
    dest, row_tok, block_exp, n_used, n_blocks = _route(top_idx, n_tok)
    yb = _experts(block_exp, n_used, row_tok.reshape(n_blocks, 1, MOE_TM), x1, wg_b, b_gate[d], wu_b, b_up[d],
                  wd_b, b_down[d])

    def combine(dest_g, gate_g, n, row0):
        tc = _pick(n, (256, 128, 64, 32, 16, 8))
        tc = math.gcd(tc, row0) if row0 else tc
        return _combine(dest_g.reshape(n // tc, 1, tc * TOP_K), gate_g, x1, row(ln2_g[d]), row(ln2_b[d]), yb, tc,
                        row0)

    y_prompt = combine(dest[:n_tok_p * TOP_K], gate_p, n_tok_p, 0).reshape(n_p, t_p, D_MODEL)
    y_sample = combine(dest[n_tok_p * TOP_K:], gate_s, n_tok_s, n_tok_p).reshape(n_s, t_s, D_MODEL)
    return (y_prompt, y_sample, conv_p[None], shift_p[None], wkv_p[None], conv_s[None], shift_s[None], wkv_s[None])
```

```python
import functools
import math

import jax
import jax.numpy as jnp
from jax import lax
from jax.experimental import pallas as pl
from jax.experimental.pallas import tpu as pltpu

F32 = jnp.float32
BF16 = jnp.bfloat16

D_MODEL = 2048
C_CONV = 1024
C_RWKV = 1024
HEAD = 64
N_HEADS = C_RWKV // HEAD
CONV_WIDTH = 31
R_DECAY = 64
R_ICLR = 64
R_GATE = 160
N_SHIFT = 3 * C_RWKV + R_DECAY + R_ICLR + R_GATE
N_EXPERTS = 32
TOP_K = 4
D_FF = 2048
SWIGLU_LIMIT = 7.0
SWIGLU_ALPHA = 1.702
LN_EPS = 1e-5
GN_EPS = 64e-5
ALPHA = 2.0 ** 0.25

LANES = 128
SUBLANES = 8

HIST = 32
LORA_PAD = 512
P_PAD = 2 * C_CONV + 3 * C_RWKV + LORA_PAD
GROUP_HEADS = 2
PACK = GROUP_HEADS * HEAD
N_GROUPS = N_HEADS // GROUP_HEADS
MOE_TM = 512
MOE_NJ = 2
OUT_TM = 512
IN_TN = P_PAD // 2
IN_TM = 512
RWKV_CHUNK = 64
RWKV_BLOCK = 128
ROUTER_PAD = LANES


def _dot(a, b, prec=1, dims=(((1,), (0,)), ((), ()))):
    if prec == 6:
        return lax.dot_general(a.astype(F32), b.astype(F32), dims, precision=lax.Precision.HIGHEST,
                               preferred_element_type=F32)
    d = lambda x, y: lax.dot_general(x, y, dims, preferred_element_type=F32)
    if prec == 1:
        return d(a.astype(BF16), b.astype(BF16))
    a_hi = a.astype(BF16)
    a_lo = (a - a_hi.astype(F32)).astype(BF16)
    b_hi = b.astype(BF16)
    b_lo = (b - b_hi.astype(F32)).astype(BF16)
    return d(a_hi, b_hi) + d(a_hi, b_lo) + d(a_lo, b_hi)


_NT = (((1,), (1,)), ((), ()))
_BNN = (((2,), (1,)), ((0,), (0,)))
_BNT = (((2,), (2,)), ((0,), (0,)))


def _split3(x):
    p1 = x.astype(BF16)
    r1 = x - p1.astype(F32)
    p2 = r1.astype(BF16)
    p3 = (r1 - p2.astype(F32)).astype(BF16)
    return p1, p2, p3


def _dot_exact_rhs(x, m_bf16):
    d = lambda a: jnp.dot(a, m_bf16, preferred_element_type=F32)
    p1, p2, p3 = _split3(x)
    return d(p1) + d(p2) + d(p3)


def _dot_exact_lhs(m_bf16, x):
    d = lambda a: jnp.dot(m_bf16, a, preferred_element_type=F32)
    p1, p2, p3 = _split3(x)
    return d(p1) + d(p2) + d(p3)


def _dot_split_w(x, w_hi, w_lo):
    x_hi = x.astype(BF16)
    x_lo = (x - x_hi.astype(F32)).astype(BF16)
    d = lambda a, b: jnp.dot(a, b, preferred_element_type=F32)
    return d(x_hi, w_hi) + d(x_hi, w_lo) + d(x_lo, w_hi)


def _mm_bias_kernel(x_ref, w_ref, b_ref, o_ref, xb_ref):
    @pl.when(pl.program_id(1) == 0)
    def _():
        xb_ref[...] = x_ref[...].astype(BF16)

    o_ref[...] = jnp.dot(xb_ref[...], w_ref[...], preferred_element_type=F32) + b_ref[...]


def _in_proj(x, w_bf16, b, tm, tn):
    n, k = x.shape
    p = w_bf16.shape[1]
    return pl.pallas_call(
        _mm_bias_kernel,
        out_shape=jax.ShapeDtypeStruct((n, p), F32),
        grid=(n // tm, p // tn),
        in_specs=[pl.BlockSpec((tm, k), lambda i, j: (i, 0)),
                  pl.BlockSpec((k, tn), lambda i, j: (0, j)),
                  pl.BlockSpec((1, tn), lambda i, j: (0, j))],
        out_specs=pl.BlockSpec((tm, tn), lambda i, j: (i, j)),
        scratch_shapes=[pltpu.VMEM((tm, k), BF16)],
        compiler_params=pltpu.CompilerParams(dimension_semantics=("arbitrary", "arbitrary")),
        name="in_proj",
    )(x, w_bf16, b)


def _conv_kernel(val_ref, gate_ref, hist_ref, w_ref, cb_ref, g_ref, b_ref, c_ref, tail_ref, ext_ref, sh_ref):
    t = pl.program_id(1)
    tt = val_ref.shape[1]

    @pl.when(t == 0)
    def _():
        ext_ref[0:HIST, :] = hist_ref[0]

    u = val_ref[0] * jax.nn.sigmoid(gate_ref[0])
    ext_ref[HIST:HIST + tt, :] = u
    span = tt + HIST - SUBLANES
    for s in range(1, SUBLANES):
        sh_ref[s, 0:span, :] = ext_ref[s:s + span, :]
    off = HIST - (CONV_WIDTH - 1)
    acc = jnp.broadcast_to(cb_ref[...], (tt, C_CONV))
    for j in range(CONV_WIDTH):
        base = (off + j) // SUBLANES * SUBLANES
        s = (off + j) % SUBLANES
        src = ext_ref[base:base + tt, :] if s == 0 else sh_ref[s, base:base + tt, :]
        acc = acc + w_ref[j:j + 1, :] * src
    mu = jnp.mean(acc, axis=-1, keepdims=True)
    xc = acc - mu
    var = jnp.mean(xc * xc, axis=-1, keepdims=True)
    y = xc * lax.rsqrt(var + LN_EPS) * g_ref[...] + b_ref[...]
    c_ref[0] = (y * jax.nn.sigmoid(y)).astype(c_ref.dtype)
    tail = ext_ref[tt:tt + HIST, :]
    ext_ref[0:HIST, :] = tail
    tail_ref[0] = tail


def _conv_module(proj3, hist, conv_w, conv_b, ln_g, ln_b, tt):
    bsz, t_len, _ = proj3.shape
    nblk = C_CONV // C_CONV
    return pl.pallas_call(
        _conv_kernel,
        out_shape=(jax.ShapeDtypeStruct((bsz, t_len, C_CONV), BF16),
                   jax.ShapeDtypeStruct((bsz, HIST, C_CONV), F32)),
        grid=(bsz, t_len // tt),
        in_specs=[pl.BlockSpec((1, tt, C_CONV), lambda b, t: (b, t, 0)),
                  pl.BlockSpec((1, tt, C_CONV), lambda b, t: (b, t, nblk)),
                  pl.BlockSpec((1, HIST, C_CONV), lambda b, t: (b, 0, 0)),
                  pl.BlockSpec((CONV_WIDTH, C_CONV), lambda b, t: (0, 0)),
                  pl.BlockSpec((1, C_CONV), lambda b, t: (0, 0)),
                  pl.BlockSpec((1, C_CONV), lambda b, t: (0, 0)),
                  pl.BlockSpec((1, C_CONV), lambda b, t: (0, 0))],
        out_specs=(pl.BlockSpec((1, tt, C_CONV), lambda b, t: (b, t, 0)),
                   pl.BlockSpec((1, HIST, C_CONV), lambda b, t: (b, 0, 0))),
        scratch_shapes=[pltpu.VMEM((HIST + tt, C_CONV), F32),
                        pltpu.VMEM((SUBLANES, HIST + tt, C_CONV), F32)],
        compiler_params=pltpu.CompilerParams(dimension_semantics=("arbitrary", "arbitrary")),
        name="conv_module",
    )(proj3, proj3, hist, conv_w, conv_b, ln_g, ln_b)


PREC_CHUNK = 1
PREC_STATE = 1


def _seg_sum(x, e_ref, et_ref):
    return _dot_exact_rhs(_dot_exact_rhs(x, e_ref[...]), et_ref[...])


def _rwkv_kernel(chunk, n_cast, *refs):
    n_in = 28
    (r_ref, k_ref, v_ref, lo_ref, shr_ref, shk_ref, shv_ref, shlo_ref, st0_ref,
     mur_ref, muk_ref, muv_ref, mulo_ref, w0_ref, w2h_ref, w2l_ref, a0_ref, a2h_ref, a2l_ref,
     g2h_ref, g2l_ref, kkw_ref, kaw_ref, rkw_ref, lng_ref, lnb_ref, e_ref, et_ref) = refs[:n_in]
    cast_in = refs[n_in:n_in + n_cast]
    y_ref, stout_ref, shout_ref = refs[n_in + n_cast:n_in + n_cast + 3]
    cast_out = refs[n_in + n_cast + 3:n_in + 2 * n_cast + 3]
    st_sc, pr_sc, pk_sc, pv_sc, plo_sc = refs[n_in + 2 * n_cast + 3:]

    for src, dst in zip(cast_in, cast_out):
        dst[...] = src[...].astype(dst.dtype)

    c = pl.program_id(1)
    n_chunks = pl.num_programs(1)
    Tb = r_ref.shape[1]
    L = chunk
    n_sub = Tb // L
    GL = GROUP_HEADS * L
    log2l = int(math.log2(L))

    @pl.when(c == 0)
    def _():
        st_sc[...] = st0_ref[0]
        pr_sc[...] = shr_ref[0]
        pk_sc[...] = shk_ref[0]
        pv_sc[...] = shv_ref[0]
        plo_sc[...] = shlo_ref[0]

    def token_shift(x_ref, prev_sc, mu_ref):
        x = x_ref[0]
        row = lax.broadcasted_iota(jnp.int32, x.shape, 0)
        xprev = jnp.where(row == 0, jnp.broadcast_to(prev_sc[...], x.shape), pltpu.roll(x, 1, 0))
        prev_sc[...] = x[Tb - 1:Tb, :]
        return x + mu_ref[...] * (xprev - x)

    r = token_shift(r_ref, pr_sc, mur_ref)
    k = token_shift(k_ref, pk_sc, muk_ref)
    v = token_shift(v_ref, pv_sc, muv_ref)
    lo = token_shift(lo_ref, plo_sc, mulo_ref)
    xw = lo[:, 0:LANES]
    xa = lo[:, LANES:2 * LANES]
    xg = lo[:, 2 * LANES:LORA_PAD]

    u_dec = w0_ref[...] + _dot_split_w(jnp.tanh(xw), w2h_ref[...], w2l_ref[...])
    logw = (-math.exp(-0.5)) * jax.nn.sigmoid(u_dec)
    a = jax.nn.sigmoid(a0_ref[...] + _dot_split_w(xa, a2h_ref[...], a2l_ref[...]))
    g = _dot_split_w(jax.nn.sigmoid(xg), g2h_ref[...], g2l_ref[...])

    kk = k * kkw_ref[...]
    nrm = jnp.sqrt(_seg_sum(kk * kk, e_ref, et_ref))
    kappa = kk / jnp.maximum(nrm, 1e-12)
    k2 = k * (1.0 + (a - 1.0) * kaw_ref[...])
    bvec = kappa * a
    bonus = _seg_sum(r * k2 * rkw_ref[...], e_ref, et_ref) * v

    ti = lax.broadcasted_iota(jnp.int32, (Tb, Tb), 0)
    tj = lax.broadcasted_iota(jnp.int32, (Tb, Tb), 1)
    tril = jnp.where((tj <= ti) & ((ti >> log2l) == (tj >> log2l)), 1.0, 0.0).astype(BF16)
    cum = _dot_exact_lhs(tril, logw)
    trow = lax.broadcasted_iota(jnp.int32, (Tb, C_RWKV), 0)
    cum_end = jnp.broadcast_to(cum[L - 1:L, :], (Tb, C_RWKV))
    for s in range(1, n_sub):
        cum_end = jnp.where(trow >= s * L, jnp.broadcast_to(cum[(s + 1) * L - 1:(s + 1) * L, :], (Tb, C_RWKV)),
                            cum_end)
    gam = jnp.exp(cum)
    ginv = jnp.exp(-cum)
    gprev = jnp.exp(cum - logw)
    gtail = jnp.exp(cum_end - cum)

    kt = kappa * gprev
    kinv = k2 * ginv
    binv = bvec * ginv
    rt = r * gam
    khat = k2 * gtail
    bhat = bvec * gtail

    rr = lax.broadcasted_iota(jnp.int32, (GL, GL), 0)
    cc = lax.broadcasted_iota(jnp.int32, (GL, GL), 1)
    same = (rr >> log2l) == (cc >> log2l)
    tpos = rr & (L - 1)
    jpos = cc & (L - 1)
    mask_s = same & (jpos < tpos)
    mask_i = same & (jpos <= tpos)
    eye = rr == cc
    srow = lax.broadcasted_iota(jnp.int32, (GL, PACK), 0)
    slane = lax.broadcasted_iota(jnp.int32, (GL, PACK), 1)
    bmask = (srow >> log2l) == (slane >> int(math.log2(HEAD)))
    drow = lax.broadcasted_iota(jnp.int32, (PACK, PACK), 0)
    dcol = lax.broadcasted_iota(jnp.int32, (PACK, PACK), 1)
    deye = drow == dcol

    lane_split = GL % LANES == 0

    def bdot(a_, b_, dims=_BNN):
        return lax.dot_general(a_.astype(BF16), b_.astype(BF16), dims, preferred_element_type=F32)

    def btrans(x):
        return jnp.stack([x[gi].T for gi in range(N_GROUPS)], axis=0)

    def chunk_step(row0, st):
        def stack(x, dtype=F32):
            xc = x[row0:row0 + L].astype(dtype)
            x3 = jnp.stack([xc[:, gi * PACK:(gi + 1) * PACK] for gi in range(N_GROUPS)], axis=0)
            return jnp.where(bmask[None], jnp.concatenate([x3] * GROUP_HEADS, axis=1), jnp.zeros((), dtype))

        kt_s = stack(kt, BF16)
        rt_s = stack(rt)
        binv_s = stack(binv, BF16)
        kinv_s = stack(kinv, BF16)
        v_s = stack(v, BF16)
        khat_s = stack(khat)
        bhat_s = stack(bhat)

        if lane_split:
            a_all = bdot(jnp.concatenate([kt_s, rt_s.astype(BF16)], axis=1),
                         jnp.concatenate([binv_s, kinv_s], axis=1), _BNT)
            a_parts = (a_all[:, :GL, :GL], a_all[:, :GL, GL:], a_all[:, GL:, :GL], a_all[:, GL:, GL:])
        else:
            a_parts = (bdot(kt_s, binv_s, _BNT), bdot(kt_s, kinv_s, _BNT),
                       bdot(rt_s, binv_s, _BNT), bdot(rt_s, kinv_s, _BNT))
        n_mat = jnp.where(mask_s[None], a_parts[0], 0.0)
        a_kk = jnp.where(mask_s[None], a_parts[1], 0.0)
        a_br = jnp.where(mask_i[None], a_parts[2], 0.0)
        a_kr = jnp.where(mask_i[None], a_parts[3], 0.0)

        p_mat = -n_mat
        t_mat = jnp.where(eye[None], 1.0, 0.0) + p_mat
        if log2l > 1:
            p_mat = bdot(p_mat, p_mat)
        for lvl in range(1, log2l):
            if lvl == log2l - 1:
                t_mat = t_mat + bdot(p_mat, t_mat)
            elif lane_split:
                both = bdot(p_mat, jnp.concatenate([p_mat, t_mat], axis=2))
                t_mat = t_mat + both[:, :, GL:]
                p_mat = both[:, :, :GL]
            else:
                t_mat = t_mat + bdot(p_mat, t_mat)
                p_mat = bdot(p_mat, p_mat)

        av = bdot(jnp.concatenate([a_kk, a_kr], axis=1), v_s)
        wu = bdot(t_mat, jnp.concatenate([kt_s, av[:, :GL].astype(BF16)], axis=2))
        br = bdot(a_br, wu)
        q_s = rt_s - br[:, :, :PACK]
        y0_s = av[:, GL:] - br[:, :, PACK:]
        bhat_t = btrans(bhat_s)
        khat_t = btrans(khat_s)
        gam_end = jnp.exp(cum[row0 + L - 1:row0 + L, :])
        gl3 = jnp.stack([gam_end[:, gi * PACK:(gi + 1) * PACK] for gi in range(N_GROUPS)], axis=0)
        bw = bdot(bhat_t, wu)
        m_mat = jnp.where(deye[None], jnp.broadcast_to(gl3, (N_GROUPS, PACK, PACK)), 0.0) - bw[:, :, :PACK]
        c_mat = bdot(khat_t, v_s) - bw[:, :, PACK:]

        qm = bdot(jnp.concatenate([q_s, m_mat], axis=1), st)
        ys = qm[:, :GL] + y0_s
        yg = ys[:, 0:L]
        for h in range(1, GROUP_HEADS):
            yg = yg + ys[:, h * L:(h + 1) * L]
        return jnp.concatenate([yg[gi] for gi in range(N_GROUPS)], axis=1), qm[:, GL:] + c_mat

    st = st_sc[...]
    y_chunks = []
    for s in range(n_sub):
        y_c, st = chunk_step(s * L, st)
        y_chunks.append(y_c)
    st_sc[...] = st
    y = y_chunks[0] if n_sub == 1 else jnp.concatenate(y_chunks, axis=0)

    inv_head = 1.0 / HEAD
    mu = _seg_sum(y, e_ref, et_ref) * inv_head
    yc = y - mu
    var = _seg_sum(yc * yc, e_ref, et_ref) * inv_head
    yn = yc * lax.rsqrt(var + GN_EPS) * lng_ref[...] + lnb_ref[...]
    y_ref[0] = ((yn + bonus) * g).astype(y_ref.dtype)

    @pl.when(c == n_chunks - 1)
    def _():
        stout_ref[0] = st_sc[...]
        shout_ref[0, :, 0:C_RWKV] = pr_sc[...]
        shout_ref[0, :, C_RWKV:2 * C_RWKV] = pk_sc[...]
        shout_ref[0, :, 2 * C_RWKV:3 * C_RWKV] = pv_sc[...]
        shout_ref[0, :, 3 * C_RWKV:3 * C_RWKV + LORA_PAD] = plo_sc[...]


def _rwkv_mix(proj3, shift_parts, st0, params, chunk, block, cast_arrays=()):
    bsz, t_len, _ = proj3.shape
    L = block
    n_steps = bsz * (t_len // L)
    n_cast = len(cast_arrays)
    rkv_blk0 = 2 * C_CONV // C_RWKV
    lora_blk = (2 * C_CONV + 3 * C_RWKV) // LORA_PAD
    row = lambda n: pl.BlockSpec((1, n), lambda b, c: (0, 0))
    full = lambda s: pl.BlockSpec(s, lambda b, c: tuple(0 for _ in s))
    sh = lambda n: pl.BlockSpec((1, 1, n), lambda b, c: (b, 0, 0))
    in_specs = [
        pl.BlockSpec((1, L, C_RWKV), lambda b, c: (b, c, rkv_blk0)),
        pl.BlockSpec((1, L, C_RWKV), lambda b, c: (b, c, rkv_blk0 + 1)),
        pl.BlockSpec((1, L, C_RWKV), lambda b, c: (b, c, rkv_blk0 + 2)),
        pl.BlockSpec((1, L, LORA_PAD), lambda b, c: (b, c, lora_blk)),
        sh(C_RWKV), sh(C_RWKV), sh(C_RWKV), sh(LORA_PAD),
        pl.BlockSpec((1, N_GROUPS, PACK, PACK), lambda b, c: (b, 0, 0, 0)),
        row(C_RWKV), row(C_RWKV), row(C_RWKV), row(LORA_PAD),
        row(C_RWKV), full((LANES, C_RWKV)), full((LANES, C_RWKV)),
        row(C_RWKV), full((LANES, C_RWKV)), full((LANES, C_RWKV)),
        full((2 * LANES, C_RWKV)), full((2 * LANES, C_RWKV)),
        row(C_RWKV), row(C_RWKV), row(C_RWKV), row(C_RWKV), row(C_RWKV),
        full((C_RWKV, LANES)), full((LANES, C_RWKV)),
    ]
    out_shape = (jax.ShapeDtypeStruct((bsz, t_len, C_RWKV), BF16),
                 jax.ShapeDtypeStruct((bsz, N_GROUPS, PACK, PACK), F32),
                 jax.ShapeDtypeStruct((bsz, 1, 3 * C_RWKV + LORA_PAD), F32))
    out_specs = (pl.BlockSpec((1, L, C_RWKV), lambda b, c: (b, c, 0)),
                 pl.BlockSpec((1, N_GROUPS, PACK, PACK), lambda b, c: (b, 0, 0, 0)),
                 pl.BlockSpec((1, 1, 3 * C_RWKV + LORA_PAD), lambda b, c: (b, 0, 0)))
    steps_per_b = t_len // L
    for arr in cast_arrays:
        rows, width = arr.shape
        win = rows // n_steps
        assert win * n_steps == rows
        spec = pl.BlockSpec((win, width), lambda b, c: (b * steps_per_b + c, 0))
        in_specs = in_specs + [spec]
        out_specs = out_specs + (spec,)
        out_shape = out_shape + (jax.ShapeDtypeStruct((rows, width), BF16),)
    return pl.pallas_call(
        functools.partial(_rwkv_kernel, chunk, n_cast),
        out_shape=out_shape,
        grid=(bsz, t_len // L),
        in_specs=in_specs,
        out_specs=out_specs,
        scratch_shapes=[pltpu.VMEM((N_GROUPS, PACK, PACK), F32),
                        pltpu.VMEM((1, C_RWKV), F32), pltpu.VMEM((1, C_RWKV), F32),
                        pltpu.VMEM((1, C_RWKV), F32), pltpu.VMEM((1, LORA_PAD), F32)],
        compiler_params=pltpu.CompilerParams(dimension_semantics=("arbitrary", "arbitrary")),
        name="rwkv7_mix",
    )(proj3, proj3, proj3, proj3, *shift_parts, st0, *params, *cast_arrays)


def _outproj_kernel(c_ref, y_ref, x_ref, wa_ref, wb_ref, g_ref, b_ref, rwh_ref, rwl_ref, rb_ref,
                    x1_ref, idx_ref, gate_ref):
    mix = (jnp.dot(c_ref[...], wa_ref[...], preferred_element_type=F32)
           + jnp.dot(y_ref[...], wb_ref[...], preferred_element_type=F32))
    h = ALPHA * x_ref[...] + mix
    mu = jnp.mean(h, axis=-1, keepdims=True)
    hc = h - mu
    var = jnp.mean(hc * hc, axis=-1, keepdims=True)
    x1 = hc * lax.rsqrt(var + LN_EPS) * g_ref[...] + b_ref[...]
    x1_ref[...] = x1
    logits = _dot_split_w(x1, rwh_ref[...], rwl_ref[...]) + rb_ref[...]
    lane = lax.broadcasted_iota(jnp.int32, logits.shape, 1)
    idx_out = jnp.zeros(logits.shape, jnp.int32)
    val_out = jnp.zeros(logits.shape, F32)
    vals = []
    for kk in range(TOP_K):
        m = jnp.max(logits, axis=-1, keepdims=True)
        sel = jnp.min(jnp.where(logits == m, lane, ROUTER_PAD), axis=-1, keepdims=True)
        vals.append(m)
        idx_out = jnp.where(lane == kk, sel, idx_out)
        logits = jnp.where(lane == sel, -jnp.inf, logits)
    exps = [jnp.exp(vv - vals[0]) for vv in vals]
    denom = exps[0]
    for ee in exps[1:]:
        denom = denom + ee
    for kk in range(TOP_K):
        val_out = jnp.where(lane == kk, exps[kk] / denom, val_out)
    idx_ref[...] = idx_out
    gate_ref[...] = val_out


def _outproj_into_kernel(base_ref, *refs):
    del base_ref
    _outproj_kernel(*refs)


def _out_proj(c2, y2, x2, wa, wb, ln_g, ln_b, rw_hi, rw_lo, rb, tm, n_total, x1_base=None, row0=0):
    n = x2.shape[0]
    blk0 = row0 // tm
    assert blk0 * tm == row0
    row = lambda w: pl.BlockSpec((1, w), lambda i: (0, 0))
    in_specs = [pl.BlockSpec((tm, C_CONV), lambda i: (i, 0)),
                pl.BlockSpec((tm, C_RWKV), lambda i: (i, 0)),
                pl.BlockSpec((tm, D_MODEL), lambda i: (i, 0)),
                pl.BlockSpec((C_CONV, D_MODEL), lambda i: (0, 0)),
                pl.BlockSpec((C_RWKV, D_MODEL), lambda i: (0, 0)),
                row(D_MODEL), row(D_MODEL),
                pl.BlockSpec((D_MODEL, ROUTER_PAD), lambda i: (0, 0)),
                pl.BlockSpec((D_MODEL, ROUTER_PAD), lambda i: (0, 0)),
                row(ROUTER_PAD)]
    args = (c2, y2, x2, wa, wb, ln_g, ln_b, rw_hi, rw_lo, rb)
    body, aliases = _outproj_kernel, {}
    if x1_base is not None:
        in_specs = [pl.BlockSpec(memory_space=pl.ANY)] + in_specs
        args = (x1_base,) + args
        body, aliases = _outproj_into_kernel, {0: 0}
    return pl.pallas_call(
        body,
        out_shape=(jax.ShapeDtypeStruct((n_total, D_MODEL), F32),
                   jax.ShapeDtypeStruct((n, ROUTER_PAD), jnp.int32),
                   jax.ShapeDtypeStruct((n, ROUTER_PAD), F32)),
        grid=(n // tm,),
        in_specs=in_specs,
        out_specs=(pl.BlockSpec((tm, D_MODEL), lambda i: (i + blk0, 0)),
                   pl.BlockSpec((tm, ROUTER_PAD), lambda i: (i, 0)),
                   pl.BlockSpec((tm, ROUTER_PAD), lambda i: (i, 0))),
        input_output_aliases=aliases,
        compiler_params=pltpu.CompilerParams(dimension_semantics=("arbitrary",)),
        name="out_proj_ln_router",
    )(*args)


DMA_PRIORITIES = 2


def _expert_kernel(bexp_ref, nused_ref, tokc_ref, tokn_ref, x_hbm, wg_ref, bg_ref, wu_ref, bu_ref, wd_ref, bd_ref,
                   o_ref, xg_ref, xb_ref, sems):
    del bexp_ref
    i = pl.program_id(0)
    j = pl.program_id(1)
    nj = pl.num_programs(1)
    n_used = nused_ref[0]
    rows = xb_ref.shape[0]
    slot = lax.rem(i, 2)

    def row_copy(tok_ref, s, g, sub):
        tok = tok_ref[0, 0, g * SUBLANES + sub]
        return pltpu.make_async_copy(x_hbm.at[pl.ds(tok, 1)], xg_ref.at[s, g, pl.ds(sub, 1)], sems.at[s])

    def issue(tok_ref, s, g0, n_groups):
        def body(g, carry):
            for sub in range(SUBLANES):
                row_copy(tok_ref, s, g0 + g, sub).start(priority=sub % DMA_PRIORITIES)
            return carry

        lax.fori_loop(0, n_groups, body, 0)

    groups = rows // SUBLANES

    @pl.when((i == 0) & (j == 0) & (n_used > 0))
    def _():
        issue(tokc_ref, 0, 0, groups)

    @pl.when(i + 1 < n_used)
    def _():
        per_step = groups // MOE_NJ
        issue(tokn_ref, 1 - slot, j * per_step, per_step)

    @pl.when(i < n_used)
    def _():
        @pl.when(j == 0)
        def _():
            def drain(g, carry):
                for sub in range(SUBLANES):
                    row_copy(tokc_ref, slot, g, sub).wait()
                return carry

            lax.fori_loop(0, groups, drain, 0)
            xb_ref[...] = xg_ref[slot].reshape(rows, D_MODEL).astype(BF16)
            o_ref[...] = jnp.broadcast_to(bd_ref[0], o_ref.shape)

        x = xb_ref[...]
        gate = jnp.minimum(jnp.dot(x, wg_ref[0], preferred_element_type=F32) + bg_ref[0, pl.ds(j, 1), :],
                           SWIGLU_LIMIT)
        up = jnp.clip(jnp.dot(x, wu_ref[0], preferred_element_type=F32) + bu_ref[0, pl.ds(j, 1), :],
                      -SWIGLU_LIMIT, SWIGLU_LIMIT)
        hmid = (up + 1.0) * gate * jax.nn.sigmoid(SWIGLU_ALPHA * gate)
        o_ref[...] += jnp.dot(hmid.astype(BF16), wd_ref[0], preferred_element_type=F32)

    @pl.when((i >= nused_ref[0]) & (j == nj - 1))
    def _():
        o_ref[...] = jnp.zeros_like(o_ref)


def _experts(block_exp, n_used, row_tok3, x, wg, bg, wu, bu, wd, bd):
    nb, _, rows = row_tok3.shape
    n_exp = wg.shape[0]
    nj = MOE_NJ
    tf = D_FF // nj

    def jsel(i, j, nu):
        return jnp.where(i < nu[0], j, nj - 1)

    tok_spec = lambda off: pl.BlockSpec((1, 1, rows), lambda i, j, be, nu: (jnp.minimum(i + off, nb - 1), 0, 0),
                                        memory_space=pltpu.SMEM)
    grid_spec = pltpu.PrefetchScalarGridSpec(
        num_scalar_prefetch=2,
        grid=(nb, nj),
        in_specs=[tok_spec(0), tok_spec(1),
                  pl.BlockSpec(memory_space=pl.ANY),
                  pl.BlockSpec((1, D_MODEL, tf), lambda i, j, be, nu: (be[i], 0, jsel(i, j, nu))),
                  pl.BlockSpec((1, nj, tf), lambda i, j, be, nu: (be[i], 0, 0)),
                  pl.BlockSpec((1, D_MODEL, tf), lambda i, j, be, nu: (be[i], 0, jsel(i, j, nu))),
                  pl.BlockSpec((1, nj, tf), lambda i, j, be, nu: (be[i], 0, 0)),
                  pl.BlockSpec((1, tf, D_MODEL), lambda i, j, be, nu: (be[i], jsel(i, j, nu), 0)),
                  pl.BlockSpec((1, 1, D_MODEL), lambda i, j, be, nu: (be[i], 0, 0))],
        out_specs=pl.BlockSpec((rows, D_MODEL), lambda i, j, be, nu: (i, 0)),
        scratch_shapes=[pltpu.VMEM((2, rows // SUBLANES, SUBLANES, D_MODEL), x.dtype),
                        pltpu.VMEM((rows, D_MODEL), BF16), pltpu.SemaphoreType.DMA((2,))],
    )
    return pl.pallas_call(
        _expert_kernel,
        out_shape=jax.ShapeDtypeStruct((nb * rows, D_MODEL), F32),
        grid_spec=grid_spec,
        compiler_params=pltpu.CompilerParams(dimension_semantics=("arbitrary", "arbitrary")),
        name="moe_experts",
    )(block_exp, n_used, row_tok3, row_tok3, x, wg, bg.reshape(n_exp, nj, tf), wu, bu.reshape(n_exp, nj, tf),
      wd, bd.reshape(n_exp, 1, D_MODEL))


def _combine_kernel(dest_ref, gate_ref, x1_ref, g_ref, b_ref, yb_hbm, o_ref, buf_ref, sem):
    tc = x1_ref.shape[0]

    def row_copy(d, kk, g, sub):
        return pltpu.make_async_copy(yb_hbm.at[pl.ds(d, 1)], buf_ref.at[kk, g, pl.ds(sub, 1)], sem)

    def issue(g, carry):
        for sub in range(SUBLANES):
            for kk in range(TOP_K):
                d = dest_ref[0, 0, (g * SUBLANES + sub) * TOP_K + kk]
                row_copy(d, kk, g, sub).start(priority=kk % DMA_PRIORITIES)
        return carry

    lax.fori_loop(0, tc // SUBLANES, issue, 0)

    def drain(g, carry):
        for sub in range(SUBLANES):
            for kk in range(TOP_K):
                row_copy(0, kk, g, sub).wait()
        return carry

    lax.fori_loop(0, tc // SUBLANES, drain, 0)

    gates = gate_ref[...]
    moe = gates[:, 0:1] * buf_ref[0].reshape(tc, D_MODEL)
    for kk in range(1, TOP_K):
        moe = moe + gates[:, kk:kk + 1] * buf_ref[kk].reshape(tc, D_MODEL)
    h = ALPHA * x1_ref[...] + moe
    mu = jnp.mean(h, axis=-1, keepdims=True)
    hc = h - mu
    var = jnp.mean(hc * hc, axis=-1, keepdims=True)
    o_ref[...] = hc * lax.rsqrt(var + LN_EPS) * g_ref[...] + b_ref[...]


def _combine(dest3, gates, x1, ln_g, ln_b, yb, tc, row0=0):
    n = dest3.shape[0] * tc
    blk0 = row0 // tc
    assert blk0 * tc == row0
    return pl.pallas_call(
        _combine_kernel,
        out_shape=jax.ShapeDtypeStruct((n, D_MODEL), F32),
        grid=(n // tc,),
        in_specs=[pl.BlockSpec((1, 1, tc * TOP_K), lambda i: (i, 0, 0), memory_space=pltpu.SMEM),
                  pl.BlockSpec((tc, ROUTER_PAD), lambda i: (i, 0)),
                  pl.BlockSpec((tc, D_MODEL), lambda i: (i + blk0, 0)),
                  pl.BlockSpec((1, D_MODEL), lambda i: (0, 0)),
                  pl.BlockSpec((1, D_MODEL), lambda i: (0, 0)),
                  pl.BlockSpec(memory_space=pl.ANY)],
        out_specs=pl.BlockSpec((tc, D_MODEL), lambda i: (i, 0)),
        scratch_shapes=[pltpu.VMEM((TOP_K, tc // SUBLANES, SUBLANES, D_MODEL), F32), pltpu.SemaphoreType.DMA],
        compiler_params=pltpu.CompilerParams(dimension_semantics=("arbitrary",)),
        name="moe_combine_ln2",
    )(dest3, gates, x1, ln_g, ln_b, yb)


def _pad_cols(w, width):
    return jnp.pad(w, ((0, 0), (0, width - w.shape[1])))


def _pad_rows(w, height):
    return jnp.pad(w, ((0, height - w.shape[0]), (0, 0)))


def _split_lora_cols(w):
    xw = w[..., 0:R_DECAY]
    xa = w[..., R_DECAY:R_DECAY + R_ICLR]
    xg = w[..., R_DECAY + R_ICLR:]
    pad = lambda x, n: jnp.pad(x, [(0, 0)] * (x.ndim - 1) + [(0, n - x.shape[-1])])
    return jnp.concatenate([pad(xw, LANES), pad(xa, LANES), pad(xg, 2 * LANES)], axis=-1)


def _pick(n, prefs):
    for p in prefs:
        if n % p == 0:
            return p
    return n


def _mixer_group(x, conv_buf, shift_buf, wkv_state, wts, cast_along=()):
    bsz, t_len, _ = x.shape
    n = bsz * t_len
    proj = _in_proj(x.reshape(n, D_MODEL), wts["w_in"], wts["b_in"], _pick(n, (IN_TM, 256, 128)), IN_TN)
    proj3 = proj.reshape(bsz, t_len, P_PAD)

    hist = jnp.pad(conv_buf, ((0, 0), (HIST - (CONV_WIDTH - 1), 0), (0, 0)))
    c, tail = _conv_module(proj3, hist, wts["conv_w"], wts["conv_b"], wts["conv_ln_g"], wts["conv_ln_b"],
                           _pick(t_len, (256, 128, 64, 32, 16, 8)))
    new_conv = tail[:, HIST - (CONV_WIDTH - 1):, :]

    sh_rkv = shift_buf[:, :, :3 * C_RWKV]
    sh_lo = _split_lora_cols(shift_buf[:, :, 3 * C_RWKV:])
    shift_parts = (sh_rkv[:, :, 0:C_RWKV], sh_rkv[:, :, C_RWKV:2 * C_RWKV], sh_rkv[:, :, 2 * C_RWKV:], sh_lo)
    st_t = jnp.swapaxes(wkv_state, -1, -2).reshape(bsz, N_GROUPS, GROUP_HEADS, HEAD, HEAD)
    eye_h = jnp.eye(GROUP_HEADS, dtype=F32)
    st0 = jnp.einsum("bghkv,hj->bghkjv", st_t, eye_h).reshape(bsz, N_GROUPS, PACK, PACK)
    chunk = _pick(t_len, (RWKV_CHUNK, 32, 16))
    block = RWKV_BLOCK if t_len % RWKV_BLOCK == 0 else chunk
    n_steps = bsz * (t_len // block)
    ride = tuple(w for w in cast_along if w.shape[0] % n_steps == 0 and (w.shape[0] // n_steps) % 16 == 0)
    yb, st_out, sh_out, *cast_done = _rwkv_mix(proj3, shift_parts, st0, wts["rwkv_params"], chunk, block, ride)
    if len(ride) != len(cast_along):
        cast_done = [w.astype(BF16) for w in cast_along]
    st5 = st_out.reshape(bsz, N_GROUPS, GROUP_HEADS, HEAD, GROUP_HEADS, HEAD)
    st_diag = jnp.einsum("bghkhv->bghkv", st5)
    new_wkv = jnp.swapaxes(st_diag, -1, -2).reshape(bsz, N_HEADS, HEAD, HEAD)
    lo = sh_out[:, :, 3 * C_RWKV:]
    new_shift = jnp.concatenate([sh_out[:, :, :3 * C_RWKV], lo[:, :, 0:R_DECAY], lo[:, :, LANES:LANES + R_ICLR],
                                 lo[:, :, 2 * LANES:2 * LANES + R_GATE]], axis=-1)
    return c.reshape(n, C_CONV), yb.reshape(n, C_RWKV), new_conv, new_shift, new_wkv, cast_done


def _route(top_idx, n_tok):
    n_assign = n_tok * TOP_K
    flat_e = top_idx.reshape(-1)
    onehot = (flat_e[:, None] == jnp.arange(N_EXPERTS, dtype=jnp.int32)[None, :]).astype(jnp.int32)
    csum = jnp.cumsum(onehot, axis=0)
    rank = jnp.take_along_axis(csum, flat_e[:, None], axis=1)[:, 0] - 1
    counts = csum[-1]
    padded = (counts + MOE_TM - 1) // MOE_TM * MOE_TM
    seg_end = jnp.cumsum(padded)
    seg_start = seg_end - padded
    dest = (seg_start[flat_e] + rank).astype(jnp.int32)
    n_rows = (n_assign + N_EXPERTS * (MOE_TM - 1) + MOE_TM - 1) // MOE_TM * MOE_TM
    n_blocks = n_rows // MOE_TM
    block_start = jnp.arange(n_blocks, dtype=jnp.int32) * MOE_TM
    block_exp = jnp.minimum(jnp.sum((seg_end[None, :] <= block_start[:, None]).astype(jnp.int32), axis=1),
                            N_EXPERTS - 1).astype(jnp.int32)
    order = jnp.argsort(flat_e, stable=True).astype(jnp.int32)
    start = jnp.cumsum(counts) - counts
    local = jnp.arange(MOE_TM, dtype=jnp.int32)[None, :] + (block_start - seg_start[block_exp])[:, None]
    valid = local < counts[block_exp][:, None]
    pos = jnp.clip(start[block_exp][:, None] + local, 0, n_assign - 1)
    row_tok = jnp.where(valid, order[pos.reshape(-1)].reshape(n_blocks, MOE_TM) // TOP_K, 0).reshape(-1)
    n_used = (seg_end[-1] // MOE_TM).astype(jnp.int32).reshape(1)
    return dest, row_tok, block_exp, n_used, n_blocks


def kernel(x_prompt, x_sample, state_conv, state_shift, state_wkv, w_in, b_in, mu_shift, conv_w, conv_b,
           conv_ln_g, conv_ln_b, rwkv_w0, rwkv_w2, rwkv_a0, rwkv_a2, rwkv_g2, rwkv_k_k, rwkv_k_a, rwkv_r_k,
           rwkv_ln_g, rwkv_ln_b, w_out, ln1_g, ln1_b, router_w, router_b, w_gate, b_gate, w_up, b_up,
           w_down, b_down, ln2_g, ln2_b):
    assert w_in.shape[0] == 1, "single layer"
    d = 0
    row = lambda v: v.reshape(1, -1)
    n_p, t_p, _ = x_prompt.shape
    n_s, t_s, _ = x_sample.shape

    w_rkv = w_in[d][:, 2 * C_CONV:2 * C_CONV + 3 * C_RWKV]
    w_lo = _split_lora_cols(w_in[d][:, 2 * C_CONV + 3 * C_RWKV:])
    w_in_p = jnp.concatenate([w_in[d][:, :2 * C_CONV], w_rkv, w_lo], axis=1).astype(BF16)
    b_in_p = jnp.concatenate([b_in[d][None, :2 * C_CONV], b_in[d][None, 2 * C_CONV:2 * C_CONV + 3 * C_RWKV],
                              _split_lora_cols(b_in[d][None, 2 * C_CONV + 3 * C_RWKV:])], axis=1)
    mu = mu_shift[d][None, :]
    mu_lo = _split_lora_cols(mu[:, 3 * C_RWKV:])
    head_of_lane = jnp.arange(C_RWKV, dtype=jnp.int32) // HEAD
    e_mat = (head_of_lane[:, None] == jnp.arange(LANES, dtype=jnp.int32)[None, :]).astype(BF16)

    def hi_lo(w, height):
        w = _pad_rows(w, height)
        w_hi = w.astype(BF16)
        return w_hi, (w - w_hi.astype(F32)).astype(BF16)

    rwkv_params = (
        mu[:, 0:C_RWKV], mu[:, C_RWKV:2 * C_RWKV], mu[:, 2 * C_RWKV:3 * C_RWKV], mu_lo,
        row(rwkv_w0[d]), *hi_lo(rwkv_w2[d], LANES), row(rwkv_a0[d]), *hi_lo(rwkv_a2[d], LANES),
        *hi_lo(rwkv_g2[d], 2 * LANES),
        row(rwkv_k_k[d]), row(rwkv_k_a[d]), row(rwkv_r_k[d]), row(rwkv_ln_g[d]), row(rwkv_ln_b[d]),
        e_mat, e_mat.T,
    )
    wts = dict(w_in=w_in_p, b_in=b_in_p, conv_w=conv_w[d], conv_b=row(conv_b[d]),
               conv_ln_g=row(conv_ln_g[d]), conv_ln_b=row(conv_ln_b[d]), rwkv_params=rwkv_params)

    zero_conv = jnp.zeros((n_p, CONV_WIDTH - 1, C_CONV), x_prompt.dtype)
    zero_shift = jnp.zeros((n_p, 1, N_SHIFT), x_prompt.dtype)
    zero_wkv = jnp.zeros((n_p, N_HEADS, HEAD, HEAD), state_wkv.dtype)
    expert_w = (w_gate[d].reshape(-1, D_FF), w_up[d].reshape(-1, D_FF), w_down[d].reshape(-1, D_MODEL))
    c_p, y_p, conv_p, shift_p, wkv_p, expert_w = _mixer_group(x_prompt, zero_conv, zero_shift, zero_wkv, wts,
                                                              expert_w)
    wg_b = expert_w[0].reshape(N_EXPERTS, D_MODEL, D_FF)
    wu_b = expert_w[1].reshape(N_EXPERTS, D_MODEL, D_FF)
    wd_b = expert_w[2].reshape(N_EXPERTS, D_FF, D_MODEL)
    c_s, y_s, conv_s, shift_s, wkv_s, _ = _mixer_group(x_sample, state_conv[d], state_shift[d], state_wkv[d], wts)

    w_out_b = w_out[d].astype(BF16)
    rw = _pad_cols(router_w[d], ROUTER_PAD)
    rw_hi = rw.astype(BF16)
    rw_lo = (rw - rw_hi.astype(F32)).astype(BF16)
    rb = jnp.concatenate([router_b[d], jnp.full((ROUTER_PAD - N_EXPERTS,), -jnp.inf, F32)])[None, :]

    n_tok_p = n_p * t_p
    n_tok_s = n_s * t_s
    n_tok = n_tok_p + n_tok_s

    def out_proj(c2, y2, x3, x1_base, row0):
        n = c2.shape[0]
        tm = math.gcd(_pick(n, (OUT_TM, 256, 128)), row0) if row0 else _pick(n, (OUT_TM, 256, 128))
        return _out_proj(c2, y2, x3.reshape(n, D_MODEL), w_out_b[:C_CONV], w_out_b[C_CONV:], row(ln1_g[d]),
                         row(ln1_b[d]), rw_hi, rw_lo, rb, tm, n_tok, x1_base, row0)

    x1, idx_p, gate_p = out_proj(c_p, y_p, x_prompt, None, 0)
    x1, idx_s, gate_s = out_proj(c_s, y_s, x_sample, x1, n_tok_p)
    top_idx = jnp.concatenate([idx_p[:, :TOP_K], idx_s[:, :TOP_K]], axis=0)
    dest, row_tok, block_exp, n_used, n_blocks = _route(top_idx, n_tok)
    yb = _experts(block_exp, n_used, row_tok.reshape(n_blocks, 1, MOE_TM), x1, wg_b, b_gate[d], wu_b, b_up[d],
                  wd_b, b_down[d])

    def combine(dest_g, gate_g, n, row0):
        tc = _pick(n, (512, 256, 128, 64, 32, 16, 8))
        tc = math.gcd(tc, row0) if row0 else tc
        return _combine(dest_g.reshape(n // tc, 1, tc * TOP_K), gate_g, x1, row(ln2_g[d]), row(ln2_b[d]), yb, tc,
                        row0)

    y_prompt = combine(dest[:n_tok_p * TOP_K], gate_p, n_tok_p, 0).reshape(n_p, t_p, D_MODEL)
    y_sample = combine(dest[n_tok_p * TOP_K:], gate_s, n_tok_s, n_tok_p).reshape(n_s, t_s, D_MODEL)
    return (y_prompt, y_sample, conv_p[None], shift_p[None], wkv_p[None], conv_s[None], shift_s[None], wkv_s[None])
```
